```python
import math
import jax, jax.numpy as jnp
from jax import lax
import numpy as np

D_MODEL = 2048
BATCH = 1
SEQ = 8192
DEPTH = 2

RMS_EPS = 1e-6
S5_GROUP = 16
S5_GROUPS = D_MODEL // S5_GROUP
S5_STATE = 64
S5_DT_MIN = 1e-3
S5_DT_MAX = 1e-1
HEAD_DIM = 128
N_Q_HEADS = D_MODEL // HEAD_DIM
N_KV_HEADS = N_Q_HEADS // 4
GQA_GROUP = N_Q_HEADS // N_KV_HEADS
ROPE_DIM = HEAD_DIM // 4
ROPE_THETA = 500000.0
CMP_BLOCK = 32
CMP_STRIDE = 16
CMP_HIDDEN = 2 * HEAD_DIM
SEL_BLOCK = 64
SEL_TOPK = 16
SEL_LOCAL = 2
WINDOW = 512
Q_BLOCK = 128
N_BRANCH = 3
Q_DIM = N_Q_HEADS * HEAD_DIM
KV_DIM = N_KV_HEADS * HEAD_DIM
NSA_IN = Q_DIM + 6 * KV_DIM + N_BRANCH * N_Q_HEADS
NSA_SPLITS = tuple(Q_DIM + i * KV_DIM for i in range(7))
NEG = -1e30
D_FF = 5632
N_EXPERTS = 8
TOP_K = 2
EXPERT_FF = 5632
MOE_ROW_BLOCK = 256

kernel_name = "hybrid_s5_nsa_moe_trunk"


def rms_norm(x, gain):
    xf = x.astype(jnp.float32)
    y = xf * lax.rsqrt(jnp.mean(xf * xf, axis=-1, keepdims=True) + RMS_EPS)
    return (y * gain.astype(jnp.float32)).astype(x.dtype)


def rope_partial(x, pos):
    half = ROPE_DIM // 2
    inv = jnp.power(ROPE_THETA, -jnp.arange(half, dtype=jnp.float32) / half)
    ang = pos.astype(jnp.float32)[..., None] * inv
    cos = jnp.cos(ang)[:, :, None, :]
    sin = jnp.sin(ang)[:, :, None, :]
    xr = x[..., :ROPE_DIM].astype(jnp.float32)
    x1, x2 = xr[..., :half], xr[..., half:]
    rot = jnp.concatenate([x1 * cos - x2 * sin, x2 * cos + x1 * sin], axis=-1).astype(x.dtype)
    return jnp.concatenate([rot, x[..., ROPE_DIM:]], axis=-1)


def masked_softmax(s, mask):
    sm = jnp.where(mask, s, NEG)
    m = jnp.max(sm, axis=-1, keepdims=True)
    e = jnp.where(mask, jnp.exp(sm - m), 0.0)
    return e / jnp.maximum(jnp.sum(e, axis=-1, keepdims=True), 1e-30)


def _ssm_combine(e1, e2):
    a1r, a1i, b1r, b1i = e1
    a2r, a2i, b2r, b2i = e2
    return (a2r * a1r - a2i * a1i,
            a2r * a1i + a2i * a1r,
            a2r * b1r - a2i * b1i + b2r,
            a2r * b1i + a2i * b1r + b2i)


def s5_mixer(u, a_re, a_im, log_step, b_re, b_im, c_re, c_im, d_skip, w_glu):
    f32 = jnp.float32
    Bsz, L, _ = u.shape
    dt = jnp.exp(log_step.astype(f32))[:, None]
    ar = a_re.astype(f32)
    ai = a_im.astype(f32)
    mag = jnp.exp(ar * dt)
    lb_re = mag * jnp.cos(ai * dt)
    lb_im = mag * jnp.sin(ai * dt)
    den = ar * ar + ai * ai
    nr = lb_re - 1.0
    coef_re = (nr * ar + lb_im * ai) / den
    coef_im = (lb_im * ar - nr * ai) / den
    ug = u.astype(f32).reshape(Bsz, L, S5_GROUPS, S5_GROUP)
    bu_re = jnp.einsum('blgc,gpc->blgp', ug, b_re.astype(f32))
    bu_im = jnp.einsum('blgc,gpc->blgp', ug, b_im.astype(f32))
    x_re = coef_re * bu_re - coef_im * bu_im
    x_im = coef_re * bu_im + coef_im * bu_re
    a_el_re = jnp.broadcast_to(lb_re, x_re.shape)
    a_el_im = jnp.broadcast_to(lb_im, x_re.shape)
    _, _, h_re, h_im = lax.associative_scan(_ssm_combine, (a_el_re, a_el_im, x_re, x_im), axis=1)
    y = (jnp.einsum('blgp,gcp->blgc', h_re, c_re.astype(f32))
         - jnp.einsum('blgp,gcp->blgc', h_im, c_im.astype(f32)))
    y = y.reshape(Bsz, L, D_MODEL) + d_skip.astype(f32) * u.astype(f32)
    g = jax.nn.gelu(y).astype(u.dtype)
    hg = g @ w_glu
    return hg[..., :D_MODEL] * jax.nn.sigmoid(hg[..., D_MODEL:])


def compress_blocks(t, pe, w1, w2):
    Bsz, L, H, hd = t.shape
    nl = CMP_BLOCK // CMP_STRIDE
    n_cmp = L // CMP_STRIDE - nl + 1
    c = t.reshape(Bsz, L // CMP_STRIDE, CMP_STRIDE, H, hd)
    blocks = jnp.concatenate([c[:, m:m + n_cmp] for m in range(nl)], axis=2)
    blocks = blocks + pe[:, None, :].astype(t.dtype)
    flat = blocks.transpose(0, 1, 3, 2, 4).reshape(Bsz, n_cmp, H, CMP_BLOCK * hd)
    return jax.nn.gelu(flat @ w1) @ w2


def nsa_mixer(u, positions, w_in, q_gain, k_gain, pe_k, pe_v, ck_w1, ck_w2, cv_w1, cv_w2, w_out):
    f32 = jnp.float32
    Bsz, L, _ = u.shape
    proj = u @ w_in
    q, kc, vc, ksl, vsl, kw, vw, g = jnp.split(proj, NSA_SPLITS, axis=-1)

    def heads(t, n):
        return t.reshape(Bsz, L, n, HEAD_DIM)

    q = rope_partial(rms_norm(heads(q, N_Q_HEADS), q_gain), positions)
    ksl = rope_partial(rms_norm(heads(ksl, N_KV_HEADS), k_gain[1]), positions)
    kw = rope_partial(rms_norm(heads(kw, N_KV_HEADS), k_gain[2]), positions)
    vsl = heads(vsl, N_KV_HEADS)
    vw = heads(vw, N_KV_HEADS)

    nl = CMP_BLOCK // CMP_STRIDE
    n_cmp = L // CMP_STRIDE - nl + 1
    kcmp = compress_blocks(heads(kc, N_KV_HEADS), pe_k, ck_w1, ck_w2)
    vcmp = compress_blocks(heads(vc, N_KV_HEADS), pe_v, cv_w1, cv_w2)
    pos_cmp = positions[:, CMP_BLOCK - 1::CMP_STRIDE][:, :n_cmp]
    kcmp = rope_partial(rms_norm(kcmp, k_gain[0]), pos_cmp).transpose(0, 2, 1, 3)
    vcmp = vcmp.transpose(0, 2, 1, 3)
    cmp_end = jnp.arange(n_cmp) * CMP_STRIDE + (CMP_BLOCK - 1)

    n_sel = L // SEL_BLOCK
    k_top = min(SEL_TOPK, n_sel)
    ksb = ksl.reshape(Bsz, n_sel, SEL_BLOCK, N_KV_HEADS, HEAD_DIM).transpose(0, 3, 1, 2, 4)
    vsb = vsl.reshape(Bsz, n_sel, SEL_BLOCK, N_KV_HEADS, HEAD_DIM).transpose(0, 3, 1, 2, 4)

    kwp = jnp.pad(kw.transpose(0, 2, 1, 3), ((0, 0), (0, 0), (WINDOW, 0), (0, 0)))
    vwp = jnp.pad(vw.transpose(0, 2, 1, 3), ((0, 0), (0, 0), (WINDOW, 0), (0, 0)))

    qh = q.reshape(Bsz, L, N_KV_HEADS, GQA_GROUP, HEAD_DIM).transpose(0, 2, 3, 1, 4)
    gates = jax.nn.sigmoid(g.astype(f32)).reshape(Bsz, L, N_BRANCH, N_KV_HEADS, GQA_GROUP)
    gates = gates.transpose(2, 0, 3, 4, 1)

    r = SEL_BLOCK // CMP_STRIDE
    pad_right = max(0, r * n_sel + nl - 1 - n_cmp)
    b_ix = jnp.arange(Bsz)[:, None, None, None]
    h_ix = jnp.arange(N_KV_HEADS)[None, :, None, None]
    blk = jnp.arange(n_sel)
    scale = HEAD_DIM ** -0.5

    def q_block(qb):
        t0 = qb * Q_BLOCK
        t_idx = t0 + jnp.arange(Q_BLOCK)
        qi = lax.dynamic_slice_in_dim(qh, t0, Q_BLOCK, axis=3)
        gi = lax.dynamic_slice_in_dim(gates, t0, Q_BLOCK, axis=4)[..., None]

        s_c = jnp.einsum('bhgtd,bhnd->bhgtn', qi, kcmp, preferred_element_type=f32) * scale
        p_c = masked_softmax(s_c, cmp_end[None, :] <= t_idx[:, None])
        o_c = jnp.einsum('bhgtn,bhnd->bhgtd', p_c.astype(vcmp.dtype), vcmp)

        imp = jnp.pad(jnp.sum(p_c, axis=2), ((0, 0), (0, 0), (0, 0), (0, pad_right)))
        p_slc = jnp.zeros(imp.shape[:-1] + (n_sel,), f32)
        for m in range(r):
            for n in range(nl):
                p_slc = p_slc + imp[..., m + n::r][..., :n_sel]
        cur = t_idx // SEL_BLOCK
        dist = cur[:, None] - blk[None, :]
        forced = (blk[None, :] == 0) | ((dist >= 0) & (dist < SEL_LOCAL))
        score = jnp.where(forced, jnp.inf, jnp.where(dist >= 0, p_slc, -jnp.inf))
        _, sel = lax.top_k(score, k_top)

        ks_g = ksb[b_ix, h_ix, sel].reshape(Bsz, N_KV_HEADS, Q_BLOCK, k_top * SEL_BLOCK, HEAD_DIM)
        vs_g = vsb[b_ix, h_ix, sel].reshape(Bsz, N_KV_HEADS, Q_BLOCK, k_top * SEL_BLOCK, HEAD_DIM)
        pos_s = (sel[..., None] * SEL_BLOCK + jnp.arange(SEL_BLOCK)).reshape(Bsz, N_KV_HEADS, Q_BLOCK, -1)
        s_s = jnp.einsum('bhgtd,bhtkd->bhgtk', qi, ks_g, preferred_element_type=f32) * scale
        p_s = masked_softmax(s_s, (pos_s <= t_idx[:, None])[:, :, None])
        o_s = jnp.einsum('bhgtk,bhtkd->bhgtd', p_s.astype(vs_g.dtype), vs_g)

        kwi = lax.dynamic_slice_in_dim(kwp, t0, Q_BLOCK + WINDOW, axis=2)
        vwi = lax.dynamic_slice_in_dim(vwp, t0, Q_BLOCK + WINDOW, axis=2)
        s_pos = t0 - WINDOW + jnp.arange(Q_BLOCK + WINDOW)
        rel = t_idx[:, None] - s_pos[None, :]
        mask_w = (rel >= 0) & (rel < WINDOW) & (s_pos[None, :] >= 0)
        s_w = jnp.einsum('bhgtd,bhsd->bhgts', qi, kwi, preferred_element_type=f32) * scale
        p_w = masked_softmax(s_w, mask_w)
        o_w = jnp.einsum('bhgts,bhsd->bhgtd', p_w.astype(vwi.dtype), vwi)

        return gi[0] * o_c + gi[1] * o_s + gi[2] * o_w

    o = lax.map(q_block, jnp.arange(L // Q_BLOCK))
    o = o.transpose(1, 0, 4, 2, 3, 5).reshape(Bsz, L, Q_DIM)
    return o.astype(u.dtype) @ w_out


def swiglu(u, w_gu, w_down):
    f = w_down.shape[0]
    gu = u @ w_gu
    return (jax.nn.silu(gu[..., :f]) * gu[..., f:]) @ w_down


def moe_ffn(u, w_router, b_router, w_gu, w_down):
    f32 = jnp.float32
    Bsz, L, D = u.shape
    xt = u.reshape(-1, D)
    n_tok = xt.shape[0]
    logits = (xt @ w_router).astype(f32) + b_router.astype(f32)
    top_v, top_e = lax.top_k(logits, TOP_K)
    top_w = jax.nn.softmax(top_v, axis=-1)
    e_flat = top_e.reshape(-1)
    w_flat = top_w.reshape(-1)
    t_flat = jnp.repeat(jnp.arange(n_tok, dtype=jnp.int32), TOP_K)
    order = jnp.argsort(e_flat)
    e_s, t_s, w_s = e_flat[order], t_flat[order], w_flat[order]
    counts = jnp.bincount(e_flat, length=N_EXPERTS)
    start = jnp.cumsum(counts) - counts
    padded = (counts + MOE_ROW_BLOCK - 1) // MOE_ROW_BLOCK * MOE_ROW_BLOCK
    pad_end = jnp.cumsum(padded)
    pad_start = pad_end - padded
    n_assign = n_tok * TOP_K
    dest = pad_start[e_s] + jnp.arange(n_assign) - start[e_s]
    n_rows = n_assign + N_EXPERTS * MOE_ROW_BLOCK
    row_tok = jnp.full((n_rows,), n_tok, jnp.int32).at[dest].set(t_s)
    row_w = jnp.zeros((n_rows,), f32).at[dest].set(w_s)
    n_blk = n_rows // MOE_ROW_BLOCK
    blk_e = jnp.minimum(jnp.searchsorted(pad_end, jnp.arange(n_blk) * MOE_ROW_BLOCK, side='right'),
                        N_EXPERTS - 1)
    x_rows = jnp.concatenate([xt, jnp.zeros((1, D), xt.dtype)], axis=0)[row_tok]
    x_rows = x_rows.reshape(n_blk, MOE_ROW_BLOCK, D)

    def expert_rows(args):
        xb, e = args
        gu = xb @ w_gu[e]
        return (jax.nn.silu(gu[:, :EXPERT_FF]) * gu[:, EXPERT_FF:]) @ w_down[e]

    out = lax.map(expert_rows, (x_rows, blk_e)).reshape(n_rows, D)
    y = jnp.zeros((n_tok + 1, D), f32).at[row_tok].add(out.astype(f32) * row_w[:, None])[:n_tok]
    return y.reshape(Bsz, L, D).astype(u.dtype)


def setup_inputs(seed: int = 0) -> dict:
    key = jax.random.key(seed)
    keys = jax.random.split(key, 32)
    f32 = jnp.float32
    n_even = (DEPTH + 1) // 2
    n_odd = DEPTH // 2

    def nrm(i, shape, scale):
        return jax.random.normal(keys[i], shape, f32) * scale

    x = nrm(0, (BATCH, SEQ, D_MODEL), 1.0)
    positions = jnp.tile(jnp.arange(SEQ, dtype=jnp.int32)[None, :], (BATCH, 1))
    norm_mix = 1.0 + nrm(1, (DEPTH, D_MODEL), 0.02)
    norm_ffn = 1.0 + nrm(2, (DEPTH, D_MODEL), 0.02)
    n_idx = jnp.arange(S5_STATE, dtype=f32)
    s5_a_re = -0.5 + nrm(3, (n_even, S5_GROUPS, S5_STATE), 0.01)
    s5_a_im = math.pi * n_idx + nrm(4, (n_even, S5_GROUPS, S5_STATE), 0.01)
    s5_log_step = jax.random.uniform(keys[5], (n_even, S5_GROUPS), f32,
                                     math.log(S5_DT_MIN), math.log(S5_DT_MAX))
    s5_b_re = nrm(6, (n_even, S5_GROUPS, S5_STATE, S5_GROUP), (2 * S5_GROUP) ** -0.5)
    s5_b_im = nrm(7, (n_even, S5_GROUPS, S5_STATE, S5_GROUP), (2 * S5_GROUP) ** -0.5)
    s5_c_re = nrm(8, (n_even, S5_GROUPS, S5_GROUP, S5_STATE), S5_STATE ** -0.5)
    s5_c_im = nrm(9, (n_even, S5_GROUPS, S5_GROUP, S5_STATE), S5_STATE ** -0.5)
    s5_d = nrm(10, (n_even, D_MODEL), 1.0)
    s5_w_glu = nrm(11, (n_even, D_MODEL, 2 * D_MODEL), D_MODEL ** -0.5)
    nsa_w_in = nrm(12, (n_odd, D_MODEL, NSA_IN), D_MODEL ** -0.5)
    nsa_q_gain = 1.0 + nrm(13, (n_odd, HEAD_DIM), 0.02)
    nsa_k_gain = 1.0 + nrm(14, (n_odd, N_BRANCH, HEAD_DIM), 0.02)
    nsa_pe_k = nrm(15, (n_odd, CMP_BLOCK, HEAD_DIM), 0.1)
    nsa_pe_v = nrm(16, (n_odd, CMP_BLOCK, HEAD_DIM), 0.1)
    nsa_ck_w1 = nrm(17, (n_odd, CMP_BLOCK * HEAD_DIM, CMP_HIDDEN), (CMP_BLOCK * HEAD_DIM) ** -0.5)
    nsa_ck_w2 = nrm(18, (n_odd, CMP_HIDDEN, HEAD_DIM), CMP_HIDDEN ** -0.5)
    nsa_cv_w1 = nrm(19, (n_odd, CMP_BLOCK * HEAD_DIM, CMP_HIDDEN), (CMP_BLOCK * HEAD_DIM) ** -0.5)
    nsa_cv_w2 = nrm(20, (n_odd, CMP_HIDDEN, HEAD_DIM), CMP_HIDDEN ** -0.5)
    nsa_w_out = nrm(21, (n_odd, Q_DIM, D_MODEL), Q_DIM ** -0.5)
    ffn_w_gu = nrm(22, (n_even, D_MODEL, 2 * D_FF), D_MODEL ** -0.5)
    ffn_w_down = nrm(23, (n_even, D_FF, D_MODEL), D_FF ** -0.5)
    moe_w_router = nrm(24, (n_odd, D_MODEL, N_EXPERTS), D_MODEL ** -0.5)
    moe_b_router = nrm(25, (n_odd, N_EXPERTS), 0.01)
    moe_w_gu = nrm(26, (n_odd, N_EXPERTS, D_MODEL, 2 * EXPERT_FF), D_MODEL ** -0.5)
    moe_w_down = nrm(27, (n_odd, N_EXPERTS, EXPERT_FF, D_MODEL), EXPERT_FF ** -0.5)
    return {"x": x, "positions": positions, "norm_mix": norm_mix, "norm_ffn": norm_ffn,
            "s5_a_re": s5_a_re, "s5_a_im": s5_a_im, "s5_log_step": s5_log_step,
            "s5_b_re": s5_b_re, "s5_b_im": s5_b_im, "s5_c_re": s5_c_re, "s5_c_im": s5_c_im,
            "s5_d": s5_d, "s5_w_glu": s5_w_glu,
            "nsa_w_in": nsa_w_in, "nsa_q_gain": nsa_q_gain, "nsa_k_gain": nsa_k_gain,
            "nsa_pe_k": nsa_pe_k, "nsa_pe_v": nsa_pe_v, "nsa_ck_w1": nsa_ck_w1, "nsa_ck_w2": nsa_ck_w2,
            "nsa_cv_w1": nsa_cv_w1, "nsa_cv_w2": nsa_cv_w2, "nsa_w_out": nsa_w_out,
            "ffn_w_gu": ffn_w_gu, "ffn_w_down": ffn_w_down,
            "moe_w_router": moe_w_router, "moe_b_router": moe_b_router,
            "moe_w_gu": moe_w_gu, "moe_w_down": moe_w_down}


def reference(x, positions, norm_mix, norm_ffn,
              s5_a_re, s5_a_im, s5_log_step, s5_b_re, s5_b_im, s5_c_re, s5_c_im, s5_d, s5_w_glu,
              nsa_w_in, nsa_q_gain, nsa_k_gain, nsa_pe_k, nsa_pe_v, nsa_ck_w1, nsa_ck_w2,
              nsa_cv_w1, nsa_cv_w2, nsa_w_out,
              ffn_w_gu, ffn_w_down,
              moe_w_router, moe_b_router, moe_w_gu, moe_w_down):
    h = x
    for i in range(DEPTH):
        j = i // 2
        u = rms_norm(h, norm_mix[i])
        if i % 2 == 0:
            m = s5_mixer(u, s5_a_re[j], s5_a_im[j], s5_log_step[j], s5_b_re[j], s5_b_im[j],
                         s5_c_re[j], s5_c_im[j], s5_d[j], s5_w_glu[j])
        else:
            m = nsa_mixer(u, positions, nsa_w_in[j], nsa_q_gain[j], nsa_k_gain[j], nsa_pe_k[j],
                          nsa_pe_v[j], nsa_ck_w1[j], nsa_ck_w2[j], nsa_cv_w1[j], nsa_cv_w2[j],
                          nsa_w_out[j])
        h = h + m.astype(h.dtype)
        u = rms_norm(h, norm_ffn[i])
        if i % 2 == 0:
            f = swiglu(u, ffn_w_gu[j], ffn_w_down[j])
        else:
            f = moe_ffn(u, moe_w_router[j], moe_b_router[j], moe_w_gu[j], moe_w_down[j])
        h = h + f.astype(h.dtype)
    return h
```

```python
import functools
import math

import jax
import jax.numpy as jnp
from jax import lax
from jax.experimental import pallas as pl
from jax.experimental.pallas import tpu as pltpu

F32 = jnp.float32
BF16 = jnp.bfloat16

RMS_EPS = 1e-6
S5_GROUP = 16
S5_STATE = 64
HEAD_DIM = 128
N_KV_HEADS = 4
GQA_GROUP = 4
ROPE_DIM = 32
ROPE_THETA = 500000.0
CMP_BLOCK = 32
CMP_STRIDE = 16
SEL_BLOCK = 64
SEL_TOPK = 16
SEL_LOCAL = 2
WINDOW = 512
Q_BLOCK = 128
N_EXPERTS = 8
NEG = -1e30

LANES = 128
SUBLANES = 8
VMEM_LIMIT = 56 * 1024 * 1024

S5_SLAB = 256
S5_SLAB_STATES = S5_SLAB // S5_GROUP * S5_STATE
S5_SUB = 64
MOE_ROWS = 256


def _params(*sem):
    return pltpu.CompilerParams(dimension_semantics=sem, vmem_limit_bytes=VMEM_LIMIT)


def _rms_kernel(x_ref, g_ref, o_ref):
    x = x_ref[...]
    ms = jnp.mean(x * x, axis=-1, keepdims=True)
    o_ref[...] = (x * lax.rsqrt(ms + RMS_EPS) * g_ref[...]).astype(o_ref.dtype)


def _rms_norm(x, gain, out_dtype, tm=512):
    m, d = x.shape
    return pl.pallas_call(
        _rms_kernel,
        out_shape=jax.ShapeDtypeStruct((m, d), out_dtype),
        grid=(m // tm,),
        in_specs=[pl.BlockSpec((tm, d), lambda i: (i, 0)),
                  pl.BlockSpec((1, d), lambda i: (0, 0))],
        out_specs=pl.BlockSpec((tm, d), lambda i: (i, 0)),
        compiler_params=_params("arbitrary"),
        name="rms_norm",
    )(x, gain.reshape(1, d))


def _s5_kernel(u_ref, b_ref, c_ref, lam_ref, ptab_ref, apow_ref, d_ref, o_ref,
               xs_ref, carry_ref, up_ref, us_ref, *, sub):
    n = S5_SLAB_STATES

    @pl.when(pl.program_id(1) == 0)
    def _():
        carry_ref[...] = jnp.zeros_like(carry_ref)

    halves = S5_SLAB // LANES
    for c in range(halves):
        us_ref[c] = u_ref[:, LANES * c:LANES * (c + 1)]
    for i in range(sub):
        for c in range(halves):
            up_ref[SUBLANES * i:SUBLANES * (i + 1), LANES * c:LANES * (c + 1)] = (
                us_ref[c, pl.ds(i, SUBLANES, stride=sub), :])
    up = up_ref[...]
    xs_ref[...] = jnp.dot(up.astype(BF16), b_ref[0], preferred_element_type=F32)

    lam = lam_ref[0]
    lr, li = lam[:, :n], lam[:, n:]

    def local_scan(i, h):
        hr, hi = h
        off = pl.multiple_of(i * SUBLANES, SUBLANES)
        x = xs_ref[pl.ds(off, SUBLANES), :]
        nr = lr * hr - li * hi + x[:, :n]
        ni = lr * hi + li * hr + x[:, n:]
        xs_ref[pl.ds(off, SUBLANES), :] = jnp.concatenate([nr, ni], axis=1)
        return nr, ni

    zero = jnp.zeros((SUBLANES, n), F32)
    er, ei = lax.fori_loop(0, sub, local_scan, (zero, zero))

    row = lax.broadcasted_iota(jnp.int32, (SUBLANES, n), 0)
    cin = carry_ref[...]
    zr = jnp.where(row == 0, cin[:, :n], pltpu.roll(er, 1, 0))
    zi = jnp.where(row == 0, cin[:, n:], pltpu.roll(ei, 1, 0))
    apow = apow_ref[0]
    for s, d in enumerate((1, 2, 4)):
        ar = apow[SUBLANES * s:SUBLANES * (s + 1), :n]
        ai = apow[SUBLANES * s:SUBLANES * (s + 1), n:]
        sr = pltpu.roll(zr, d, 0)
        si = pltpu.roll(zi, d, 0)
        keep = row >= d
        zr, zi = (zr + jnp.where(keep, ar * sr - ai * si, 0.0),
                  zi + jnp.where(keep, ar * si + ai * sr, 0.0))
    a1r, a1i = apow[:SUBLANES, :n], apow[:SUBLANES, n:]
    nxt_r = a1r * zr - a1i * zi + er
    nxt_i = a1r * zi + a1i * zr + ei
    carry_ref[...] = jnp.concatenate(
        [jnp.broadcast_to(nxt_r[SUBLANES - 1:, :], (SUBLANES, n)),
         jnp.broadcast_to(nxt_i[SUBLANES - 1:, :], (SUBLANES, n))], axis=1)

    def add_carry(i, _):
        off = pl.multiple_of(i * SUBLANES, SUBLANES)
        x = xs_ref[pl.ds(off, SUBLANES), :]
        p = ptab_ref[0, pl.ds(off, SUBLANES), :]
        pr, pi = p[:, :n], p[:, n:]
        nr = x[:, :n] + pr * zr - pi * zi
        ni = x[:, n:] + pr * zi + pi * zr
        xs_ref[pl.ds(off, SUBLANES), :] = jnp.concatenate([nr, ni], axis=1)
        return 0

    lax.fori_loop(0, sub, add_carry, 0)

    y = jnp.dot(xs_ref[...].astype(BF16), c_ref[0], preferred_element_type=F32)
    g = jax.nn.gelu(y + d_ref[...] * up)
    for c in range(halves):
        us_ref[c] = g[:, LANES * c:LANES * (c + 1)]
    for j in range(SUBLANES):
        for c in range(halves):
            o_ref[sub * j:sub * (j + 1), LANES * c:LANES * (c + 1)] = (
                us_ref[c, pl.ds(j, sub, stride=SUBLANES), :].astype(o_ref.dtype))


def _s5_tables(a_re, a_im, log_step, b_re, b_im, c_re, c_im, sub):
    g = a_re.shape[0]
    n_slab = g * S5_GROUP // S5_SLAB
    gl = S5_SLAB // S5_GROUP
    dt = jnp.exp(log_step.astype(F32))[:, None]
    ar = a_re.astype(F32)
    ai = a_im.astype(F32)
    mag = jnp.exp(ar * dt)
    lb_re = mag * jnp.cos(ai * dt)
    lb_im = mag * jnp.sin(ai * dt)
    den = ar * ar + ai * ai
    nr = lb_re - 1.0
    coef_re = (nr * ar + lb_im * ai) / den
    coef_im = (lb_im * ar - nr * ai) / den
    bb_re = coef_re[..., None] * b_re - coef_im[..., None] * b_im
    bb_im = coef_re[..., None] * b_im + coef_im[..., None] * b_re
    eye = jnp.eye(gl, dtype=F32)

    def b_slab(t):
        t = t.reshape(n_slab, gl, S5_STATE, S5_GROUP)
        return jnp.einsum("kgpc,gh->kgchp", t, eye).reshape(n_slab, S5_SLAB, gl * S5_STATE)

    def c_slab(t):
        t = t.reshape(n_slab, gl, S5_GROUP, S5_STATE)
        return jnp.einsum("kgcp,gh->kgphc", t, eye).reshape(n_slab, gl * S5_STATE, S5_SLAB)

    b_dense = jnp.concatenate([b_slab(bb_re), b_slab(bb_im)], axis=2).astype(BF16)
    c_dense = jnp.concatenate([c_slab(c_re.astype(F32)), -c_slab(c_im.astype(F32))], axis=1).astype(BF16)

    def flat(t):
        return t.reshape(n_slab, gl * S5_STATE)

    def power(k):
        kk = k.astype(F32)[None, :, None]
        m = jnp.exp(flat(ar * dt)[:, None, :] * kk)
        ph = flat(ai * dt)[:, None, :] * kk
        return jnp.concatenate([m * jnp.cos(ph), m * jnp.sin(ph)], axis=2)

    lam = jnp.repeat(power(jnp.array([1])), SUBLANES, axis=1)
    ptab = jnp.repeat(power(jnp.arange(1, sub + 1)), SUBLANES, axis=1)
    apow = jnp.repeat(power(jnp.array([sub, 2 * sub, 4 * sub])), SUBLANES, axis=1)
    return b_dense, c_dense, lam, ptab, apow


def _s5_mixer(u, a_re, a_im, log_step, b_re, b_im, c_re, c_im, d_skip, sub=S5_SUB):
    seq, d = u.shape
    rows = SUBLANES * sub
    n_slab = d // S5_SLAB
    n2 = 2 * S5_SLAB_STATES
    b_dense, c_dense, lam, ptab, apow = _s5_tables(a_re, a_im, log_step, b_re, b_im, c_re, c_im, sub)
    return pl.pallas_call(
        functools.partial(_s5_kernel, sub=sub),
        out_shape=jax.ShapeDtypeStruct((seq, d), BF16),
        grid=(n_slab, seq // rows),
        in_specs=[
            pl.BlockSpec((rows, S5_SLAB), lambda k, c: (c, k)),
            pl.BlockSpec((1, S5_SLAB, n2), lambda k, c: (k, 0, 0)),
            pl.BlockSpec((1, n2, S5_SLAB), lambda k, c: (k, 0, 0)),
            pl.BlockSpec((1, SUBLANES, n2), lambda k, c: (k, 0, 0)),
            pl.BlockSpec((1, rows, n2), lambda k, c: (k, 0, 0)),
            pl.BlockSpec((1, 3 * SUBLANES, n2), lambda k, c: (k, 0, 0)),
            pl.BlockSpec((1, S5_SLAB), lambda k, c: (0, k)),
        ],
        out_specs=pl.BlockSpec((rows, S5_SLAB), lambda k, c: (c, k)),
        scratch_shapes=[pltpu.VMEM((rows, n2), F32),
                        pltpu.VMEM((SUBLANES, n2), F32),
                        pltpu.VMEM((rows, S5_SLAB), F32),
                        pltpu.VMEM((S5_SLAB // LANES, rows, LANES), F32)],
        compiler_params=_params("arbitrary", "arbitrary"),
        name="s5_scan",
    )(u, b_dense, c_dense, lam, ptab, apow, d_skip.reshape(1, d).astype(F32))


def _cache_weights(first, pairs):
    @pl.when(first)
    def _():
        for src, dst in pairs:
            dst[...] = src[...].astype(BF16)


def _glu_kernel(a_ref, wa_ref, wb_ref, r_ref, o_ref, wa_s, wb_s):
    _cache_weights(pl.program_id(1) == 0, ((wa_ref, wa_s), (wb_ref, wb_s)))
    a = a_ref[...]
    va = jnp.dot(a, wa_s[...], preferred_element_type=F32)
    vb = jnp.dot(a, wb_s[...], preferred_element_type=F32)
    o_ref[...] = r_ref[...] + va * jax.nn.sigmoid(vb)


def _glu_residual(a, w, res, tm=512, tn=512):
    m, k = a.shape
    n = w.shape[1] // 2
    nb = n // tn
    return pl.pallas_call(
        _glu_kernel,
        out_shape=jax.ShapeDtypeStruct((m, n), F32),
        grid=(nb, m // tm),
        in_specs=[pl.BlockSpec((tm, k), lambda j, i: (i, 0)),
                  pl.BlockSpec((k, tn), lambda j, i: (0, j)),
                  pl.BlockSpec((k, tn), lambda j, i: (0, j + nb)),
                  pl.BlockSpec((tm, tn), lambda j, i: (i, j))],
        out_specs=pl.BlockSpec((tm, tn), lambda j, i: (i, j)),
        scratch_shapes=[pltpu.VMEM((k, tn), BF16), pltpu.VMEM((k, tn), BF16)],
        compiler_params=_params("arbitrary", "arbitrary"),
        name="glu_residual",
    )(a, w, w, res)


def _swiglu_up_kernel(a_ref, wg_ref, wu_ref, o_ref, wg_s, wu_s):
    _cache_weights(pl.program_id(1) == 0, ((wg_ref, wg_s), (wu_ref, wu_s)))
    a = a_ref[...]
    vg = jnp.dot(a, wg_s[...], preferred_element_type=F32)
    vu = jnp.dot(a, wu_s[...], preferred_element_type=F32)
    o_ref[...] = (jax.nn.silu(vg) * vu).astype(o_ref.dtype)


def _swiglu_up(a, w_gu, tm=512, tn=512):
    m, k = a.shape
    f = w_gu.shape[1] // 2
    nb = f // tn
    return pl.pallas_call(
        _swiglu_up_kernel,
        out_shape=jax.ShapeDtypeStruct((m, f), BF16),
        grid=(nb, m // tm),
        in_specs=[pl.BlockSpec((tm, k), lambda j, i: (i, 0)),
                  pl.BlockSpec((k, tn), lambda j, i: (0, j)),
                  pl.BlockSpec((k, tn), lambda j, i: (0, j + nb))],
        out_specs=pl.BlockSpec((tm, tn), lambda j, i: (i, j)),
        scratch_shapes=[pltpu.VMEM((k, tn), BF16), pltpu.VMEM((k, tn), BF16)],
        compiler_params=_params("arbitrary", "arbitrary"),
        name="swiglu_up",
    )(a, w_gu, w_gu)


def _mm_res_kernel(a_ref, w_ref, r_ref, o_ref, w_s):
    _cache_weights(pl.program_id(1) == 0, ((w_ref, w_s),))
    o_ref[...] = r_ref[...] + jnp.dot(a_ref[...], w_s[...], preferred_element_type=F32)


def _matmul_residual(a, w, res, tm=512, tn=256):
    m, k = a.shape
    n = w.shape[1]
    return pl.pallas_call(
        _mm_res_kernel,
        out_shape=jax.ShapeDtypeStruct((m, n), F32),
        grid=(n // tn, m // tm),
        in_specs=[pl.BlockSpec((tm, k), lambda j, i: (i, 0)),
                  pl.BlockSpec((k, tn), lambda j, i: (0, j)),
                  pl.BlockSpec((tm, tn), lambda j, i: (i, j))],
        out_specs=pl.BlockSpec((tm, tn), lambda j, i: (i, j)),
        scratch_shapes=[pltpu.VMEM((k, tn), BF16)],
        compiler_params=_params("arbitrary", "arbitrary"),
        name="matmul_residual",
    )(a, w, res)


def _rope_kernel(pos_ref, inv_ref, cos_ref, sa_ref, sb_ref):
    ang = pos_ref[...].astype(F32) * inv_ref[...]
    c = jnp.cos(ang)
    s = jnp.sin(ang)
    lane = lax.broadcasted_iota(jnp.int32, ang.shape, 1)
    first_half = lane < ROPE_DIM // 2
    cos_ref[...] = c
    sa_ref[...] = jnp.where(first_half, -s, 0.0)
    sb_ref[...] = jnp.where(first_half, 0.0, s)


def _rope_tables(pos, tm=512):
    n = pos.shape[0]
    tm = min(tm, n)
    half = ROPE_DIM // 2
    inv = jnp.power(ROPE_THETA, -jnp.arange(half, dtype=F32) / half)
    inv = jnp.concatenate([inv, inv, jnp.zeros((LANES - ROPE_DIM,), F32)]).reshape(1, LANES)
    spec = pl.BlockSpec((tm, LANES), lambda i: (i, 0))
    return pl.pallas_call(
        _rope_kernel,
        out_shape=[jax.ShapeDtypeStruct((n, LANES), F32)] * 3,
        grid=(n // tm,),
        in_specs=[pl.BlockSpec((tm, 1), lambda i: (i, 0)),
                  pl.BlockSpec((1, LANES), lambda i: (0, 0))],
        out_specs=[spec, spec, spec],
        compiler_params=_params("arbitrary"),
        name="rope_tables",
    )(pos.reshape(n, 1), inv)


def _rope(y, c, sa, sb):
    half = ROPE_DIM // 2
    return y * c + pltpu.roll(y, LANES - half, 1) * sa + pltpu.roll(y, half, 1) * sb


def _head_norm(x, gain):
    return x * lax.rsqrt(jnp.mean(x * x, axis=-1, keepdims=True) + RMS_EPS) * gain


def _nsa_proj_kernel(a_ref, w_ref, gain_ref, cos_ref, sa_ref, sb_ref, o_ref, w_s, *, norm_tiles):
    j = pl.program_id(0)
    _cache_weights(pl.program_id(1) == 0, ((w_ref, w_s),))
    acc = jnp.dot(a_ref[...], w_s[...], preferred_element_type=F32)
    is_norm = functools.reduce(jnp.logical_or, [j == t for t in norm_tiles])

    @pl.when(is_norm)
    def _():
        c, sa, sb = cos_ref[...], sa_ref[...], sb_ref[...]
        gain = gain_ref[0]
        for hh in range(acc.shape[1] // HEAD_DIM):
            sl = slice(HEAD_DIM * hh, HEAD_DIM * (hh + 1))
            o_ref[:, sl] = _rope(_head_norm(acc[:, sl], gain), c, sa, sb).astype(o_ref.dtype)

    @pl.when(jnp.logical_not(is_norm))
    def _():
        o_ref[...] = acc.astype(o_ref.dtype)


def _nsa_proj(u, w_in, q_gain, k_gain, rope, tm=512, tn=512):
    m, k = u.shape
    q_dim = GQA_GROUP * N_KV_HEADS * HEAD_DIM
    kv_dim = N_KV_HEADS * HEAD_DIM
    assert kv_dim == tn
    n_q = q_dim // tn
    n_tiles = n_q + 6
    ones = jnp.ones((HEAD_DIM,), F32)
    gains = jnp.stack([q_gain] * n_q + [ones, ones, k_gain[1], ones, k_gain[2], ones]).reshape(n_tiles, 1, HEAD_DIM)
    norm_tiles = tuple(range(n_q)) + (n_q + 2, n_q + 4)
    tab = pl.BlockSpec((tm, LANES), lambda j, i: (i, 0))
    return pl.pallas_call(
        functools.partial(_nsa_proj_kernel, norm_tiles=norm_tiles),
        out_shape=jax.ShapeDtypeStruct((m, n_tiles * tn), BF16),
        grid=(n_tiles, m // tm),
        in_specs=[pl.BlockSpec((tm, k), lambda j, i: (i, 0)),
                  pl.BlockSpec((k, tn), lambda j, i: (0, j)),
                  pl.BlockSpec((1, 1, HEAD_DIM), lambda j, i: (j, 0, 0)),
                  tab, tab, tab],
        out_specs=pl.BlockSpec((tm, tn), lambda j, i: (i, j)),
        scratch_shapes=[pltpu.VMEM((k, tn), BF16)],
        compiler_params=_params("arbitrary", "arbitrary"),
        name="nsa_proj",
    )(u, w_in, gains, *rope)


def _gate_kernel(a_ref, w_ref, o_ref):
    o_ref[...] = jax.nn.sigmoid(jnp.dot(a_ref[...], w_ref[...].astype(BF16), preferred_element_type=F32))


def _nsa_gates(u, w_gate, tm=512):
    m, k = u.shape
    n = w_gate.shape[1]
    w_pad = jnp.pad(w_gate, ((0, 0), (0, LANES - n)))
    return pl.pallas_call(
        _gate_kernel,
        out_shape=jax.ShapeDtypeStruct((m, LANES), F32),
        grid=(m // tm,),
        in_specs=[pl.BlockSpec((tm, k), lambda i: (i, 0)),
                  pl.BlockSpec((k, LANES), lambda i: (0, 0))],
        out_specs=pl.BlockSpec((tm, LANES), lambda i: (i, 0)),
        compiler_params=_params("arbitrary"),
        name="nsa_gates",
    )(u, w_pad)


def _compress_kernel(*refs, is_key):
    if is_key:
        ca_ref, cb_ref, pe_ref, w1_ref, w2_ref, gain_ref, cos_ref, sa_ref, sb_ref, o_ref = refs
    else:
        ca_ref, cb_ref, pe_ref, w1_ref, w2_ref, o_ref = refs
    half = w1_ref.shape[0] // 2
    pe = pe_ref[...]
    xa = (ca_ref[0].astype(F32) + pe[:, :half]).astype(BF16)
    xb = (cb_ref[0].astype(F32) + pe[:, half:]).astype(BF16)
    hid = (jnp.dot(xa, w1_ref[:half, :].astype(BF16), preferred_element_type=F32)
           + jnp.dot(xb, w1_ref[half:, :].astype(BF16), preferred_element_type=F32))
    out = jnp.dot(jax.nn.gelu(hid).astype(BF16), w2_ref[...].astype(BF16), preferred_element_type=F32)
    if is_key:
        out = _rope(_head_norm(out, gain_ref[...]), cos_ref[...], sa_ref[...], sb_ref[...])
    o_ref[0] = out.astype(o_ref.dtype)


def _compress(t, pe, w1, w2, key_extras=None):
    seq = t.shape[0]
    nc = seq // CMP_STRIDE
    width = CMP_STRIDE * HEAD_DIM
    ca = t.reshape(nc, CMP_STRIDE, N_KV_HEADS, HEAD_DIM).transpose(2, 0, 1, 3).reshape(N_KV_HEADS, nc, width)
    cb = jnp.concatenate([ca[:, 1:], jnp.zeros((N_KV_HEADS, 1, width), ca.dtype)], axis=1)
    blk = pl.BlockSpec((1, nc, width), lambda h: (h, 0, 0))
    full = lambda a: pl.BlockSpec(a.shape, lambda h: (0,) * a.ndim)
    args = [ca, cb, pe.reshape(1, CMP_BLOCK * HEAD_DIM), w1, w2]
    if key_extras is not None:
        args += list(key_extras)
    return pl.pallas_call(
        functools.partial(_compress_kernel, is_key=key_extras is not None),
        out_shape=jax.ShapeDtypeStruct((N_KV_HEADS, nc, HEAD_DIM), BF16),
        grid=(N_KV_HEADS,),
        in_specs=[blk, blk] + [full(a) for a in args[2:]],
        out_specs=pl.BlockSpec((1, nc, HEAD_DIM), lambda h: (h, 0, 0)),
        compiler_params=_params("arbitrary"),
        name="nsa_compress_k" if key_extras is not None else "nsa_compress_v",
    )(*args)


def _masked_softmax(s, mask):
    sm = jnp.where(mask, s, NEG)
    m = jnp.max(sm, axis=-1, keepdims=True)
    e = jnp.where(mask, jnp.exp(sm - m), 0.0)
    return e / jnp.maximum(jnp.sum(e, axis=-1, keepdims=True), 1e-30)


def _dot_nt(a, b):
    return lax.dot_general(a, b, (((1,), (1,)), ((), ())), preferred_element_type=F32)


def _split3(x):
    hi = x.astype(BF16)
    r1 = x - hi.astype(F32)
    mid = r1.astype(BF16)
    lo = (r1 - mid.astype(F32)).astype(BF16)
    return hi, mid, lo


def _nsa_attn_kernel(q_ref, kc_ref, vc_ref, ks_ref, vs_ref, kw_ref, vw_ref, gate_ref, o_ref, *, seq, tk):
    t0 = pl.program_id(1) * Q_BLOCK
    nc = kc_ref.shape[1]
    ns = seq // SEL_BLOCK
    scale = HEAD_DIM ** -0.5
    grp = GQA_GROUP
    q = q_ref[...]
    qs = jnp.concatenate([q[:, HEAD_DIM * g:HEAD_DIM * (g + 1)] for g in range(grp)], axis=0)
    t_col = t0 + lax.broadcasted_iota(jnp.int32, (Q_BLOCK, 1), 0)

    def tile_rows(mask):
        return jnp.concatenate([mask] * grp, axis=0)

    n_idx = lax.broadcasted_iota(jnp.int32, (Q_BLOCK, nc), 1)
    mask_c = (n_idx * CMP_STRIDE + (CMP_BLOCK - 1) <= t_col) & (n_idx < nc - 1)
    p_c = _masked_softmax(_dot_nt(qs, kc_ref[0]) * scale, tile_rows(mask_c))
    o_c = jnp.dot(p_c.astype(BF16), vc_ref[0], preferred_element_type=F32)

    imp = p_c[:Q_BLOCK]
    for g in range(1, grp):
        imp = imp + p_c[Q_BLOCK * g:Q_BLOCK * (g + 1)]
    ratio = SEL_BLOCK // CMP_STRIDE
    d = (lax.broadcasted_iota(jnp.int32, (nc, ns), 0)
         - ratio * lax.broadcasted_iota(jnp.int32, (nc, ns), 1))
    overlap = jnp.zeros((nc, ns), F32)
    for n in range(CMP_BLOCK // CMP_STRIDE):
        overlap = overlap + jnp.where((d - n >= 0) & (d - n < ratio), 1.0, 0.0)
    overlap = overlap.astype(BF16)
    p_slc = sum(jnp.dot(part, overlap, preferred_element_type=F32) for part in _split3(imp))

    j_idx = lax.broadcasted_iota(jnp.int32, (Q_BLOCK, ns), 1)
    j_f = j_idx.astype(F32)
    dist = jnp.right_shift(t_col, int(math.log2(SEL_BLOCK))) - j_idx
    forced = (j_idx == 0) | ((dist >= 0) & (dist < SEL_LOCAL))
    score = jnp.where(forced, jnp.inf, jnp.where(dist >= 0, p_slc, -jnp.inf))
    sel = jnp.zeros((Q_BLOCK, ns), F32)
    for _ in range(min(SEL_TOPK, ns)):
        top = jnp.max(score, axis=-1, keepdims=True)
        idx = jnp.min(jnp.where(score == top, j_f, float(ns)), axis=-1, keepdims=True)
        pick = j_f == idx
        sel = jnp.where(pick, 1.0, sel)
        score = jnp.where(pick, -jnp.inf, score)
    sel_b = sel.astype(BF16)

    blocks_per_tile = tk // SEL_BLOCK

    def sel_step(kt, carry):
        m_i, l_i, acc = carry
        k0 = pl.multiple_of(kt * tk, tk)
        s = _dot_nt(qs, ks_ref[pl.ds(k0, tk), :]) * scale
        jj = lax.broadcasted_iota(jnp.int32, (ns, tk), 0)
        cc = lax.broadcasted_iota(jnp.int32, (ns, tk), 1)
        expand = jnp.where(jj == kt * blocks_per_tile + jnp.right_shift(cc, int(math.log2(SEL_BLOCK))),
                           1.0, 0.0).astype(BF16)
        chosen = jnp.dot(sel_b, expand, preferred_element_type=F32) > 0.5
        kpos = k0 + lax.broadcasted_iota(jnp.int32, (Q_BLOCK, tk), 1)
        mask = tile_rows(chosen & (kpos <= t_col))
        sm = jnp.where(mask, s, NEG)
        m_new = jnp.maximum(m_i, jnp.max(sm, axis=-1, keepdims=True))
        e = jnp.where(mask, jnp.exp(sm - m_new), 0.0)
        alpha = jnp.exp(m_i - m_new)
        l_new = alpha * l_i + jnp.sum(e, axis=-1, keepdims=True)
        acc = alpha * acc + jnp.dot(e.astype(BF16), vs_ref[pl.ds(k0, tk), :], preferred_element_type=F32)
        return m_new, l_new, acc

    rows = grp * Q_BLOCK
    init = (jnp.full((rows, 1), NEG, F32), jnp.zeros((rows, 1), F32), jnp.zeros((rows, HEAD_DIM), F32))
    n_kt = (t0 + Q_BLOCK + tk - 1) // tk
    _, l_s, acc_s = lax.fori_loop(0, n_kt, sel_step, init)
    o_s = acc_s / jnp.maximum(l_s, 1e-30)

    span = WINDOW + Q_BLOCK
    w0 = pl.multiple_of(jnp.maximum(t0 - WINDOW, 0), Q_BLOCK)
    rel = t_col - (w0 + lax.broadcasted_iota(jnp.int32, (Q_BLOCK, span), 1))
    mask_w = tile_rows((rel >= 0) & (rel < WINDOW))
    p_w = _masked_softmax(_dot_nt(qs, kw_ref[pl.ds(w0, span), :]) * scale, mask_w)
    o_w = jnp.dot(p_w.astype(BF16), vw_ref[pl.ds(w0, span), :], preferred_element_type=F32)

    gate = gate_ref[0]
    for g in range(grp):
        sl = slice(Q_BLOCK * g, Q_BLOCK * (g + 1))
        mixed = (gate[:, g:g + 1] * o_c[sl] + gate[:, grp + g:grp + g + 1] * o_s[sl]
                 + gate[:, 2 * grp + g:2 * grp + g + 1] * o_w[sl])
        o_ref[:, HEAD_DIM * g:HEAD_DIM * (g + 1)] = mixed.astype(o_ref.dtype)


def _nsa_attention(proj, kcmp, vcmp, gates, tk=512):
    seq = proj.shape[0]
    tk = min(tk, seq)
    q_dim = GQA_GROUP * N_KV_HEADS * HEAD_DIM
    kv_blocks = N_KV_HEADS
    first = q_dim // HEAD_DIM + 2 * kv_blocks
    nc = kcmp.shape[1]

    def kv_spec(which):
        return pl.BlockSpec((seq, HEAD_DIM), lambda h, qb: (0, first + which * kv_blocks + h))

    cmp_spec = pl.BlockSpec((1, nc, HEAD_DIM), lambda h, qb: (h, 0, 0))
    q_spec = pl.BlockSpec((Q_BLOCK, GQA_GROUP * HEAD_DIM), lambda h, qb: (qb, h))
    n_gate = 3 * GQA_GROUP
    return pl.pallas_call(
        functools.partial(_nsa_attn_kernel, seq=seq, tk=tk),
        out_shape=jax.ShapeDtypeStruct((seq, q_dim), BF16),
        grid=(N_KV_HEADS, seq // Q_BLOCK),
        in_specs=[q_spec, cmp_spec, cmp_spec, kv_spec(0), kv_spec(1), kv_spec(2), kv_spec(3),
                  pl.BlockSpec((1, Q_BLOCK, n_gate), lambda h, qb: (h, qb, 0))],
        out_specs=q_spec,
        compiler_params=_params("arbitrary", "arbitrary"),
        name="nsa_attention",
    )(proj, kcmp, vcmp, proj, proj, proj, proj, gates)


def _nsa_mixer(u, positions, w_in, q_gain, k_gain, pe_k, pe_v, ck_w1, ck_w2, cv_w1, cv_w2):
    seq = u.shape[0]
    q_dim = GQA_GROUP * N_KV_HEADS * HEAD_DIM
    kv_dim = N_KV_HEADS * HEAD_DIM
    n_main = q_dim + 6 * kv_dim
    nc = seq // CMP_STRIDE
    rope = _rope_tables(positions)
    proj = _nsa_proj(u, w_in, q_gain, k_gain, rope)
    gate = _nsa_gates(u, w_in[:, n_main:])
    gates = (gate[:, :3 * N_KV_HEADS * GQA_GROUP].reshape(seq, 3, N_KV_HEADS, GQA_GROUP)
             .transpose(2, 0, 1, 3).reshape(N_KV_HEADS, seq, 3 * GQA_GROUP))
    pos_cmp = jnp.concatenate([positions[CMP_BLOCK - 1::CMP_STRIDE][:nc - 1], jnp.zeros((1,), positions.dtype)])
    rope_cmp = _rope_tables(pos_cmp)
    kcmp = _compress(proj[:, q_dim:q_dim + kv_dim], pe_k, ck_w1, ck_w2,
                     key_extras=(k_gain[0].reshape(1, HEAD_DIM),) + tuple(rope_cmp))
    vcmp = _compress(proj[:, q_dim + kv_dim:q_dim + 2 * kv_dim], pe_v, cv_w1, cv_w2)
    return _nsa_attention(proj, kcmp, vcmp, gates)


def _router_kernel(x_ref, g_ref, w_ref, b_ref, u_ref, r_ref):
    x = x_ref[...]
    u = x * lax.rsqrt(jnp.mean(x * x, axis=-1, keepdims=True) + RMS_EPS) * g_ref[...]
    u_ref[...] = u
    uh, um, _ = _split3(u)
    wh, wm, _ = _split3(w_ref[...])
    logits = (jnp.dot(uh, wh, preferred_element_type=F32) + jnp.dot(uh, wm, preferred_element_type=F32)
              + jnp.dot(um, wh, preferred_element_type=F32)) + b_ref[...]
    lane = lax.broadcasted_iota(jnp.int32, logits.shape, 1).astype(F32)
    lg = jnp.where(lane < N_EXPERTS, logits, -jnp.inf)
    v1 = jnp.max(lg, axis=-1, keepdims=True)
    i1 = jnp.min(jnp.where(lg == v1, lane, float(LANES)), axis=-1, keepdims=True)
    lg = jnp.where(lane == i1, -jnp.inf, lg)
    v2 = jnp.max(lg, axis=-1, keepdims=True)
    i2 = jnp.min(jnp.where(lg == v2, lane, float(LANES)), axis=-1, keepdims=True)
    e2 = jnp.exp(v2 - v1)
    den = 1.0 + e2
    r_ref[...] = jnp.where(lane == 0, i1, jnp.where(lane == 1, i2, jnp.where(
        lane == 2, 1.0 / den, jnp.where(lane == 3, e2 / den, 0.0))))


def _router(h, gain, w_router, b_router, tm=256):
    m, d = h.shape
    w_pad = jnp.pad(w_router.astype(F32), ((0, 0), (0, LANES - N_EXPERTS)))
    b_pad = jnp.pad(b_router.astype(F32), (0, LANES - N_EXPERTS)).reshape(1, LANES)
    return pl.pallas_call(
        _router_kernel,
        out_shape=[jax.ShapeDtypeStruct((m, d), F32), jax.ShapeDtypeStruct((m, LANES), F32)],
        grid=(m // tm,),
        in_specs=[pl.BlockSpec((tm, d), lambda i: (i, 0)),
                  pl.BlockSpec((1, d), lambda i: (0, 0)),
                  pl.BlockSpec((d, LANES), lambda i: (0, 0)),
                  pl.BlockSpec((1, LANES), lambda i: (0, 0))],
        out_specs=[pl.BlockSpec((tm, d), lambda i: (i, 0)), pl.BlockSpec((tm, LANES), lambda i: (i, 0))],
        compiler_params=_params("arbitrary"),
        name="moe_router",
    )(h, gain.reshape(1, d), w_pad, b_pad)


def _row_copy(src_hbm, row, dst, r, sem):
    return pltpu.make_async_copy(src_hbm.at[pl.ds(row, 1), :], dst.at[pl.ds(r, 1), :], sem)


def _gather_kernel(idx_ref, src_hbm, o_ref, sem):
    rows = o_ref.shape[0]
    base = pl.program_id(0) * rows

    def start(r, _):
        _row_copy(src_hbm, idx_ref[base + r], o_ref, r, sem).start()
        return 0

    def wait(r, _):
        _row_copy(src_hbm, 0, o_ref, r, sem).wait()
        return 0

    lax.fori_loop(0, rows, start, 0)
    lax.fori_loop(0, rows, wait, 0)


def _gather_rows(src, idx, rows=MOE_ROWS):
    n = idx.shape[0]
    d = src.shape[1]
    return pl.pallas_call(
        _gather_kernel,
        out_shape=jax.ShapeDtypeStruct((n, d), src.dtype),
        grid_spec=pltpu.PrefetchScalarGridSpec(
            num_scalar_prefetch=1,
            grid=(n // rows,),
            in_specs=[pl.BlockSpec(memory_space=pl.ANY)],
            out_specs=pl.BlockSpec((rows, d), lambda i, idx: (i, 0)),
            scratch_shapes=[pltpu.SemaphoreType.DMA(())]),
        compiler_params=_params("arbitrary"),
        name="moe_gather",
    )(idx, src)


def _expert_changed(be_ref, i):
    return (i == 0) | (be_ref[i] != be_ref[jnp.maximum(i - 1, 0)])


def _moe_up_kernel(be_ref, x_ref, wg_ref, wu_ref, o_ref, wg_s, wu_s):
    first = _expert_changed(be_ref, pl.program_id(1))

    @pl.when(first)
    def _():
        wg_s[...] = wg_ref[0].astype(BF16)
        wu_s[...] = wu_ref[0].astype(BF16)

    a = x_ref[...].astype(BF16)
    vg = jnp.dot(a, wg_s[...], preferred_element_type=F32)
    vu = jnp.dot(a, wu_s[...], preferred_element_type=F32)
    o_ref[...] = (jax.nn.silu(vg) * vu).astype(o_ref.dtype)


def _moe_up(x_rows, blk_e, w_gu, tn=512, rows=MOE_ROWS):
    n, k = x_rows.shape
    f = w_gu.shape[2] // 2
    nb = f // tn
    return pl.pallas_call(
        _moe_up_kernel,
        out_shape=jax.ShapeDtypeStruct((n, f), BF16),
        grid_spec=pltpu.PrefetchScalarGridSpec(
            num_scalar_prefetch=1,
            grid=(nb, n // rows),
            in_specs=[pl.BlockSpec((rows, k), lambda j, i, be: (i, 0)),
                      pl.BlockSpec((1, k, tn), lambda j, i, be: (be[i], 0, j)),
                      pl.BlockSpec((1, k, tn), lambda j, i, be: (be[i], 0, j + nb))],
            out_specs=pl.BlockSpec((rows, tn), lambda j, i, be: (i, j)),
            scratch_shapes=[pltpu.VMEM((k, tn), BF16), pltpu.VMEM((k, tn), BF16)]),
        compiler_params=_params("arbitrary", "arbitrary"),
        name="moe_up",
    )(blk_e, x_rows, w_gu, w_gu)


def _moe_down_kernel(be_ref, a_ref, w_ref, o_ref, w_s):
    first = _expert_changed(be_ref, pl.program_id(1))

    @pl.when(first)
    def _():
        w_s[...] = w_ref[0].astype(BF16)

    o_ref[...] = jnp.dot(a_ref[...], w_s[...], preferred_element_type=F32)


def _moe_down(act, blk_e, w_down, tn=256, rows=MOE_ROWS):
    n, k = act.shape
    d = w_down.shape[2]
    return pl.pallas_call(
        _moe_down_kernel,
        out_shape=jax.ShapeDtypeStruct((n, d), F32),
        grid_spec=pltpu.PrefetchScalarGridSpec(
            num_scalar_prefetch=1,
            grid=(d // tn, n // rows),
            in_specs=[pl.BlockSpec((rows, k), lambda j, i, be: (i, 0)),
                      pl.BlockSpec((1, k, tn), lambda j, i, be: (be[i], 0, j))],
            out_specs=pl.BlockSpec((rows, tn), lambda j, i, be: (i, j)),
            scratch_shapes=[pltpu.VMEM((k, tn), BF16)]),
        compiler_params=_params("arbitrary", "arbitrary"),
        name="moe_down",
    )(blk_e, act, w_down)


def _combine_kernel(dest_ref, h_ref, r_ref, rows_hbm, o_ref, buf, sem):
    tm = h_ref.shape[0]
    base = pl.program_id(0) * tm

    def start(r, _):
        for k in range(2):
            _row_copy(rows_hbm, dest_ref[2 * (base + r) + k], buf.at[k], r, sem.at[k]).start()
        return 0

    def wait(r, _):
        for k in range(2):
            _row_copy(rows_hbm, 0, buf.at[k], r, sem.at[k]).wait()
        return 0

    lax.fori_loop(0, tm, start, 0)
    lax.fori_loop(0, tm, wait, 0)
    w = r_ref[...]
    o_ref[...] = h_ref[...] + (w[:, 2:3] * buf[0] + w[:, 3:4] * buf[1])


def _moe_combine(h, route, out_rows, dest, tm=256):
    m, d = h.shape
    return pl.pallas_call(
        _combine_kernel,
        out_shape=jax.ShapeDtypeStruct((m, d), F32),
        grid_spec=pltpu.PrefetchScalarGridSpec(
            num_scalar_prefetch=1,
            grid=(m // tm,),
            in_specs=[pl.BlockSpec((tm, d), lambda i, dest: (i, 0)),
                      pl.BlockSpec((tm, LANES), lambda i, dest: (i, 0)),
                      pl.BlockSpec(memory_space=pl.ANY)],
            out_specs=pl.BlockSpec((tm, d), lambda i, dest: (i, 0)),
            scratch_shapes=[pltpu.VMEM((2, tm, d), F32), pltpu.SemaphoreType.DMA((2,))]),
        compiler_params=_params("arbitrary"),
        name="moe_combine",
    )(dest.reshape(-1), h, route, out_rows)


def _moe_layout(top_e, rows=MOE_ROWS):
    n_tok = top_e.shape[0]
    e_flat = top_e.reshape(-1)
    onehot = (e_flat[:, None] == jnp.arange(N_EXPERTS, dtype=jnp.int32)[None, :]).astype(jnp.int32)
    csum = jnp.cumsum(onehot, axis=0)
    rank = jnp.take_along_axis(csum, e_flat[:, None], axis=1)[:, 0] - 1
    counts = csum[-1]
    padded = (counts + rows - 1) // rows * rows
    pad_end = jnp.cumsum(padded)
    dest = (pad_end - padded)[e_flat] + rank
    n_rows = e_flat.shape[0] + N_EXPERTS * rows
    t_flat = jnp.repeat(jnp.arange(n_tok, dtype=jnp.int32), top_e.shape[1])
    row_tok = jnp.zeros((n_rows,), jnp.int32).at[dest].set(t_flat)
    n_blk = n_rows // rows
    blk_e = jnp.minimum(jnp.searchsorted(pad_end, jnp.arange(n_blk, dtype=jnp.int32) * rows, side="right"),
                        N_EXPERTS - 1).astype(jnp.int32)
    return row_tok, blk_e, dest.astype(jnp.int32).reshape(n_tok, -1)


def _moe_ffn_residual(h, gain, w_router, b_router, w_gu, w_down):
    u, route = _router(h, gain, w_router, b_router)
    top_e = route[:, :2].astype(jnp.int32)
    row_tok, blk_e, dest = _moe_layout(top_e)
    x_rows = _gather_rows(u, row_tok)
    act = _moe_up(x_rows, blk_e, w_gu)
    out_rows = _moe_down(act, blk_e, w_down)
    return _moe_combine(h, route, out_rows, dest)


def kernel(x, positions, norm_mix, norm_ffn, s5_a_re, s5_a_im, s5_log_step, s5_b_re, s5_b_im, s5_c_re, s5_c_im, s5_d, s5_w_glu, nsa_w_in, nsa_q_gain, nsa_k_gain, nsa_pe_k, nsa_pe_v, nsa_ck_w1, nsa_ck_w2, nsa_cv_w1, nsa_cv_w2, nsa_w_out, ffn_w_gu, ffn_w_down, moe_w_router, moe_b_router, moe_w_gu, moe_w_down):
    bsz, seq, d = x.shape
    assert bsz == 1, "the scan and attention kernels take one sequence"
    h = x.reshape(seq, d)
    h = _layer_s5(h, norm_mix[0], norm_ffn[0], s5_a_re[0], s5_a_im[0], s5_log_step[0], s5_b_re[0],
                  s5_b_im[0], s5_c_re[0], s5_c_im[0], s5_d[0], s5_w_glu[0], ffn_w_gu[0], ffn_w_down[0])
    h = _layer_nsa(h, positions[0], norm_mix[1], norm_ffn[1], nsa_w_in[0], nsa_q_gain[0], nsa_k_gain[0],
                   nsa_pe_k[0], nsa_pe_v[0], nsa_ck_w1[0], nsa_ck_w2[0], nsa_cv_w1[0], nsa_cv_w2[0],
                   nsa_w_out[0], moe_w_router[0], moe_b_router[0], moe_w_gu[0], moe_w_down[0])
    return h.reshape(bsz, seq, d)


def _layer_nsa(h, positions, g_mix, g_ffn, w_in, q_gain, k_gain, pe_k, pe_v, ck_w1, ck_w2, cv_w1, cv_w2,
               w_out, w_router, b_router, w_gu, w_down):
    u = _rms_norm(h, g_mix, BF16)
    o = _nsa_mixer(u, positions, w_in, q_gain, k_gain, pe_k, pe_v, ck_w1, ck_w2, cv_w1, cv_w2)
    h = _matmul_residual(o, w_out, h)
    return _moe_ffn_residual(h, g_ffn, w_router, b_router, w_gu, w_down)


def _layer_s5(h, g_mix, g_ffn, a_re, a_im, log_step, b_re, b_im, c_re, c_im, d_skip, w_glu, w_gu, w_down):
    u = _rms_norm(h, g_mix, F32)
    g = _s5_mixer(u, a_re, a_im, log_step, b_re, b_im, c_re, c_im, d_skip)
    h = _glu_residual(g, w_glu, h)
    u = _rms_norm(h, g_ffn, BF16)
    act = _swiglu_up(u, w_gu)
    return _matmul_residual(act, w_down, h)
```

```python
import functools
import math

import jax
import jax.numpy as jnp
from jax import lax
from jax.experimental import pallas as pl
from jax.experimental.pallas import tpu as pltpu

F32 = jnp.float32
BF16 = jnp.bfloat16

RMS_EPS = 1e-6
S5_GROUP = 16
S5_STATE = 64
HEAD_DIM = 128
N_KV_HEADS = 4
GQA_GROUP = 4
ROPE_DIM = 32
ROPE_THETA = 500000.0
CMP_BLOCK = 32
CMP_STRIDE = 16
SEL_BLOCK = 64
SEL_TOPK = 16
SEL_LOCAL = 2
WINDOW = 512
Q_BLOCK = 128
N_EXPERTS = 8
NEG = -1e30

LANES = 128
SUBLANES = 8
VMEM_LIMIT = 56 * 1024 * 1024

S5_SLAB = 256
S5_SLAB_STATES = S5_SLAB // S5_GROUP * S5_STATE
S5_SUB = 64
MOE_ROWS = 256


def _params(*sem):
    return pltpu.CompilerParams(dimension_semantics=sem, vmem_limit_bytes=VMEM_LIMIT)


def _rms_kernel(x_ref, g_ref, o_ref):
    x = x_ref[...]
    ms = jnp.mean(x * x, axis=-1, keepdims=True)
    o_ref[...] = (x * lax.rsqrt(ms + RMS_EPS) * g_ref[...]).astype(o_ref.dtype)


def _rms_norm(x, gain, out_dtype, tm=512):
    m, d = x.shape
    return pl.pallas_call(
        _rms_kernel,
        out_shape=jax.ShapeDtypeStruct((m, d), out_dtype),
        grid=(m // tm,),
        in_specs=[pl.BlockSpec((tm, d), lambda i: (i, 0)),
                  pl.BlockSpec((1, d), lambda i: (0, 0))],
        out_specs=pl.BlockSpec((tm, d), lambda i: (i, 0)),
        compiler_params=_params("arbitrary"),
        name="rms_norm",
    )(x, gain.reshape(1, d))


def _s5_kernel(u_ref, b_ref, c_ref, lam_ref, ptab_ref, apow_ref, d_ref, o_ref,
               xs_ref, carry_ref, up_ref, us_ref, *, sub):
    n = S5_SLAB_STATES

    @pl.when(pl.program_id(1) == 0)
    def _():
        carry_ref[...] = jnp.zeros_like(carry_ref)

    halves = S5_SLAB // LANES
    for c in range(halves):
        us_ref[c] = u_ref[:, LANES * c:LANES * (c + 1)]
    for i in range(sub):
        for c in range(halves):
            up_ref[SUBLANES * i:SUBLANES * (i + 1), LANES * c:LANES * (c + 1)] = (
                us_ref[c, pl.ds(i, SUBLANES, stride=sub), :])
    up = up_ref[...]
    xs_ref[...] = jnp.dot(up.astype(BF16), b_ref[0], preferred_element_type=F32)

    lam = lam_ref[0]
    lr, li = lam[:, :n], lam[:, n:]

    def local_scan(i, h):
        hr, hi = h
        off = pl.multiple_of(i * SUBLANES, SUBLANES)
        x = xs_ref[pl.ds(off, SUBLANES), :]
        nr = lr * hr - li * hi + x[:, :n]
        ni = lr * hi + li * hr + x[:, n:]
        xs_ref[pl.ds(off, SUBLANES), :] = jnp.concatenate([nr, ni], axis=1)
        return nr, ni

    zero = jnp.zeros((SUBLANES, n), F32)
    er, ei = lax.fori_loop(0, sub, local_scan, (zero, zero))

    row = lax.broadcasted_iota(jnp.int32, (SUBLANES, n), 0)
    cin = carry_ref[...]
    zr = jnp.where(row == 0, cin[:, :n], pltpu.roll(er, 1, 0))
    zi = jnp.where(row == 0, cin[:, n:], pltpu.roll(ei, 1, 0))
    apow = apow_ref[0]
    for s, d in enumerate((1, 2, 4)):
        ar = apow[SUBLANES * s:SUBLANES * (s + 1), :n]
        ai = apow[SUBLANES * s:SUBLANES * (s + 1), n:]
        sr = pltpu.roll(zr, d, 0)
        si = pltpu.roll(zi, d, 0)
        keep = row >= d
        zr, zi = (zr + jnp.where(keep, ar * sr - ai * si, 0.0),
                  zi + jnp.where(keep, ar * si + ai * sr, 0.0))
    a1r, a1i = apow[:SUBLANES, :n], apow[:SUBLANES, n:]
    nxt_r = a1r * zr - a1i * zi + er
    nxt_i = a1r * zi + a1i * zr + ei
    carry_ref[...] = jnp.concatenate(
        [jnp.broadcast_to(nxt_r[SUBLANES - 1:, :], (SUBLANES, n)),
         jnp.broadcast_to(nxt_i[SUBLANES - 1:, :], (SUBLANES, n))], axis=1)

    def add_carry(i, _):
        off = pl.multiple_of(i * SUBLANES, SUBLANES)
        x = xs_ref[pl.ds(off, SUBLANES), :]
        p = ptab_ref[0, pl.ds(off, SUBLANES), :]
        pr, pi = p[:, :n], p[:, n:]
        nr = x[:, :n] + pr * zr - pi * zi
        ni = x[:, n:] + pr * zi + pi * zr
        xs_ref[pl.ds(off, SUBLANES), :] = jnp.concatenate([nr, ni], axis=1)
        return 0

    lax.fori_loop(0, sub, add_carry, 0)

    y = jnp.dot(xs_ref[...].astype(BF16), c_ref[0], preferred_element_type=F32)
    g = jax.nn.gelu(y + d_ref[...] * up)
    for c in range(halves):
        us_ref[c] = g[:, LANES * c:LANES * (c + 1)]
    for j in range(SUBLANES):
        for c in range(halves):
            o_ref[sub * j:sub * (j + 1), LANES * c:LANES * (c + 1)] = (
                us_ref[c, pl.ds(j, sub, stride=SUBLANES), :].astype(o_ref.dtype))


def _s5_tables(a_re, a_im, log_step, b_re, b_im, c_re, c_im, sub):
    g = a_re.shape[0]
    n_slab = g * S5_GROUP // S5_SLAB
    gl = S5_SLAB // S5_GROUP
    dt = jnp.exp(log_step.astype(F32))[:, None]
    ar = a_re.astype(F32)
    ai = a_im.astype(F32)
    mag = jnp.exp(ar * dt)
    lb_re = mag * jnp.cos(ai * dt)
    lb_im = mag * jnp.sin(ai * dt)
    den = ar * ar + ai * ai
    nr = lb_re - 1.0
    coef_re = (nr * ar + lb_im * ai) / den
    coef_im = (lb_im * ar - nr * ai) / den
    bb_re = coef_re[..., None] * b_re - coef_im[..., None] * b_im
    bb_im = coef_re[..., None] * b_im + coef_im[..., None] * b_re
    eye = jnp.eye(gl, dtype=F32)

    def b_slab(t):
        t = t.reshape(n_slab, gl, S5_STATE, S5_GROUP)
        return jnp.einsum("kgpc,gh->kgchp", t, eye).reshape(n_slab, S5_SLAB, gl * S5_STATE)

    def c_slab(t):
        t = t.reshape(n_slab, gl, S5_GROUP, S5_STATE)
        return jnp.einsum("kgcp,gh->kgphc", t, eye).reshape(n_slab, gl * S5_STATE, S5_SLAB)

    b_dense = jnp.concatenate([b_slab(bb_re), b_slab(bb_im)], axis=2).astype(BF16)
    c_dense = jnp.concatenate([c_slab(c_re.astype(F32)), -c_slab(c_im.astype(F32))], axis=1).astype(BF16)

    def flat(t):
        return t.reshape(n_slab, gl * S5_STATE)

    def power(k):
        kk = k.astype(F32)[None, :, None]
        m = jnp.exp(flat(ar * dt)[:, None, :] * kk)
        ph = flat(ai * dt)[:, None, :] * kk
        return jnp.concatenate([m * jnp.cos(ph), m * jnp.sin(ph)], axis=2)

    lam = jnp.repeat(power(jnp.array([1])), SUBLANES, axis=1)
    ptab = jnp.repeat(power(jnp.arange(1, sub + 1)), SUBLANES, axis=1)
    apow = jnp.repeat(power(jnp.array([sub, 2 * sub, 4 * sub])), SUBLANES, axis=1)
    return b_dense, c_dense, lam, ptab, apow


def _s5_mixer(u, a_re, a_im, log_step, b_re, b_im, c_re, c_im, d_skip, sub=S5_SUB):
    seq, d = u.shape
    rows = SUBLANES * sub
    n_slab = d // S5_SLAB
    n2 = 2 * S5_SLAB_STATES
    b_dense, c_dense, lam, ptab, apow = _s5_tables(a_re, a_im, log_step, b_re, b_im, c_re, c_im, sub)
    return pl.pallas_call(
        functools.partial(_s5_kernel, sub=sub),
        out_shape=jax.ShapeDtypeStruct((seq, d), BF16),
        grid=(n_slab, seq // rows),
        in_specs=[
            pl.BlockSpec((rows, S5_SLAB), lambda k, c: (c, k)),
            pl.BlockSpec((1, S5_SLAB, n2), lambda k, c: (k, 0, 0)),
            pl.BlockSpec((1, n2, S5_SLAB), lambda k, c: (k, 0, 0)),
            pl.BlockSpec((1, SUBLANES, n2), lambda k, c: (k, 0, 0)),
            pl.BlockSpec((1, rows, n2), lambda k, c: (k, 0, 0)),
            pl.BlockSpec((1, 3 * SUBLANES, n2), lambda k, c: (k, 0, 0)),
            pl.BlockSpec((1, S5_SLAB), lambda k, c: (0, k)),
        ],
        out_specs=pl.BlockSpec((rows, S5_SLAB), lambda k, c: (c, k)),
        scratch_shapes=[pltpu.VMEM((rows, n2), F32),
                        pltpu.VMEM((SUBLANES, n2), F32),
                        pltpu.VMEM((rows, S5_SLAB), F32),
                        pltpu.VMEM((S5_SLAB // LANES, rows, LANES), F32)],
        compiler_params=_params("arbitrary", "arbitrary"),
        name="s5_scan",
    )(u, b_dense, c_dense, lam, ptab, apow, d_skip.reshape(1, d).astype(F32))


def _cache_weights(first, pairs):
    @pl.when(first)
    def _():
        for src, dst in pairs:
            dst[...] = src[...].astype(BF16)


def _glu_kernel(a_ref, wa_ref, wb_ref, r_ref, o_ref, wa_s, wb_s):
    _cache_weights(pl.program_id(1) == 0, ((wa_ref, wa_s), (wb_ref, wb_s)))
    a = a_ref[...]
    va = jnp.dot(a, wa_s[...], preferred_element_type=F32)
    vb = jnp.dot(a, wb_s[...], preferred_element_type=F32)
    o_ref[...] = r_ref[...] + va * jax.nn.sigmoid(vb)


def _glu_residual(a, w, res, tm=512, tn=512):
    m, k = a.shape
    n = w.shape[1] // 2
    nb = n // tn
    return pl.pallas_call(
        _glu_kernel,
        out_shape=jax.ShapeDtypeStruct((m, n), F32),
        grid=(nb, m // tm),
        in_specs=[pl.BlockSpec((tm, k), lambda j, i: (i, 0)),
                  pl.BlockSpec((k, tn), lambda j, i: (0, j)),
                  pl.BlockSpec((k, tn), lambda j, i: (0, j + nb)),
                  pl.BlockSpec((tm, tn), lambda j, i: (i, j))],
        out_specs=pl.BlockSpec((tm, tn), lambda j, i: (i, j)),
        scratch_shapes=[pltpu.VMEM((k, tn), BF16), pltpu.VMEM((k, tn), BF16)],
        compiler_params=_params("arbitrary", "arbitrary"),
        name="glu_residual",
    )(a, w, w, res)


def _swiglu_up_kernel(a_ref, wg_ref, wu_ref, o_ref, wg_s, wu_s):
    _cache_weights(pl.program_id(1) == 0, ((wg_ref, wg_s), (wu_ref, wu_s)))
    a = a_ref[...]
    vg = jnp.dot(a, wg_s[...], preferred_element_type=F32)
    vu = jnp.dot(a, wu_s[...], preferred_element_type=F32)
    o_ref[...] = (jax.nn.silu(vg) * vu).astype(o_ref.dtype)


def _swiglu_up(a, w_gu, tm=512, tn=512):
    m, k = a.shape
    f = w_gu.shape[1] // 2
    nb = f // tn
    return pl.pallas_call(
        _swiglu_up_kernel,
        out_shape=jax.ShapeDtypeStruct((m, f), BF16),
        grid=(nb, m // tm),
        in_specs=[pl.BlockSpec((tm, k), lambda j, i: (i, 0)),
                  pl.BlockSpec((k, tn), lambda j, i: (0, j)),
                  pl.BlockSpec((k, tn), lambda j, i: (0, j + nb))],
        out_specs=pl.BlockSpec((tm, tn), lambda j, i: (i, j)),
        scratch_shapes=[pltpu.VMEM((k, tn), BF16), pltpu.VMEM((k, tn), BF16)],
        compiler_params=_params("arbitrary", "arbitrary"),
        name="swiglu_up",
    )(a, w_gu, w_gu)


def _mm_res_kernel(a_ref, w_ref, r_ref, o_ref, w_s):
    _cache_weights(pl.program_id(1) == 0, ((w_ref, w_s),))
    o_ref[...] = r_ref[...] + jnp.dot(a_ref[...], w_s[...], preferred_element_type=F32)


def _matmul_residual(a, w, res, tm=512, tn=512):
    m, k = a.shape
    n = w.shape[1]
    return pl.pallas_call(
        _mm_res_kernel,
        out_shape=jax.ShapeDtypeStruct((m, n), F32),
        grid=(n // tn, m // tm),
        in_specs=[pl.BlockSpec((tm, k), lambda j, i: (i, 0)),
                  pl.BlockSpec((k, tn), lambda j, i: (0, j)),
                  pl.BlockSpec((tm, tn), lambda j, i: (i, j))],
        out_specs=pl.BlockSpec((tm, tn), lambda j, i: (i, j)),
        scratch_shapes=[pltpu.VMEM((k, tn), BF16)],
        compiler_params=_params("arbitrary", "arbitrary"),
        name="matmul_residual",
    )(a, w, res)


def _rope_kernel(pos_ref, inv_ref, cos_ref, sa_ref, sb_ref):
    ang = pos_ref[...].astype(F32) * inv_ref[...]
    c = jnp.cos(ang)
    s = jnp.sin(ang)
    lane = lax.broadcasted_iota(jnp.int32, ang.shape, 1)
    first_half = lane < ROPE_DIM // 2
    cos_ref[...] = c
    sa_ref[...] = jnp.where(first_half, -s, 0.0)
    sb_ref[...] = jnp.where(first_half, 0.0, s)


def _rope_tables(pos, tm=512):
    n = pos.shape[0]
    tm = min(tm, n)
    half = ROPE_DIM // 2
    inv = jnp.power(ROPE_THETA, -jnp.arange(half, dtype=F32) / half)
    inv = jnp.concatenate([inv, inv, jnp.zeros((LANES - ROPE_DIM,), F32)]).reshape(1, LANES)
    spec = pl.BlockSpec((tm, LANES), lambda i: (i, 0))
    return pl.pallas_call(
        _rope_kernel,
        out_shape=[jax.ShapeDtypeStruct((n, LANES), F32)] * 3,
        grid=(n // tm,),
        in_specs=[pl.BlockSpec((tm, 1), lambda i: (i, 0)),
                  pl.BlockSpec((1, LANES), lambda i: (0, 0))],
        out_specs=[spec, spec, spec],
        compiler_params=_params("arbitrary"),
        name="rope_tables",
    )(pos.reshape(n, 1), inv)


def _rope(y, c, sa, sb):
    half = ROPE_DIM // 2
    return y * c + pltpu.roll(y, LANES - half, 1) * sa + pltpu.roll(y, half, 1) * sb


def _head_norm(x, gain):
    return x * lax.rsqrt(jnp.mean(x * x, axis=-1, keepdims=True) + RMS_EPS) * gain


def _nsa_proj_kernel(a_ref, w_ref, gain_ref, cos_ref, sa_ref, sb_ref, o_ref, w_s, *, norm_tiles):
    j = pl.program_id(0)
    _cache_weights(pl.program_id(1) == 0, ((w_ref, w_s),))
    acc = jnp.dot(a_ref[...], w_s[...], preferred_element_type=F32)
    is_norm = functools.reduce(jnp.logical_or, [j == t for t in norm_tiles])

    @pl.when(is_norm)
    def _():
        c, sa, sb = cos_ref[...], sa_ref[...], sb_ref[...]
        gain = gain_ref[0]
        for hh in range(acc.shape[1] // HEAD_DIM):
            sl = slice(HEAD_DIM * hh, HEAD_DIM * (hh + 1))
            o_ref[:, sl] = _rope(_head_norm(acc[:, sl], gain), c, sa, sb).astype(o_ref.dtype)

    @pl.when(jnp.logical_not(is_norm))
    def _():
        o_ref[...] = acc.astype(o_ref.dtype)


def _nsa_proj(u, w_in, q_gain, k_gain, rope, tm=512, tn=512):
    m, k = u.shape
    q_dim = GQA_GROUP * N_KV_HEADS * HEAD_DIM
    kv_dim = N_KV_HEADS * HEAD_DIM
    assert kv_dim == tn
    n_q = q_dim // tn
    n_tiles = n_q + 6
    ones = jnp.ones((HEAD_DIM,), F32)
    q_scaled = q_gain.astype(F32) * (HEAD_DIM ** -0.5 * math.log2(math.e))
    gains = jnp.stack([q_scaled] * n_q + [ones, ones, k_gain[1], ones, k_gain[2], ones]).reshape(n_tiles, 1, HEAD_DIM)
    norm_tiles = tuple(range(n_q)) + (n_q + 2, n_q + 4)
    tab = pl.BlockSpec((tm, LANES), lambda j, i: (i, 0))
    return pl.pallas_call(
        functools.partial(_nsa_proj_kernel, norm_tiles=norm_tiles),
        out_shape=jax.ShapeDtypeStruct((m, n_tiles * tn), BF16),
        grid=(n_tiles, m // tm),
        in_specs=[pl.BlockSpec((tm, k), lambda j, i: (i, 0)),
                  pl.BlockSpec((k, tn), lambda j, i: (0, j)),
                  pl.BlockSpec((1, 1, HEAD_DIM), lambda j, i: (j, 0, 0)),
                  tab, tab, tab],
        out_specs=pl.BlockSpec((tm, tn), lambda j, i: (i, j)),
        scratch_shapes=[pltpu.VMEM((k, tn), BF16)],
        compiler_params=_params("arbitrary", "arbitrary"),
        name="nsa_proj",
    )(u, w_in, gains, *rope)


def _gate_kernel(a_ref, w_ref, o_ref):
    o_ref[...] = jax.nn.sigmoid(jnp.dot(a_ref[...], w_ref[...].astype(BF16), preferred_element_type=F32))


def _nsa_gates(u, w_gate, tm=512):
    m, k = u.shape
    n = w_gate.shape[1]
    w_pad = jnp.pad(w_gate, ((0, 0), (0, LANES - n)))
    return pl.pallas_call(
        _gate_kernel,
        out_shape=jax.ShapeDtypeStruct((m, LANES), F32),
        grid=(m // tm,),
        in_specs=[pl.BlockSpec((tm, k), lambda i: (i, 0)),
                  pl.BlockSpec((k, LANES), lambda i: (0, 0))],
        out_specs=pl.BlockSpec((tm, LANES), lambda i: (i, 0)),
        compiler_params=_params("arbitrary"),
        name="nsa_gates",
    )(u, w_pad)


def _compress_kernel(*refs, is_key):
    if is_key:
        ca_ref, cb_ref, pe_ref, w1_ref, w2_ref, gain_ref, cos_ref, sa_ref, sb_ref, o_ref = refs
    else:
        ca_ref, cb_ref, pe_ref, w1_ref, w2_ref, o_ref = refs
    half = w1_ref.shape[0] // 2
    pe = pe_ref[...]
    xa = (ca_ref[0].astype(F32) + pe[:, :half]).astype(BF16)
    xb = (cb_ref[0].astype(F32) + pe[:, half:]).astype(BF16)
    hid = (jnp.dot(xa, w1_ref[:half, :].astype(BF16), preferred_element_type=F32)
           + jnp.dot(xb, w1_ref[half:, :].astype(BF16), preferred_element_type=F32))
    out = jnp.dot(jax.nn.gelu(hid).astype(BF16), w2_ref[...].astype(BF16), preferred_element_type=F32)
    if is_key:
        out = _rope(_head_norm(out, gain_ref[...]), cos_ref[...], sa_ref[...], sb_ref[...])
    o_ref[0] = out.astype(o_ref.dtype)


def _compress(t, pe, w1, w2, key_extras=None):
    seq = t.shape[0]
    nc = seq // CMP_STRIDE
    width = CMP_STRIDE * HEAD_DIM
    ca = t.reshape(nc, CMP_STRIDE, N_KV_HEADS, HEAD_DIM).transpose(2, 0, 1, 3).reshape(N_KV_HEADS, nc, width)
    cb = jnp.concatenate([ca[:, 1:], jnp.zeros((N_KV_HEADS, 1, width), ca.dtype)], axis=1)
    blk = pl.BlockSpec((1, nc, width), lambda h: (h, 0, 0))
    full = lambda a: pl.BlockSpec(a.shape, lambda h: (0,) * a.ndim)
    args = [ca, cb, pe.reshape(1, CMP_BLOCK * HEAD_DIM), w1, w2]
    if key_extras is not None:
        args += list(key_extras)
    return pl.pallas_call(
        functools.partial(_compress_kernel, is_key=key_extras is not None),
        out_shape=jax.ShapeDtypeStruct((N_KV_HEADS, nc, HEAD_DIM), BF16),
        grid=(N_KV_HEADS,),
        in_specs=[blk, blk] + [full(a) for a in args[2:]],
        out_specs=pl.BlockSpec((1, nc, HEAD_DIM), lambda h: (h, 0, 0)),
        compiler_params=_params("arbitrary"),
        name="nsa_compress_k" if key_extras is not None else "nsa_compress_v",
    )(*args)


def _dot_nt(a, b):
    return lax.dot_general(a, b, (((1,), (1,)), ((), ())), preferred_element_type=F32)


def _split3(x):
    hi = x.astype(BF16)
    r1 = x - hi.astype(F32)
    mid = r1.astype(BF16)
    lo = (r1 - mid.astype(F32)).astype(BF16)
    return hi, mid, lo


def _nsa_attn_kernel(q_ref, kc_ref, vc_ref, ks_ref, vs_ref, kw_ref, vw_ref, gate_ref, o_ref, *, seq, tk):
    t0 = pl.program_id(1) * Q_BLOCK
    nc = kc_ref.shape[1]
    ns = seq // SEL_BLOCK
    grp = GQA_GROUP
    sel_shift = int(math.log2(SEL_BLOCK))
    q = q_ref[...]
    qs = jnp.concatenate([q[:, HEAD_DIM * g:HEAD_DIM * (g + 1)] for g in range(grp)], axis=0)
    t_col = t0 + lax.broadcasted_iota(jnp.int32, (Q_BLOCK, 1), 0)

    def biased_exp(s, bias):
        sb = s + jnp.concatenate([bias] * grp, axis=0)
        m = jnp.max(sb, axis=-1, keepdims=True)
        return jnp.exp2(sb - m), m

    n_idx = lax.broadcasted_iota(jnp.int32, (Q_BLOCK, nc), 1)
    ok_c = (n_idx * CMP_STRIDE + (CMP_BLOCK - 1) <= t_col) & (n_idx < nc - 1)
    e_c, m_c = biased_exp(_dot_nt(qs, kc_ref[0]), jnp.where(ok_c, 0.0, NEG))
    den_c = jnp.maximum(jnp.sum(e_c, axis=-1, keepdims=True), 1e-30)
    p_c = e_c * jnp.where(m_c > 0.5 * NEG, 1.0 / den_c, 0.0)
    o_c = jnp.dot(p_c.astype(BF16), vc_ref[0], preferred_element_type=F32)

    imp = p_c[:Q_BLOCK]
    for g in range(1, grp):
        imp = imp + p_c[Q_BLOCK * g:Q_BLOCK * (g + 1)]
    ratio = SEL_BLOCK // CMP_STRIDE
    d = (lax.broadcasted_iota(jnp.int32, (ns, nc), 1)
         - ratio * lax.broadcasted_iota(jnp.int32, (ns, nc), 0))
    overlap = jnp.zeros((ns, nc), F32)
    for n in range(CMP_BLOCK // CMP_STRIDE):
        overlap = overlap + jnp.where((d - n >= 0) & (d - n < ratio), 1.0, 0.0)
    overlap = overlap.astype(BF16)
    p_slc = sum(_dot_nt(overlap, part) for part in _split3(imp))

    j_idx = lax.broadcasted_iota(jnp.int32, (ns, Q_BLOCK), 0)
    j_f = j_idx.astype(F32)
    t_row = t0 + lax.broadcasted_iota(jnp.int32, (1, Q_BLOCK), 1)
    dist = jnp.right_shift(t_row, sel_shift) - j_idx
    forced = (j_idx == 0) | ((dist >= 0) & (dist < SEL_LOCAL))
    score = jnp.where(forced, jnp.inf, jnp.where(dist >= 0, p_slc, -jnp.inf))
    sel = jnp.zeros((ns, Q_BLOCK), F32)
    for _ in range(min(SEL_TOPK, ns)):
        top = jnp.max(score, axis=0, keepdims=True)
        idx = jnp.min(jnp.where(score == top, j_f, float(ns)), axis=0, keepdims=True)
        pick = j_f == idx
        sel = jnp.where(pick, 1.0, sel)
        score = jnp.where(pick, -jnp.inf, score)
    sel_b = sel.T.astype(BF16)

    blocks_per_tile = tk // SEL_BLOCK
    tile_block = (lax.broadcasted_iota(jnp.int32, (ns, tk), 0)
                  - jnp.right_shift(lax.broadcasted_iota(jnp.int32, (ns, tk), 1), sel_shift))

    def sel_step(kt, carry, diagonal):
        m_i, l_i, acc = carry
        k0 = pl.multiple_of(kt * tk, tk)
        expand = jnp.where(tile_block == kt * blocks_per_tile, 1.0, 0.0).astype(BF16)
        bias = (jnp.dot(sel_b, expand, preferred_element_type=F32) - 1.0) * (-NEG)
        if diagonal:
            kpos = k0 + lax.broadcasted_iota(jnp.int32, (Q_BLOCK, tk), 1)
            bias = bias + jnp.where(kpos <= t_col, 0.0, NEG)
        sb = _dot_nt(qs, ks_ref[pl.ds(k0, tk), :]) + jnp.concatenate([bias] * grp, axis=0)
        m_new = jnp.maximum(m_i, jnp.max(sb, axis=-1, keepdims=True))
        e = jnp.exp2(sb - m_new)
        alpha = jnp.exp2(m_i - m_new)
        l_new = alpha * l_i + jnp.sum(e, axis=-1, keepdims=True)
        acc = alpha * acc + jnp.dot(e.astype(BF16), vs_ref[pl.ds(k0, tk), :], preferred_element_type=F32)
        return m_new, l_new, acc

    rows = grp * Q_BLOCK
    init = (jnp.full((rows, 1), NEG, F32), jnp.zeros((rows, 1), F32), jnp.zeros((rows, HEAD_DIM), F32))
    last = t0 // tk
    carry = lax.fori_loop(0, last, functools.partial(sel_step, diagonal=False), init)
    _, l_s, acc_s = sel_step(last, carry, True)
    o_s = acc_s * (1.0 / jnp.maximum(l_s, 1e-30))

    span = WINDOW + Q_BLOCK
    w0 = pl.multiple_of(jnp.maximum(t0 - WINDOW, 0), Q_BLOCK)
    rel = t_col - (w0 + lax.broadcasted_iota(jnp.int32, (Q_BLOCK, span), 1))
    e_w, _ = biased_exp(_dot_nt(qs, kw_ref[pl.ds(w0, span), :]),
                        jnp.where((rel >= 0) & (rel < WINDOW), 0.0, NEG))
    p_w = e_w * (1.0 / jnp.maximum(jnp.sum(e_w, axis=-1, keepdims=True), 1e-30))
    o_w = jnp.dot(p_w.astype(BF16), vw_ref[pl.ds(w0, span), :], preferred_element_type=F32)

    gate = gate_ref[0]
    for g in range(grp):
        sl = slice(Q_BLOCK * g, Q_BLOCK * (g + 1))
        mixed = (gate[:, g:g + 1] * o_c[sl] + gate[:, grp + g:grp + g + 1] * o_s[sl]
                 + gate[:, 2 * grp + g:2 * grp + g + 1] * o_w[sl])
        o_ref[:, HEAD_DIM * g:HEAD_DIM * (g + 1)] = mixed.astype(o_ref.dtype)


def _nsa_attention(proj, kcmp, vcmp, gates, tk=512):
    seq = proj.shape[0]
    tk = min(tk, seq)
    q_dim = GQA_GROUP * N_KV_HEADS * HEAD_DIM
    kv_blocks = N_KV_HEADS
    first = q_dim // HEAD_DIM + 2 * kv_blocks
    nc = kcmp.shape[1]

    def kv_spec(which):
        return pl.BlockSpec((seq, HEAD_DIM), lambda h, qb: (0, first + which * kv_blocks + h))

    cmp_spec = pl.BlockSpec((1, nc, HEAD_DIM), lambda h, qb: (h, 0, 0))
    q_spec = pl.BlockSpec((Q_BLOCK, GQA_GROUP * HEAD_DIM), lambda h, qb: (qb, h))
    n_gate = 3 * GQA_GROUP
    return pl.pallas_call(
        functools.partial(_nsa_attn_kernel, seq=seq, tk=tk),
        out_shape=jax.ShapeDtypeStruct((seq, q_dim), BF16),
        grid=(N_KV_HEADS, seq // Q_BLOCK),
        in_specs=[q_spec, cmp_spec, cmp_spec, kv_spec(0), kv_spec(1), kv_spec(2), kv_spec(3),
                  pl.BlockSpec((1, Q_BLOCK, n_gate), lambda h, qb: (h, qb, 0))],
        out_specs=q_spec,
        compiler_params=_params("arbitrary", "arbitrary"),
        name="nsa_attention",
    )(proj, kcmp, vcmp, proj, proj, proj, proj, gates)


def _nsa_mixer(u, positions, w_in, q_gain, k_gain, pe_k, pe_v, ck_w1, ck_w2, cv_w1, cv_w2):
    seq = u.shape[0]
    q_dim = GQA_GROUP * N_KV_HEADS * HEAD_DIM
    kv_dim = N_KV_HEADS * HEAD_DIM
    n_main = q_dim + 6 * kv_dim
    nc = seq // CMP_STRIDE
    rope = _rope_tables(positions)
    proj = _nsa_proj(u, w_in, q_gain, k_gain, rope)
    gate = _nsa_gates(u, w_in[:, n_main:])
    gates = (gate[:, :3 * N_KV_HEADS * GQA_GROUP].reshape(seq, 3, N_KV_HEADS, GQA_GROUP)
             .transpose(2, 0, 1, 3).reshape(N_KV_HEADS, seq, 3 * GQA_GROUP))
    pos_cmp = jnp.concatenate([positions[CMP_BLOCK - 1::CMP_STRIDE][:nc - 1], jnp.zeros((1,), positions.dtype)])
    rope_cmp = _rope_tables(pos_cmp)
    kcmp = _compress(proj[:, q_dim:q_dim + kv_dim], pe_k, ck_w1, ck_w2,
                     key_extras=(k_gain[0].reshape(1, HEAD_DIM),) + tuple(rope_cmp))
    vcmp = _compress(proj[:, q_dim + kv_dim:q_dim + 2 * kv_dim], pe_v, cv_w1, cv_w2)
    return _nsa_attention(proj, kcmp, vcmp, gates)


def _router_kernel(x_ref, g_ref, w_ref, b_ref, u_ref, r_ref):
    x = x_ref[...]
    u = x * lax.rsqrt(jnp.mean(x * x, axis=-1, keepdims=True) + RMS_EPS) * g_ref[...]
    u_ref[...] = u
    uh, um, _ = _split3(u)
    wh, wm, _ = _split3(w_ref[...])
    logits = (jnp.dot(uh, wh, preferred_element_type=F32) + jnp.dot(uh, wm, preferred_element_type=F32)
              + jnp.dot(um, wh, preferred_element_type=F32)) + b_ref[...]
    lane = lax.broadcasted_iota(jnp.int32, logits.shape, 1).astype(F32)
    lg = jnp.where(lane < N_EXPERTS, logits, -jnp.inf)
    v1 = jnp.max(lg, axis=-1, keepdims=True)
    i1 = jnp.min(jnp.where(lg == v1, lane, float(LANES)), axis=-1, keepdims=True)
    lg = jnp.where(lane == i1, -jnp.inf, lg)
    v2 = jnp.max(lg, axis=-1, keepdims=True)
    i2 = jnp.min(jnp.where(lg == v2, lane, float(LANES)), axis=-1, keepdims=True)
    e2 = jnp.exp(v2 - v1)
    den = 1.0 + e2
    r_ref[...] = jnp.where(lane == 0, i1, jnp.where(lane == 1, i2, jnp.where(
        lane == 2, 1.0 / den, jnp.where(lane == 3, e2 / den, 0.0))))


def _router(h, gain, w_router, b_router, tm=256):
    m, d = h.shape
    w_pad = jnp.pad(w_router.astype(F32), ((0, 0), (0, LANES - N_EXPERTS)))
    b_pad = jnp.pad(b_router.astype(F32), (0, LANES - N_EXPERTS)).reshape(1, LANES)
    return pl.pallas_call(
        _router_kernel,
        out_shape=[jax.ShapeDtypeStruct((m, d), F32), jax.ShapeDtypeStruct((m, LANES), F32)],
        grid=(m // tm,),
        in_specs=[pl.BlockSpec((tm, d), lambda i: (i, 0)),
                  pl.BlockSpec((1, d), lambda i: (0, 0)),
                  pl.BlockSpec((d, LANES), lambda i: (0, 0)),
                  pl.BlockSpec((1, LANES), lambda i: (0, 0))],
        out_specs=[pl.BlockSpec((tm, d), lambda i: (i, 0)), pl.BlockSpec((tm, LANES), lambda i: (i, 0))],
        compiler_params=_params("arbitrary"),
        name="moe_router",
    )(h, gain.reshape(1, d), w_pad, b_pad)


def _row_copy(src_hbm, row, dst, r, sem):
    return pltpu.make_async_copy(src_hbm.at[pl.ds(row, 1), :], dst.at[pl.ds(r, 1), :], sem)


def _gather_kernel(idx_ref, src_hbm, o_ref, buf, sem):
    rows = o_ref.shape[0]
    i = pl.program_id(0)

    def issue(blk):
        slot = blk % 2

        def start(r, _):
            _row_copy(src_hbm, idx_ref[blk * rows + r], buf.at[slot], r, sem.at[slot]).start()
            return 0

        lax.fori_loop(0, rows, start, 0)

    @pl.when(i == 0)
    def _():
        issue(i)

    @pl.when(i + 1 < pl.num_programs(0))
    def _():
        issue(i + 1)

    slot = i % 2

    def wait(r, _):
        _row_copy(src_hbm, 0, buf.at[slot], r, sem.at[slot]).wait()
        return 0

    lax.fori_loop(0, rows, wait, 0)
    o_ref[...] = buf[slot].astype(o_ref.dtype)


def _gather_rows(src, idx, out_dtype, rows=MOE_ROWS):
    n = idx.shape[0]
    d = src.shape[1]
    return pl.pallas_call(
        _gather_kernel,
        out_shape=jax.ShapeDtypeStruct((n, d), out_dtype),
        grid_spec=pltpu.PrefetchScalarGridSpec(
            num_scalar_prefetch=1,
            grid=(n // rows,),
            in_specs=[pl.BlockSpec(memory_space=pl.ANY)],
            out_specs=pl.BlockSpec((rows, d), lambda i, idx: (i, 0)),
            scratch_shapes=[pltpu.VMEM((2, rows, d), src.dtype), pltpu.SemaphoreType.DMA((2,))]),
        compiler_params=_params("arbitrary"),
        name="moe_gather",
    )(idx, src)


def _expert_changed(be_ref, i):
    return (i == 0) | (be_ref[i] != be_ref[jnp.maximum(i - 1, 0)])


def _moe_up_kernel(be_ref, x_ref, wg_ref, wu_ref, o_ref, wg_s, wu_s):
    first = _expert_changed(be_ref, pl.program_id(1))

    @pl.when(first)
    def _():
        wg_s[...] = wg_ref[0].astype(BF16)
        wu_s[...] = wu_ref[0].astype(BF16)

    a = x_ref[...]
    vg = jnp.dot(a, wg_s[...], preferred_element_type=F32)
    vu = jnp.dot(a, wu_s[...], preferred_element_type=F32)
    o_ref[...] = (jax.nn.silu(vg) * vu).astype(o_ref.dtype)


def _moe_up(x_rows, blk_e, w_gu, tn=512, rows=MOE_ROWS):
    n, k = x_rows.shape
    f = w_gu.shape[2] // 2
    nb = f // tn
    return pl.pallas_call(
        _moe_up_kernel,
        out_shape=jax.ShapeDtypeStruct((n, f), BF16),
        grid_spec=pltpu.PrefetchScalarGridSpec(
            num_scalar_prefetch=1,
            grid=(nb, n // rows),
            in_specs=[pl.BlockSpec((rows, k), lambda j, i, be: (i, 0)),
                      pl.BlockSpec((1, k, tn), lambda j, i, be: (be[i], 0, j)),
                      pl.BlockSpec((1, k, tn), lambda j, i, be: (be[i], 0, j + nb))],
            out_specs=pl.BlockSpec((rows, tn), lambda j, i, be: (i, j)),
            scratch_shapes=[pltpu.VMEM((k, tn), BF16), pltpu.VMEM((k, tn), BF16)]),
        compiler_params=_params("arbitrary", "arbitrary"),
        name="moe_up",
    )(blk_e, x_rows, w_gu, w_gu)


def _moe_down_kernel(be_ref, a_ref, w_ref, o_ref, w_s):
    first = _expert_changed(be_ref, pl.program_id(1))

    @pl.when(first)
    def _():
        w_s[...] = w_ref[0].astype(BF16)

    o_ref[...] = jnp.dot(a_ref[...], w_s[...], preferred_element_type=F32)


def _moe_down(act, blk_e, w_down, tn=512, rows=MOE_ROWS):
    n, k = act.shape
    d = w_down.shape[2]
    return pl.pallas_call(
        _moe_down_kernel,
        out_shape=jax.ShapeDtypeStruct((n, d), F32),
        grid_spec=pltpu.PrefetchScalarGridSpec(
            num_scalar_prefetch=1,
            grid=(d // tn, n // rows),
            in_specs=[pl.BlockSpec((rows, k), lambda j, i, be: (i, 0)),
                      pl.BlockSpec((1, k, tn), lambda j, i, be: (be[i], 0, j))],
            out_specs=pl.BlockSpec((rows, tn), lambda j, i, be: (i, j)),
            scratch_shapes=[pltpu.VMEM((k, tn), BF16)]),
        compiler_params=_params("arbitrary", "arbitrary"),
        name="moe_down",
    )(blk_e, act, w_down)


def _combine_kernel(dest_ref, h_ref, r_ref, rows_hbm, o_ref, buf, sem):
    tm = h_ref.shape[0]
    base = pl.program_id(0) * tm

    def start(r, _):
        for k in range(2):
            _row_copy(rows_hbm, dest_ref[2 * (base + r) + k], buf.at[k], r, sem.at[k]).start()
        return 0

    def wait(r, _):
        for k in range(2):
            _row_copy(rows_hbm, 0, buf.at[k], r, sem.at[k]).wait()
        return 0

    lax.fori_loop(0, tm, start, 0)
    lax.fori_loop(0, tm, wait, 0)
    w = r_ref[...]
    o_ref[...] = h_ref[...] + (w[:, 2:3] * buf[0] + w[:, 3:4] * buf[1])


def _moe_combine(h, route, out_rows, dest, tm=256):
    m, d = h.shape
    return pl.pallas_call(
        _combine_kernel,
        out_shape=jax.ShapeDtypeStruct((m, d), F32),
        grid_spec=pltpu.PrefetchScalarGridSpec(
            num_scalar_prefetch=1,
            grid=(m // tm,),
            in_specs=[pl.BlockSpec((tm, d), lambda i, dest: (i, 0)),
                      pl.BlockSpec((tm, LANES), lambda i, dest: (i, 0)),
                      pl.BlockSpec(memory_space=pl.ANY)],
            out_specs=pl.BlockSpec((tm, d), lambda i, dest: (i, 0)),
            scratch_shapes=[pltpu.VMEM((2, tm, d), F32), pltpu.SemaphoreType.DMA((2,))]),
        compiler_params=_params("arbitrary"),
        name="moe_combine",
    )(dest.reshape(-1), h, route, out_rows)


def _moe_layout(top_e, rows=MOE_ROWS):
    n_tok = top_e.shape[0]
    e_flat = top_e.reshape(-1)
    onehot = (e_flat[:, None] == jnp.arange(N_EXPERTS, dtype=jnp.int32)[None, :]).astype(jnp.int32)
    csum = jnp.cumsum(onehot, axis=0)
    rank = jnp.take_along_axis(csum, e_flat[:, None], axis=1)[:, 0] - 1
    counts = csum[-1]
    padded = (counts + rows - 1) // rows * rows
    pad_end = jnp.cumsum(padded)
    dest = (pad_end - padded)[e_flat] + rank
    n_rows = e_flat.shape[0] + N_EXPERTS * rows
    t_flat = jnp.repeat(jnp.arange(n_tok, dtype=jnp.int32), top_e.shape[1])
    row_tok = jnp.zeros((n_rows,), jnp.int32).at[dest].set(t_flat)
    n_blk = n_rows // rows
    blk_e = jnp.minimum(jnp.searchsorted(pad_end, jnp.arange(n_blk, dtype=jnp.int32) * rows, side="right"),
                        N_EXPERTS - 1).astype(jnp.int32)
    return row_tok, blk_e, dest.astype(jnp.int32).reshape(n_tok, -1)


def _moe_ffn_residual(h, gain, w_router, b_router, w_gu, w_down):
    u, route = _router(h, gain, w_router, b_router)
    top_e = route[:, :2].astype(jnp.int32)
    row_tok, blk_e, dest = _moe_layout(top_e)
    x_rows = _gather_rows(u, row_tok, BF16)
    act = _moe_up(x_rows, blk_e, w_gu)
    out_rows = _moe_down(act, blk_e, w_down)
    return _moe_combine(h, route, out_rows, dest)


def kernel(x, positions, norm_mix, norm_ffn, s5_a_re, s5_a_im, s5_log_step, s5_b_re, s5_b_im, s5_c_re, s5_c_im, s5_d, s5_w_glu, nsa_w_in, nsa_q_gain, nsa_k_gain, nsa_pe_k, nsa_pe_v, nsa_ck_w1, nsa_ck_w2, nsa_cv_w1, nsa_cv_w2, nsa_w_out, ffn_w_gu, ffn_w_down, moe_w_router, moe_b_router, moe_w_gu, moe_w_down):
    bsz, seq, d = x.shape
    assert bsz == 1, "the scan and attention kernels take one sequence"
    h = x.reshape(seq, d)
    h = _layer_s5(h, norm_mix[0], norm_ffn[0], s5_a_re[0], s5_a_im[0], s5_log_step[0], s5_b_re[0],
                  s5_b_im[0], s5_c_re[0], s5_c_im[0], s5_d[0], s5_w_glu[0], ffn_w_gu[0], ffn_w_down[0])
    h = _layer_nsa(h, positions[0], norm_mix[1], norm_ffn[1], nsa_w_in[0], nsa_q_gain[0], nsa_k_gain[0],
                   nsa_pe_k[0], nsa_pe_v[0], nsa_ck_w1[0], nsa_ck_w2[0], nsa_cv_w1[0], nsa_cv_w2[0],
                   nsa_w_out[0], moe_w_router[0], moe_b_router[0], moe_w_gu[0], moe_w_down[0])
    return h.reshape(bsz, seq, d)


def _layer_nsa(h, positions, g_mix, g_ffn, w_in, q_gain, k_gain, pe_k, pe_v, ck_w1, ck_w2, cv_w1, cv_w2,
               w_out, w_router, b_router, w_gu, w_down):
    u = _rms_norm(h, g_mix, BF16)
    o = _nsa_mixer(u, positions, w_in, q_gain, k_gain, pe_k, pe_v, ck_w1, ck_w2, cv_w1, cv_w2)
    h = _matmul_residual(o, w_out, h)
    return _moe_ffn_residual(h, g_ffn, w_router, b_router, w_gu, w_down)


def _layer_s5(h, g_mix, g_ffn, a_re, a_im, log_step, b_re, b_im, c_re, c_im, d_skip, w_glu, w_gu, w_down):
    u = _rms_norm(h, g_mix, F32)
    g = _s5_mixer(u, a_re, a_im, log_step, b_re, b_im, c_re, c_im, d_skip)
    h = _glu_residual(g, w_glu, h)
    u = _rms_norm(h, g_ffn, BF16)
    act = _swiglu_up(u, w_gu)
    return _matmul_residual(act, w_down, h)
```

```python
import functools
import math

import jax
import jax.numpy as jnp
from jax import lax
from jax.experimental import pallas as pl
from jax.experimental.pallas import tpu as pltpu

F32 = jnp.float32
BF16 = jnp.bfloat16

RMS_EPS = 1e-6
S5_GROUP = 16
S5_STATE = 64
HEAD_DIM = 128
N_KV_HEADS = 4
GQA_GROUP = 4
ROPE_DIM = 32
ROPE_THETA = 500000.0
CMP_BLOCK = 32
CMP_STRIDE = 16
SEL_BLOCK = 64
SEL_TOPK = 16
SEL_LOCAL = 2
WINDOW = 512
Q_BLOCK = 128
N_EXPERTS = 8
NEG = -1e30

LANES = 128
SUBLANES = 8
VMEM_LIMIT = 56 * 1024 * 1024

S5_SLAB = 256
S5_SLAB_STATES = S5_SLAB // S5_GROUP * S5_STATE
S5_SUB = 64
MOE_ROWS = 512
DMA_UNROLL = 8


def _params(*sem):
    return pltpu.CompilerParams(dimension_semantics=sem, vmem_limit_bytes=VMEM_LIMIT)


def _rms_kernel(x_ref, g_ref, o_ref):
    x = x_ref[...]
    ms = jnp.mean(x * x, axis=-1, keepdims=True)
    o_ref[...] = (x * lax.rsqrt(ms + RMS_EPS) * g_ref[...]).astype(o_ref.dtype)


def _rms_norm(x, gain, out_dtype, tm=512):
    m, d = x.shape
    return pl.pallas_call(
        _rms_kernel,
        out_shape=jax.ShapeDtypeStruct((m, d), out_dtype),
        grid=(m // tm,),
        in_specs=[pl.BlockSpec((tm, d), lambda i: (i, 0)),
                  pl.BlockSpec((1, d), lambda i: (0, 0))],
        out_specs=pl.BlockSpec((tm, d), lambda i: (i, 0)),
        compiler_params=_params("arbitrary"),
        name="rms_norm",
    )(x, gain.reshape(1, d))


def _s5_kernel(u_ref, b_ref, c_ref, lam_ref, ptab_ref, apow_ref, d_ref, o_ref,
               xs_ref, carry_ref, up_ref, us_ref, *, sub):
    n = S5_SLAB_STATES

    @pl.when(pl.program_id(1) == 0)
    def _():
        carry_ref[...] = jnp.zeros_like(carry_ref)

    halves = S5_SLAB // LANES
    for c in range(halves):
        us_ref[c] = u_ref[:, LANES * c:LANES * (c + 1)]
    for i in range(sub):
        for c in range(halves):
            up_ref[SUBLANES * i:SUBLANES * (i + 1), LANES * c:LANES * (c + 1)] = (
                us_ref[c, pl.ds(i, SUBLANES, stride=sub), :])
    up = up_ref[...]
    xs_ref[...] = jnp.dot(up.astype(BF16), b_ref[0], preferred_element_type=F32)

    lam = lam_ref[0]
    lr, li = lam[:, :n], lam[:, n:]

    def local_scan(i, h):
        hr, hi = h
        off = pl.multiple_of(i * SUBLANES, SUBLANES)
        x = xs_ref[pl.ds(off, SUBLANES), :]
        nr = lr * hr - li * hi + x[:, :n]
        ni = lr * hi + li * hr + x[:, n:]
        xs_ref[pl.ds(off, SUBLANES), :] = jnp.concatenate([nr, ni], axis=1)
        return nr, ni

    zero = jnp.zeros((SUBLANES, n), F32)
    er, ei = lax.fori_loop(0, sub, local_scan, (zero, zero))

    row = lax.broadcasted_iota(jnp.int32, (SUBLANES, n), 0)
    cin = carry_ref[...]
    zr = jnp.where(row == 0, cin[:, :n], pltpu.roll(er, 1, 0))
    zi = jnp.where(row == 0, cin[:, n:], pltpu.roll(ei, 1, 0))
    apow = apow_ref[0]
    for s, d in enumerate((1, 2, 4)):
        ar = apow[SUBLANES * s:SUBLANES * (s + 1), :n]
        ai = apow[SUBLANES * s:SUBLANES * (s + 1), n:]
        sr = pltpu.roll(zr, d, 0)
        si = pltpu.roll(zi, d, 0)
        keep = row >= d
        zr, zi = (zr + jnp.where(keep, ar * sr - ai * si, 0.0),
                  zi + jnp.where(keep, ar * si + ai * sr, 0.0))
    a1r, a1i = apow[:SUBLANES, :n], apow[:SUBLANES, n:]
    nxt_r = a1r * zr - a1i * zi + er
    nxt_i = a1r * zi + a1i * zr + ei
    carry_ref[...] = jnp.concatenate(
        [jnp.broadcast_to(nxt_r[SUBLANES - 1:, :], (SUBLANES, n)),
         jnp.broadcast_to(nxt_i[SUBLANES - 1:, :], (SUBLANES, n))], axis=1)

    def add_carry(i, _):
        off = pl.multiple_of(i * SUBLANES, SUBLANES)
        x = xs_ref[pl.ds(off, SUBLANES), :]
        p = ptab_ref[0, pl.ds(off, SUBLANES), :]
        pr, pi = p[:, :n], p[:, n:]
        nr = x[:, :n] + pr * zr - pi * zi
        ni = x[:, n:] + pr * zi + pi * zr
        xs_ref[pl.ds(off, SUBLANES), :] = jnp.concatenate([nr, ni], axis=1)
        return 0

    lax.fori_loop(0, sub, add_carry, 0)

    y = jnp.dot(xs_ref[...].astype(BF16), c_ref[0], preferred_element_type=F32)
    g = jax.nn.gelu(y + d_ref[...] * up)
    for c in range(halves):
        us_ref[c] = g[:, LANES * c:LANES * (c + 1)]
    for j in range(SUBLANES):
        for c in range(halves):
            o_ref[sub * j:sub * (j + 1), LANES * c:LANES * (c + 1)] = (
                us_ref[c, pl.ds(j, sub, stride=SUBLANES), :].astype(o_ref.dtype))


def _s5_tables(a_re, a_im, log_step, b_re, b_im, c_re, c_im, sub):
    g = a_re.shape[0]
    n_slab = g * S5_GROUP // S5_SLAB
    gl = S5_SLAB // S5_GROUP
    dt = jnp.exp(log_step.astype(F32))[:, None]
    ar = a_re.astype(F32)
    ai = a_im.astype(F32)
    mag = jnp.exp(ar * dt)
    lb_re = mag * jnp.cos(ai * dt)
    lb_im = mag * jnp.sin(ai * dt)
    den = ar * ar + ai * ai
    nr = lb_re - 1.0
    coef_re = (nr * ar + lb_im * ai) / den
    coef_im = (lb_im * ar - nr * ai) / den
    bb_re = coef_re[..., None] * b_re - coef_im[..., None] * b_im
    bb_im = coef_re[..., None] * b_im + coef_im[..., None] * b_re
    eye = jnp.eye(gl, dtype=F32)

    def b_slab(t):
        t = t.reshape(n_slab, gl, S5_STATE, S5_GROUP)
        return jnp.einsum("kgpc,gh->kgchp", t, eye).reshape(n_slab, S5_SLAB, gl * S5_STATE)

    def c_slab(t):
        t = t.reshape(n_slab, gl, S5_GROUP, S5_STATE)
        return jnp.einsum("kgcp,gh->kgphc", t, eye).reshape(n_slab, gl * S5_STATE, S5_SLAB)

    b_dense = jnp.concatenate([b_slab(bb_re), b_slab(bb_im)], axis=2).astype(BF16)
    c_dense = jnp.concatenate([c_slab(c_re.astype(F32)), -c_slab(c_im.astype(F32))], axis=1).astype(BF16)

    def flat(t):
        return t.reshape(n_slab, gl * S5_STATE)

    def power(k):
        kk = k.astype(F32)[None, :, None]
        m = jnp.exp(flat(ar * dt)[:, None, :] * kk)
        ph = flat(ai * dt)[:, None, :] * kk
        return jnp.concatenate([m * jnp.cos(ph), m * jnp.sin(ph)], axis=2)

    lam = jnp.repeat(power(jnp.array([1])), SUBLANES, axis=1)
    ptab = jnp.repeat(power(jnp.arange(1, sub + 1)), SUBLANES, axis=1)
    apow = jnp.repeat(power(jnp.array([sub, 2 * sub, 4 * sub])), SUBLANES, axis=1)
    return b_dense, c_dense, lam, ptab, apow


def _s5_mixer(u, a_re, a_im, log_step, b_re, b_im, c_re, c_im, d_skip, sub=S5_SUB):
    seq, d = u.shape
    rows = SUBLANES * sub
    n_slab = d // S5_SLAB
    n2 = 2 * S5_SLAB_STATES
    b_dense, c_dense, lam, ptab, apow = _s5_tables(a_re, a_im, log_step, b_re, b_im, c_re, c_im, sub)
    return pl.pallas_call(
        functools.partial(_s5_kernel, sub=sub),
        out_shape=jax.ShapeDtypeStruct((seq, d), BF16),
        grid=(n_slab, seq // rows),
        in_specs=[
            pl.BlockSpec((rows, S5_SLAB), lambda k, c: (c, k)),
            pl.BlockSpec((1, S5_SLAB, n2), lambda k, c: (k, 0, 0)),
            pl.BlockSpec((1, n2, S5_SLAB), lambda k, c: (k, 0, 0)),
            pl.BlockSpec((1, SUBLANES, n2), lambda k, c: (k, 0, 0)),
            pl.BlockSpec((1, rows, n2), lambda k, c: (k, 0, 0)),
            pl.BlockSpec((1, 3 * SUBLANES, n2), lambda k, c: (k, 0, 0)),
            pl.BlockSpec((1, S5_SLAB), lambda k, c: (0, k)),
        ],
        out_specs=pl.BlockSpec((rows, S5_SLAB), lambda k, c: (c, k)),
        scratch_shapes=[pltpu.VMEM((rows, n2), F32),
                        pltpu.VMEM((SUBLANES, n2), F32),
                        pltpu.VMEM((rows, S5_SLAB), F32),
                        pltpu.VMEM((S5_SLAB // LANES, rows, LANES), F32)],
        compiler_params=_params("arbitrary", "arbitrary"),
        name="s5_scan",
    )(u, b_dense, c_dense, lam, ptab, apow, d_skip.reshape(1, d).astype(F32))


def _cache_weights(first, pairs):
    @pl.when(first)
    def _():
        for src, dst in pairs:
            dst[...] = src[...].astype(BF16)


def _glu_kernel(a_ref, wa_ref, wb_ref, r_ref, o_ref, wa_s, wb_s):
    _cache_weights(pl.program_id(1) == 0, ((wa_ref, wa_s), (wb_ref, wb_s)))
    a = a_ref[...]
    va = jnp.dot(a, wa_s[...], preferred_element_type=F32)
    vb = jnp.dot(a, wb_s[...], preferred_element_type=F32)
    o_ref[...] = r_ref[...] + va * jax.nn.sigmoid(vb)


def _glu_residual(a, w, res, tm=512, tn=512):
    m, k = a.shape
    n = w.shape[1] // 2
    nb = n // tn
    return pl.pallas_call(
        _glu_kernel,
        out_shape=jax.ShapeDtypeStruct((m, n), F32),
        grid=(nb, m // tm),
        in_specs=[pl.BlockSpec((tm, k), lambda j, i: (i, 0)),
                  pl.BlockSpec((k, tn), lambda j, i: (0, j)),
                  pl.BlockSpec((k, tn), lambda j, i: (0, j + nb)),
                  pl.BlockSpec((tm, tn), lambda j, i: (i, j))],
        out_specs=pl.BlockSpec((tm, tn), lambda j, i: (i, j)),
        scratch_shapes=[pltpu.VMEM((k, tn), BF16), pltpu.VMEM((k, tn), BF16)],
        compiler_params=_params("arbitrary", "arbitrary"),
        name="glu_residual",
    )(a, w, w, res)


def _swiglu_up_kernel(a_ref, wg_ref, wu_ref, o_ref, wg_s, wu_s):
    _cache_weights(pl.program_id(1) == 0, ((wg_ref, wg_s), (wu_ref, wu_s)))
    a = a_ref[...]
    vg = jnp.dot(a, wg_s[...], preferred_element_type=F32)
    vu = jnp.dot(a, wu_s[...], preferred_element_type=F32)
    o_ref[...] = (jax.nn.silu(vg) * vu).astype(o_ref.dtype)


def _swiglu_up(a, w_gu, tm=512, tn=512):
    m, k = a.shape
    f = w_gu.shape[1] // 2
    nb = f // tn
    return pl.pallas_call(
        _swiglu_up_kernel,
        out_shape=jax.ShapeDtypeStruct((m, f), BF16),
        grid=(nb, m // tm),
        in_specs=[pl.BlockSpec((tm, k), lambda j, i: (i, 0)),
                  pl.BlockSpec((k, tn), lambda j, i: (0, j)),
                  pl.BlockSpec((k, tn), lambda j, i: (0, j + nb))],
        out_specs=pl.BlockSpec((tm, tn), lambda j, i: (i, j)),
        scratch_shapes=[pltpu.VMEM((k, tn), BF16), pltpu.VMEM((k, tn), BF16)],
        compiler_params=_params("arbitrary", "arbitrary"),
        name="swiglu_up",
    )(a, w_gu, w_gu)


def _mm_res_kernel(a_ref, w_ref, r_ref, o_ref, w_s):
    _cache_weights(pl.program_id(1) == 0, ((w_ref, w_s),))
    o_ref[...] = r_ref[...] + jnp.dot(a_ref[...], w_s[...], preferred_element_type=F32)


def _matmul_residual(a, w, res, tm=512, tn=512):
    m, k = a.shape
    n = w.shape[1]
    return pl.pallas_call(
        _mm_res_kernel,
        out_shape=jax.ShapeDtypeStruct((m, n), F32),
        grid=(n // tn, m // tm),
        in_specs=[pl.BlockSpec((tm, k), lambda j, i: (i, 0)),
                  pl.BlockSpec((k, tn), lambda j, i: (0, j)),
                  pl.BlockSpec((tm, tn), lambda j, i: (i, j))],
        out_specs=pl.BlockSpec((tm, tn), lambda j, i: (i, j)),
        scratch_shapes=[pltpu.VMEM((k, tn), BF16)],
        compiler_params=_params("arbitrary", "arbitrary"),
        name="matmul_residual",
    )(a, w, res)


def _rope_kernel(pos_ref, inv_ref, cos_ref, sa_ref, sb_ref):
    ang = pos_ref[...].astype(F32) * inv_ref[...]
    c = jnp.cos(ang)
    s = jnp.sin(ang)
    lane = lax.broadcasted_iota(jnp.int32, ang.shape, 1)
    first_half = lane < ROPE_DIM // 2
    cos_ref[...] = c
    sa_ref[...] = jnp.where(first_half, -s, 0.0)
    sb_ref[...] = jnp.where(first_half, 0.0, s)


def _rope_tables(pos, tm=512):
    n = pos.shape[0]
    tm = min(tm, n)
    half = ROPE_DIM // 2
    inv = jnp.power(ROPE_THETA, -jnp.arange(half, dtype=F32) / half)
    inv = jnp.concatenate([inv, inv, jnp.zeros((LANES - ROPE_DIM,), F32)]).reshape(1, LANES)
    spec = pl.BlockSpec((tm, LANES), lambda i: (i, 0))
    return pl.pallas_call(
        _rope_kernel,
        out_shape=[jax.ShapeDtypeStruct((n, LANES), F32)] * 3,
        grid=(n // tm,),
        in_specs=[pl.BlockSpec((tm, 1), lambda i: (i, 0)),
                  pl.BlockSpec((1, LANES), lambda i: (0, 0))],
        out_specs=[spec, spec, spec],
        compiler_params=_params("arbitrary"),
        name="rope_tables",
    )(pos.reshape(n, 1), inv)


def _rope(y, c, sa, sb):
    half = ROPE_DIM // 2
    return y * c + pltpu.roll(y, LANES - half, 1) * sa + pltpu.roll(y, half, 1) * sb


def _head_norm(x, gain):
    return x * lax.rsqrt(jnp.mean(x * x, axis=-1, keepdims=True) + RMS_EPS) * gain


def _nsa_proj_kernel(a_ref, w_ref, gain_ref, cos_ref, sa_ref, sb_ref, o_ref, w_s, *, norm_tiles):
    j = pl.program_id(0)
    _cache_weights(pl.program_id(1) == 0, ((w_ref, w_s),))
    acc = jnp.dot(a_ref[...], w_s[...], preferred_element_type=F32)
    is_norm = functools.reduce(jnp.logical_or, [j == t for t in norm_tiles])

    @pl.when(is_norm)
    def _():
        c, sa, sb = cos_ref[...], sa_ref[...], sb_ref[...]
        gain = gain_ref[0]
        for hh in range(acc.shape[1] // HEAD_DIM):
            sl = slice(HEAD_DIM * hh, HEAD_DIM * (hh + 1))
            o_ref[:, sl] = _rope(_head_norm(acc[:, sl], gain), c, sa, sb).astype(o_ref.dtype)

    @pl.when(jnp.logical_not(is_norm))
    def _():
        o_ref[...] = acc.astype(o_ref.dtype)


def _nsa_proj(u, w_in, q_gain, k_gain, rope, tm=512, tn=512):
    m, k = u.shape
    q_dim = GQA_GROUP * N_KV_HEADS * HEAD_DIM
    kv_dim = N_KV_HEADS * HEAD_DIM
    assert kv_dim == tn
    n_q = q_dim // tn
    n_tiles = n_q + 6
    ones = jnp.ones((HEAD_DIM,), F32)
    q_scaled = q_gain.astype(F32) * (HEAD_DIM ** -0.5 * math.log2(math.e))
    gains = jnp.stack([q_scaled] * n_q + [ones, ones, k_gain[1], ones, k_gain[2], ones]).reshape(n_tiles, 1, HEAD_DIM)
    norm_tiles = tuple(range(n_q)) + (n_q + 2, n_q + 4)
    tab = pl.BlockSpec((tm, LANES), lambda j, i: (i, 0))
    return pl.pallas_call(
        functools.partial(_nsa_proj_kernel, norm_tiles=norm_tiles),
        out_shape=jax.ShapeDtypeStruct((m, n_tiles * tn), BF16),
        grid=(n_tiles, m // tm),
        in_specs=[pl.BlockSpec((tm, k), lambda j, i: (i, 0)),
                  pl.BlockSpec((k, tn), lambda j, i: (0, j)),
                  pl.BlockSpec((1, 1, HEAD_DIM), lambda j, i: (j, 0, 0)),
                  tab, tab, tab],
        out_specs=pl.BlockSpec((tm, tn), lambda j, i: (i, j)),
        scratch_shapes=[pltpu.VMEM((k, tn), BF16)],
        compiler_params=_params("arbitrary", "arbitrary"),
        name="nsa_proj",
    )(u, w_in, gains, *rope)


def _gate_kernel(a_ref, w_ref, o_ref):
    o_ref[...] = jax.nn.sigmoid(jnp.dot(a_ref[...], w_ref[...].astype(BF16), preferred_element_type=F32))


def _nsa_gates(u, w_gate, tm=512):
    m, k = u.shape
    n = w_gate.shape[1]
    w_pad = jnp.pad(w_gate, ((0, 0), (0, LANES - n)))
    return pl.pallas_call(
        _gate_kernel,
        out_shape=jax.ShapeDtypeStruct((m, LANES), F32),
        grid=(m // tm,),
        in_specs=[pl.BlockSpec((tm, k), lambda i: (i, 0)),
                  pl.BlockSpec((k, LANES), lambda i: (0, 0))],
        out_specs=pl.BlockSpec((tm, LANES), lambda i: (i, 0)),
        compiler_params=_params("arbitrary"),
        name="nsa_gates",
    )(u, w_pad)


def _compress_kernel(*refs, is_key):
    if is_key:
        ca_ref, cb_ref, pe_ref, w1_ref, w2_ref, gain_ref, cos_ref, sa_ref, sb_ref, o_ref = refs
    else:
        ca_ref, cb_ref, pe_ref, w1_ref, w2_ref, o_ref = refs
    half = w1_ref.shape[0] // 2
    pe = pe_ref[...]
    xa = (ca_ref[0].astype(F32) + pe[:, :half]).astype(BF16)
    xb = (cb_ref[0].astype(F32) + pe[:, half:]).astype(BF16)
    hid = (jnp.dot(xa, w1_ref[:half, :].astype(BF16), preferred_element_type=F32)
           + jnp.dot(xb, w1_ref[half:, :].astype(BF16), preferred_element_type=F32))
    out = jnp.dot(jax.nn.gelu(hid).astype(BF16), w2_ref[...].astype(BF16), preferred_element_type=F32)
    if is_key:
        out = _rope(_head_norm(out, gain_ref[...]), cos_ref[...], sa_ref[...], sb_ref[...])
    o_ref[0] = out.astype(o_ref.dtype)


def _compress(t, pe, w1, w2, key_extras=None):
    seq = t.shape[0]
    nc = seq // CMP_STRIDE
    width = CMP_STRIDE * HEAD_DIM
    ca = t.reshape(nc, CMP_STRIDE, N_KV_HEADS, HEAD_DIM).transpose(2, 0, 1, 3).reshape(N_KV_HEADS, nc, width)
    cb = jnp.concatenate([ca[:, 1:], jnp.zeros((N_KV_HEADS, 1, width), ca.dtype)], axis=1)
    blk = pl.BlockSpec((1, nc, width), lambda h: (h, 0, 0))
    full = lambda a: pl.BlockSpec(a.shape, lambda h: (0,) * a.ndim)
    args = [ca, cb, pe.reshape(1, CMP_BLOCK * HEAD_DIM), w1, w2]
    if key_extras is not None:
        args += list(key_extras)
    return pl.pallas_call(
        functools.partial(_compress_kernel, is_key=key_extras is not None),
        out_shape=jax.ShapeDtypeStruct((N_KV_HEADS, nc, HEAD_DIM), BF16),
        grid=(N_KV_HEADS,),
        in_specs=[blk, blk] + [full(a) for a in args[2:]],
        out_specs=pl.BlockSpec((1, nc, HEAD_DIM), lambda h: (h, 0, 0)),
        compiler_params=_params("arbitrary"),
        name="nsa_compress_k" if key_extras is not None else "nsa_compress_v",
    )(*args)


def _dot_nt(a, b):
    return lax.dot_general(a, b, (((1,), (1,)), ((), ())), preferred_element_type=F32)


def _split3(x):
    hi = x.astype(BF16)
    r1 = x - hi.astype(F32)
    mid = r1.astype(BF16)
    lo = (r1 - mid.astype(F32)).astype(BF16)
    return hi, mid, lo


def _nsa_attn_kernel(q_ref, kc_ref, vct_ref, ks_ref, vst_ref, kw_ref, vwt_ref, gate_ref, blk_ref, o_ref,
                     acc_ref, mix_ref, *, seq, tk):
    t0 = pl.program_id(1) * Q_BLOCK
    nc = kc_ref.shape[1]
    ns = seq // SEL_BLOCK
    grp = GQA_GROUP
    cols = grp * Q_BLOCK
    sel_shift = int(math.log2(SEL_BLOCK))
    q = q_ref[...].astype(F32)
    qt = jnp.concatenate([q[:, HEAD_DIM * g:HEAD_DIM * (g + 1)].T for g in range(grp)], axis=1).astype(BF16)
    t_row = t0 + lax.broadcasted_iota(jnp.int32, (1, Q_BLOCK), 1)

    def heads(x):
        return jnp.concatenate([x] * grp, axis=1)

    def softmax_cols(s, ok):
        sb = s + heads(jnp.where(ok, 0.0, NEG))
        m = jnp.max(sb, axis=0, keepdims=True)
        e = jnp.exp2(sb - m)
        return e, m, jnp.maximum(jnp.sum(e, axis=0, keepdims=True), 1e-30)

    n_idx = lax.broadcasted_iota(jnp.int32, (nc, Q_BLOCK), 0)
    ok_c = (n_idx * CMP_STRIDE + (CMP_BLOCK - 1) <= t_row) & (n_idx < nc - 1)
    e_c, m_c, den_c = softmax_cols(jnp.dot(kc_ref[0], qt, preferred_element_type=F32), ok_c)
    p_c = e_c * jnp.where(m_c > 0.5 * NEG, 1.0 / den_c, 0.0)
    o_c = jnp.dot(vct_ref[0], p_c.astype(BF16), preferred_element_type=F32)

    span = WINDOW + Q_BLOCK
    w0 = pl.multiple_of(jnp.maximum(t0 - WINDOW, 0), Q_BLOCK)
    rel = t_row - (w0 + lax.broadcasted_iota(jnp.int32, (span, Q_BLOCK), 0))
    e_w, _, den_w = softmax_cols(jnp.dot(kw_ref[pl.ds(w0, span), :], qt, preferred_element_type=F32),
                                 (rel >= 0) & (rel < WINDOW))
    o_w = jnp.dot(vwt_ref[0, :, pl.ds(w0, span)], (e_w * (1.0 / den_w)).astype(BF16),
                  preferred_element_type=F32)
    gate = gate_ref[0, 0]
    mix_ref[...] = gate[0:1] * o_c + gate[2:3] * o_w

    imp = p_c[:, :Q_BLOCK]
    for g in range(1, grp):
        imp = imp + p_c[:, Q_BLOCK * g:Q_BLOCK * (g + 1)]
    ratio = SEL_BLOCK // CMP_STRIDE
    d = (lax.broadcasted_iota(jnp.int32, (ns, nc), 1)
         - ratio * lax.broadcasted_iota(jnp.int32, (ns, nc), 0))
    overlap = jnp.zeros((ns, nc), F32)
    for n in range(CMP_BLOCK // CMP_STRIDE):
        overlap = overlap + jnp.where((d - n >= 0) & (d - n < ratio), 1.0, 0.0)
    overlap = overlap.astype(BF16)
    p_slc = sum(jnp.dot(overlap, part, preferred_element_type=F32) for part in _split3(imp))

    j_idx = lax.broadcasted_iota(jnp.int32, (ns, Q_BLOCK), 0)
    j_f = j_idx.astype(F32)
    dist = jnp.right_shift(t_row, sel_shift) - j_idx
    forced = (j_idx == 0) | ((dist >= 0) & (dist < SEL_LOCAL))
    score = jnp.where(forced, jnp.inf, jnp.where(dist >= 0, p_slc, -jnp.inf))
    sel = jnp.zeros((ns, Q_BLOCK), F32)
    for _ in range(min(SEL_TOPK, ns)):
        top = jnp.max(score, axis=0, keepdims=True)
        idx = jnp.min(jnp.where(score == top, j_f, float(ns)), axis=0, keepdims=True)
        pick = j_f == idx
        sel = jnp.where(pick, 1.0, sel)
        score = jnp.where(pick, -jnp.inf, score)

    q_aug = jnp.concatenate([qt, heads(jnp.where(sel > 0.0, 0.0, NEG).astype(BF16))], axis=0)
    acc_ref[...] = jnp.zeros_like(acc_ref)

    def sel_step(kt, carry, diagonal):
        m_i, l_i = carry
        k0 = pl.multiple_of(kt * tk, tk)
        k_aug = jnp.concatenate([ks_ref[pl.ds(k0, tk), :], blk_ref[pl.ds(k0, tk), :]], axis=1)
        sb = jnp.dot(k_aug, q_aug, preferred_element_type=F32)
        if diagonal:
            kpos = k0 + lax.broadcasted_iota(jnp.int32, (tk, Q_BLOCK), 0)
            sb = sb + heads(jnp.where(kpos <= t_row, 0.0, NEG))
        m_new = jnp.maximum(m_i, jnp.max(sb, axis=0, keepdims=True))
        e = jnp.exp2(sb - m_new)
        alpha = jnp.exp2(m_i - m_new)
        l_new = alpha * l_i + jnp.sum(e, axis=0, keepdims=True)
        acc_ref[...] = alpha * acc_ref[...] + jnp.dot(vst_ref[0, :, pl.ds(k0, tk)], e.astype(BF16),
                                                      preferred_element_type=F32)
        return m_new, l_new

    init = (jnp.full((1, cols), NEG, F32), jnp.zeros((1, cols), F32))
    last = t0 // tk
    carry = lax.fori_loop(0, last, functools.partial(sel_step, diagonal=False), init)
    _, l_s = sel_step(last, carry, True)
    o_s = acc_ref[...] * (1.0 / jnp.maximum(l_s, 1e-30))

    mixed = mix_ref[...] + gate_ref[0, 0, 1:2] * o_s
    for g in range(grp):
        o_ref[:, HEAD_DIM * g:HEAD_DIM * (g + 1)] = mixed[:, Q_BLOCK * g:Q_BLOCK * (g + 1)].T.astype(o_ref.dtype)


def _nsa_attention(proj, kcmp, vcmp_t, vsl_t, vw_t, gates, tk=512):
    seq = proj.shape[0]
    tk = min(tk, seq)
    q_dim = GQA_GROUP * N_KV_HEADS * HEAD_DIM
    kv_blocks = N_KV_HEADS
    first = q_dim // HEAD_DIM + 2 * kv_blocks
    nc = kcmp.shape[1]
    cols = GQA_GROUP * Q_BLOCK

    def k_spec(which):
        return pl.BlockSpec((seq, HEAD_DIM), lambda h, qb: (0, first + which * kv_blocks + h))

    def vt_spec(n):
        return pl.BlockSpec((1, HEAD_DIM, n), lambda h, qb: (h, 0, 0))

    q_spec = pl.BlockSpec((Q_BLOCK, GQA_GROUP * HEAD_DIM), lambda h, qb: (qb, h))
    ns = seq // SEL_BLOCK
    key_block = (jnp.arange(seq, dtype=jnp.int32)[:, None] // SEL_BLOCK
                 == jnp.arange(ns, dtype=jnp.int32)[None, :]).astype(BF16)
    return pl.pallas_call(
        functools.partial(_nsa_attn_kernel, seq=seq, tk=tk),
        out_shape=jax.ShapeDtypeStruct((seq, q_dim), BF16),
        grid=(N_KV_HEADS, seq // Q_BLOCK),
        in_specs=[q_spec, pl.BlockSpec((1, nc, HEAD_DIM), lambda h, qb: (h, 0, 0)), vt_spec(nc),
                  k_spec(0), vt_spec(seq), k_spec(2), vt_spec(seq),
                  pl.BlockSpec((1, 1, 3, cols), lambda h, qb: (h, qb, 0, 0)),
                  pl.BlockSpec((seq, ns), lambda h, qb: (0, 0))],
        out_specs=q_spec,
        scratch_shapes=[pltpu.VMEM((HEAD_DIM, cols), F32), pltpu.VMEM((HEAD_DIM, cols), F32)],
        compiler_params=_params("arbitrary", "arbitrary"),
        name="nsa_attention",
    )(proj, kcmp, vcmp_t, proj, vsl_t, proj, vw_t, gates, key_block)


def _nsa_mixer(u, positions, w_in, q_gain, k_gain, pe_k, pe_v, ck_w1, ck_w2, cv_w1, cv_w2):
    seq = u.shape[0]
    q_dim = GQA_GROUP * N_KV_HEADS * HEAD_DIM
    kv_dim = N_KV_HEADS * HEAD_DIM
    n_main = q_dim + 6 * kv_dim
    nc = seq // CMP_STRIDE
    rope = _rope_tables(positions)
    proj = _nsa_proj(u, w_in, q_gain, k_gain, rope)
    gate = _nsa_gates(u, w_in[:, n_main:])
    gates = (gate[:, :3 * N_KV_HEADS * GQA_GROUP].reshape(seq // Q_BLOCK, Q_BLOCK, 3, N_KV_HEADS, GQA_GROUP)
             .transpose(3, 0, 2, 4, 1).reshape(N_KV_HEADS, seq // Q_BLOCK, 3, GQA_GROUP * Q_BLOCK))

    def keys_last(cols):
        return cols.reshape(seq, N_KV_HEADS, HEAD_DIM).transpose(1, 2, 0)

    pos_cmp = jnp.concatenate([positions[CMP_BLOCK - 1::CMP_STRIDE][:nc - 1], jnp.zeros((1,), positions.dtype)])
    rope_cmp = _rope_tables(pos_cmp)
    kcmp = _compress(proj[:, q_dim:q_dim + kv_dim], pe_k, ck_w1, ck_w2,
                     key_extras=(k_gain[0].reshape(1, HEAD_DIM),) + tuple(rope_cmp))
    vcmp = _compress(proj[:, q_dim + kv_dim:q_dim + 2 * kv_dim], pe_v, cv_w1, cv_w2)
    vsl_t = keys_last(proj[:, q_dim + 3 * kv_dim:q_dim + 4 * kv_dim])
    vw_t = keys_last(proj[:, q_dim + 5 * kv_dim:q_dim + 6 * kv_dim])
    return _nsa_attention(proj, kcmp, vcmp.transpose(0, 2, 1), vsl_t, vw_t, gates)


def _router_kernel(x_ref, g_ref, w_ref, b_ref, u_ref, r_ref):
    x = x_ref[...]
    u = x * lax.rsqrt(jnp.mean(x * x, axis=-1, keepdims=True) + RMS_EPS) * g_ref[...]
    u_ref[...] = u
    uh, um, _ = _split3(u)
    wh, wm, _ = _split3(w_ref[...])
    logits = (jnp.dot(uh, wh, preferred_element_type=F32) + jnp.dot(uh, wm, preferred_element_type=F32)
              + jnp.dot(um, wh, preferred_element_type=F32)) + b_ref[...]
    lane = lax.broadcasted_iota(jnp.int32, logits.shape, 1).astype(F32)
    lg = jnp.where(lane < N_EXPERTS, logits, -jnp.inf)
    v1 = jnp.max(lg, axis=-1, keepdims=True)
    i1 = jnp.min(jnp.where(lg == v1, lane, float(LANES)), axis=-1, keepdims=True)
    lg = jnp.where(lane == i1, -jnp.inf, lg)
    v2 = jnp.max(lg, axis=-1, keepdims=True)
    i2 = jnp.min(jnp.where(lg == v2, lane, float(LANES)), axis=-1, keepdims=True)
    e2 = jnp.exp(v2 - v1)
    den = 1.0 + e2
    r_ref[...] = jnp.where(lane == 0, i1, jnp.where(lane == 1, i2, jnp.where(
        lane == 2, 1.0 / den, jnp.where(lane == 3, e2 / den, 0.0))))


def _router(h, gain, w_router, b_router, tm=256):
    m, d = h.shape
    w_pad = jnp.pad(w_router.astype(F32), ((0, 0), (0, LANES - N_EXPERTS)))
    b_pad = jnp.pad(b_router.astype(F32), (0, LANES - N_EXPERTS)).reshape(1, LANES)
    return pl.pallas_call(
        _router_kernel,
        out_shape=[jax.ShapeDtypeStruct((m, d), F32), jax.ShapeDtypeStruct((m, LANES), F32)],
        grid=(m // tm,),
        in_specs=[pl.BlockSpec((tm, d), lambda i: (i, 0)),
                  pl.BlockSpec((1, d), lambda i: (0, 0)),
                  pl.BlockSpec((d, LANES), lambda i: (0, 0)),
                  pl.BlockSpec((1, LANES), lambda i: (0, 0))],
        out_specs=[pl.BlockSpec((tm, d), lambda i: (i, 0)), pl.BlockSpec((tm, LANES), lambda i: (i, 0))],
        compiler_params=_params("arbitrary"),
        name="moe_router",
    )(h, gain.reshape(1, d), w_pad, b_pad)


def _row_copy(src_hbm, row, dst, r, sem):
    return pltpu.make_async_copy(src_hbm.at[pl.ds(row, 1), :], dst.at[pl.ds(r, 1), :], sem)


def _gather_kernel(idx_ref, used_ref, src_hbm, o_ref, buf, sem):
    rows = o_ref.shape[0]
    i = pl.program_id(0)
    n_used = used_ref[0]

    def issue(blk):
        slot = blk % 2

        def start(r, _):
            _row_copy(src_hbm, idx_ref[blk * rows + r], buf.at[slot], r, sem.at[slot]).start()
            return 0

        lax.fori_loop(0, rows, start, 0, unroll=DMA_UNROLL)

    @pl.when(i == 0)
    def _():
        issue(i)

    @pl.when(i + 1 < n_used)
    def _():
        issue(i + 1)

    @pl.when(i < n_used)
    def _():
        slot = i % 2

        def wait(r, _):
            _row_copy(src_hbm, 0, buf.at[slot], r, sem.at[slot]).wait()
            return 0

        lax.fori_loop(0, rows, wait, 0, unroll=DMA_UNROLL)
        o_ref[...] = buf[slot].astype(o_ref.dtype)

    @pl.when(i >= n_used)
    def _():
        o_ref[...] = jnp.zeros_like(o_ref)


def _gather_rows(src, idx, n_used, out_dtype, rows=MOE_ROWS):
    n = idx.shape[0]
    d = src.shape[1]
    return pl.pallas_call(
        _gather_kernel,
        out_shape=jax.ShapeDtypeStruct((n, d), out_dtype),
        grid_spec=pltpu.PrefetchScalarGridSpec(
            num_scalar_prefetch=2,
            grid=(n // rows,),
            in_specs=[pl.BlockSpec(memory_space=pl.ANY)],
            out_specs=pl.BlockSpec((rows, d), lambda i, idx, used: (i, 0)),
            scratch_shapes=[pltpu.VMEM((2, rows, d), src.dtype), pltpu.SemaphoreType.DMA((2,))]),
        compiler_params=_params("arbitrary"),
        name="moe_gather",
    )(idx, n_used, src)


def _block_state(be_ref, used_ref, i):
    changed = (i == 0) | (be_ref[i] != be_ref[jnp.maximum(i - 1, 0)])
    used = i < used_ref[0]
    return used, used & changed


def _last_used(i, used):
    return jnp.minimum(i, used[0] - 1)


def _moe_up_kernel(be_ref, used_ref, x_ref, wg_ref, wu_ref, o_ref, wg_s, wu_s):
    used, first = _block_state(be_ref, used_ref, pl.program_id(1))

    @pl.when(first)
    def _():
        wg_s[...] = wg_ref[0].astype(BF16)
        wu_s[...] = wu_ref[0].astype(BF16)

    @pl.when(used)
    def _():
        a = x_ref[...]
        vg = jnp.dot(a, wg_s[...], preferred_element_type=F32)
        vu = jnp.dot(a, wu_s[...], preferred_element_type=F32)
        o_ref[...] = (jax.nn.silu(vg) * vu).astype(o_ref.dtype)

    @pl.when(jnp.logical_not(used))
    def _():
        o_ref[...] = jnp.zeros_like(o_ref)


def _moe_up(x_rows, blk_e, n_used, w_gu, tn=512, rows=MOE_ROWS):
    n, k = x_rows.shape
    f = w_gu.shape[2] // 2
    nb = f // tn
    return pl.pallas_call(
        _moe_up_kernel,
        out_shape=jax.ShapeDtypeStruct((n, f), BF16),
        grid_spec=pltpu.PrefetchScalarGridSpec(
            num_scalar_prefetch=2,
            grid=(nb, n // rows),
            in_specs=[pl.BlockSpec((rows, k), lambda j, i, be, nu: (_last_used(i, nu), 0)),
                      pl.BlockSpec((1, k, tn), lambda j, i, be, nu: (be[_last_used(i, nu)], 0, j)),
                      pl.BlockSpec((1, k, tn), lambda j, i, be, nu: (be[_last_used(i, nu)], 0, j + nb))],
            out_specs=pl.BlockSpec((rows, tn), lambda j, i, be, nu: (i, j)),
            scratch_shapes=[pltpu.VMEM((k, tn), BF16), pltpu.VMEM((k, tn), BF16)]),
        compiler_params=_params("arbitrary", "arbitrary"),
        name="moe_up",
    )(blk_e, n_used, x_rows, w_gu, w_gu)


def _moe_down_kernel(be_ref, used_ref, a_ref, w_ref, o_ref, w_s):
    used, first = _block_state(be_ref, used_ref, pl.program_id(1))

    @pl.when(first)
    def _():
        w_s[...] = w_ref[0].astype(BF16)

    @pl.when(used)
    def _():
        o_ref[...] = jnp.dot(a_ref[...], w_s[...], preferred_element_type=F32)

    @pl.when(jnp.logical_not(used))
    def _():
        o_ref[...] = jnp.zeros_like(o_ref)


def _moe_down(act, blk_e, n_used, w_down, tn=512, rows=MOE_ROWS):
    n, k = act.shape
    d = w_down.shape[2]
    return pl.pallas_call(
        _moe_down_kernel,
        out_shape=jax.ShapeDtypeStruct((n, d), F32),
        grid_spec=pltpu.PrefetchScalarGridSpec(
            num_scalar_prefetch=2,
            grid=(d // tn, n // rows),
            in_specs=[pl.BlockSpec((rows, k), lambda j, i, be, nu: (_last_used(i, nu), 0)),
                      pl.BlockSpec((1, k, tn), lambda j, i, be, nu: (be[_last_used(i, nu)], 0, j))],
            out_specs=pl.BlockSpec((rows, tn), lambda j, i, be, nu: (i, j)),
            scratch_shapes=[pltpu.VMEM((k, tn), BF16)]),
        compiler_params=_params("arbitrary", "arbitrary"),
        name="moe_down",
    )(blk_e, n_used, act, w_down)


def _combine_kernel(dest_ref, h_ref, r_ref, rows_hbm, o_ref, buf, sem):
    tm = h_ref.shape[0]
    base = pl.program_id(0) * tm

    def start(r, _):
        for k in range(2):
            _row_copy(rows_hbm, dest_ref[2 * (base + r) + k], buf.at[k], r, sem.at[k]).start()
        return 0

    def wait(r, _):
        for k in range(2):
            _row_copy(rows_hbm, 0, buf.at[k], r, sem.at[k]).wait()
        return 0

    lax.fori_loop(0, tm, start, 0, unroll=DMA_UNROLL)
    lax.fori_loop(0, tm, wait, 0, unroll=DMA_UNROLL)
    w = r_ref[...]
    o_ref[...] = h_ref[...] + (w[:, 2:3] * buf[0] + w[:, 3:4] * buf[1])


def _moe_combine(h, route, out_rows, dest, tm=256):
    m, d = h.shape
    return pl.pallas_call(
        _combine_kernel,
        out_shape=jax.ShapeDtypeStruct((m, d), F32),
        grid_spec=pltpu.PrefetchScalarGridSpec(
            num_scalar_prefetch=1,
            grid=(m // tm,),
            in_specs=[pl.BlockSpec((tm, d), lambda i, dest: (i, 0)),
                      pl.BlockSpec((tm, LANES), lambda i, dest: (i, 0)),
                      pl.BlockSpec(memory_space=pl.ANY)],
            out_specs=pl.BlockSpec((tm, d), lambda i, dest: (i, 0)),
            scratch_shapes=[pltpu.VMEM((2, tm, d), F32), pltpu.SemaphoreType.DMA((2,))]),
        compiler_params=_params("arbitrary"),
        name="moe_combine",
    )(dest.reshape(-1), h, route, out_rows)


def _moe_layout(top_e, rows=MOE_ROWS):
    n_tok = top_e.shape[0]
    e_flat = top_e.reshape(-1)
    onehot = (e_flat[:, None] == jnp.arange(N_EXPERTS, dtype=jnp.int32)[None, :]).astype(jnp.int32)
    csum = jnp.cumsum(onehot, axis=0)
    rank = jnp.take_along_axis(csum, e_flat[:, None], axis=1)[:, 0] - 1
    counts = csum[-1]
    padded = (counts + rows - 1) // rows * rows
    pad_end = jnp.cumsum(padded)
    dest = (pad_end - padded)[e_flat] + rank
    n_rows = e_flat.shape[0] + N_EXPERTS * rows
    t_flat = jnp.repeat(jnp.arange(n_tok, dtype=jnp.int32), top_e.shape[1])
    row_tok = jnp.zeros((n_rows,), jnp.int32).at[dest].set(t_flat)
    n_blk = n_rows // rows
    blk_start = jnp.arange(n_blk, dtype=jnp.int32) * rows
    blk_e = jnp.minimum(jnp.sum(blk_start[:, None] >= pad_end[None, :], axis=1), N_EXPERTS - 1).astype(jnp.int32)
    n_used = (pad_end[-1:] // rows).astype(jnp.int32)
    return row_tok, blk_e, n_used, dest.astype(jnp.int32).reshape(n_tok, -1)


def _moe_ffn_residual(h, gain, w_router, b_router, w_gu, w_down):
    u, route = _router(h, gain, w_router, b_router)
    top_e = route[:, :2].astype(jnp.int32)
    row_tok, blk_e, n_used, dest = _moe_layout(top_e)
    x_rows = _gather_rows(u, row_tok, n_used, BF16)
    act = _moe_up(x_rows, blk_e, n_used, w_gu)
    out_rows = _moe_down(act, blk_e, n_used, w_down)
    return _moe_combine(h, route, out_rows, dest)


def kernel(x, positions, norm_mix, norm_ffn, s5_a_re, s5_a_im, s5_log_step, s5_b_re, s5_b_im, s5_c_re, s5_c_im, s5_d, s5_w_glu, nsa_w_in, nsa_q_gain, nsa_k_gain, nsa_pe_k, nsa_pe_v, nsa_ck_w1, nsa_ck_w2, nsa_cv_w1, nsa_cv_w2, nsa_w_out, ffn_w_gu, ffn_w_down, moe_w_router, moe_b_router, moe_w_gu, moe_w_down):
    bsz, seq, d = x.shape
    assert bsz == 1, "the scan and attention kernels take one sequence"
    h = x.reshape(seq, d)
    h = _layer_s5(h, norm_mix[0], norm_ffn[0], s5_a_re[0], s5_a_im[0], s5_log_step[0], s5_b_re[0],
                  s5_b_im[0], s5_c_re[0], s5_c_im[0], s5_d[0], s5_w_glu[0], ffn_w_gu[0], ffn_w_down[0])
    h = _layer_nsa(h, positions[0], norm_mix[1], norm_ffn[1], nsa_w_in[0], nsa_q_gain[0], nsa_k_gain[0],
                   nsa_pe_k[0], nsa_pe_v[0], nsa_ck_w1[0], nsa_ck_w2[0], nsa_cv_w1[0], nsa_cv_w2[0],
                   nsa_w_out[0], moe_w_router[0], moe_b_router[0], moe_w_gu[0], moe_w_down[0])
    return h.reshape(bsz, seq, d)


def _layer_nsa(h, positions, g_mix, g_ffn, w_in, q_gain, k_gain, pe_k, pe_v, ck_w1, ck_w2, cv_w1, cv_w2,
               w_out, w_router, b_router, w_gu, w_down):
    u = _rms_norm(h, g_mix, BF16)
    o = _nsa_mixer(u, positions, w_in, q_gain, k_gain, pe_k, pe_v, ck_w1, ck_w2, cv_w1, cv_w2)
    h = _matmul_residual(o, w_out, h)
    return _moe_ffn_residual(h, g_ffn, w_router, b_router, w_gu, w_down)


def _layer_s5(h, g_mix, g_ffn, a_re, a_im, log_step, b_re, b_im, c_re, c_im, d_skip, w_glu, w_gu, w_down):
    u = _rms_norm(h, g_mix, F32)
    g = _s5_mixer(u, a_re, a_im, log_step, b_re, b_im, c_re, c_im, d_skip)
    h = _glu_residual(g, w_glu, h)
    u = _rms_norm(h, g_ffn, BF16)
    act = _swiglu_up(u, w_gu)
    return _matmul_residual(act, w_down, h)
```

```python
import functools
import math

import jax
import jax.numpy as jnp
from jax import lax
from jax.experimental import pallas as pl
from jax.experimental.pallas import tpu as pltpu

F32 = jnp.float32
BF16 = jnp.bfloat16

RMS_EPS = 1e-6
S5_GROUP = 16
S5_STATE = 64
HEAD_DIM = 128
N_KV_HEADS = 4
GQA_GROUP = 4
ROPE_DIM = 32
ROPE_THETA = 500000.0
CMP_BLOCK = 32
CMP_STRIDE = 16
SEL_BLOCK = 64
SEL_TOPK = 16
SEL_LOCAL = 2
WINDOW = 512
Q_BLOCK = 128
N_EXPERTS = 8
NEG = -1e30

LANES = 128
SUBLANES = 8
VMEM_LIMIT = 56 * 1024 * 1024

S5_SLAB = 256
S5_SLAB_STATES = S5_SLAB // S5_GROUP * S5_STATE
S5_SUB = 64
MOE_ROWS = 512
DMA_UNROLL = 8


def _params(*sem):
    return pltpu.CompilerParams(dimension_semantics=sem, vmem_limit_bytes=VMEM_LIMIT)


def _rms_kernel(x_ref, g_ref, o_ref):
    x = x_ref[...]
    ms = jnp.mean(x * x, axis=-1, keepdims=True)
    o_ref[...] = (x * lax.rsqrt(ms + RMS_EPS) * g_ref[...]).astype(o_ref.dtype)


def _rms_norm(x, gain, out_dtype, tm=512):
    m, d = x.shape
    return pl.pallas_call(
        _rms_kernel,
        out_shape=jax.ShapeDtypeStruct((m, d), out_dtype),
        grid=(m // tm,),
        in_specs=[pl.BlockSpec((tm, d), lambda i: (i, 0)),
                  pl.BlockSpec((1, d), lambda i: (0, 0))],
        out_specs=pl.BlockSpec((tm, d), lambda i: (i, 0)),
        compiler_params=_params("arbitrary"),
        name="rms_norm",
    )(x, gain.reshape(1, d))


def _s5_kernel(u_ref, b_ref, c_ref, lam_ref, ptab_ref, apow_ref, d_ref, o_ref,
               xs_ref, carry_ref, up_ref, us_ref, *, sub):
    n = S5_SLAB_STATES

    @pl.when(pl.program_id(1) == 0)
    def _():
        carry_ref[...] = jnp.zeros_like(carry_ref)

    halves = S5_SLAB // LANES
    for c in range(halves):
        us_ref[c] = u_ref[:, LANES * c:LANES * (c + 1)]
    for i in range(sub):
        for c in range(halves):
            up_ref[SUBLANES * i:SUBLANES * (i + 1), LANES * c:LANES * (c + 1)] = (
                us_ref[c, pl.ds(i, SUBLANES, stride=sub), :])
    up = up_ref[...]
    xs_ref[...] = jnp.dot(up.astype(BF16), b_ref[0], preferred_element_type=F32)

    lam = lam_ref[0]
    lr, li = lam[:, :n], lam[:, n:]

    def local_scan(i, h):
        hr, hi = h
        off = pl.multiple_of(i * SUBLANES, SUBLANES)
        x = xs_ref[pl.ds(off, SUBLANES), :]
        nr = lr * hr - li * hi + x[:, :n]
        ni = lr * hi + li * hr + x[:, n:]
        xs_ref[pl.ds(off, SUBLANES), :] = jnp.concatenate([nr, ni], axis=1)
        return nr, ni

    zero = jnp.zeros((SUBLANES, n), F32)
    er, ei = lax.fori_loop(0, sub, local_scan, (zero, zero))

    row = lax.broadcasted_iota(jnp.int32, (SUBLANES, n), 0)
    cin = carry_ref[...]
    zr = jnp.where(row == 0, cin[:, :n], pltpu.roll(er, 1, 0))
    zi = jnp.where(row == 0, cin[:, n:], pltpu.roll(ei, 1, 0))
    apow = apow_ref[0]
    for s, d in enumerate((1, 2, 4)):
        ar = apow[SUBLANES * s:SUBLANES * (s + 1), :n]
        ai = apow[SUBLANES * s:SUBLANES * (s + 1), n:]
        sr = pltpu.roll(zr, d, 0)
        si = pltpu.roll(zi, d, 0)
        keep = row >= d
        zr, zi = (zr + jnp.where(keep, ar * sr - ai * si, 0.0),
                  zi + jnp.where(keep, ar * si + ai * sr, 0.0))
    a1r, a1i = apow[:SUBLANES, :n], apow[:SUBLANES, n:]
    nxt_r = a1r * zr - a1i * zi + er
    nxt_i = a1r * zi + a1i * zr + ei
    carry_ref[...] = jnp.concatenate(
        [jnp.broadcast_to(nxt_r[SUBLANES - 1:, :], (SUBLANES, n)),
         jnp.broadcast_to(nxt_i[SUBLANES - 1:, :], (SUBLANES, n))], axis=1)

    def add_carry(i, _):
        off = pl.multiple_of(i * SUBLANES, SUBLANES)
        x = xs_ref[pl.ds(off, SUBLANES), :]
        p = ptab_ref[0, pl.ds(off, SUBLANES), :]
        pr, pi = p[:, :n], p[:, n:]
        nr = x[:, :n] + pr * zr - pi * zi
        ni = x[:, n:] + pr * zi + pi * zr
        xs_ref[pl.ds(off, SUBLANES), :] = jnp.concatenate([nr, ni], axis=1)
        return 0

    lax.fori_loop(0, sub, add_carry, 0)

    y = jnp.dot(xs_ref[...].astype(BF16), c_ref[0], preferred_element_type=F32)
    g = jax.nn.gelu(y + d_ref[...] * up)
    for c in range(halves):
        us_ref[c] = g[:, LANES * c:LANES * (c + 1)]
    for j in range(SUBLANES):
        for c in range(halves):
            o_ref[sub * j:sub * (j + 1), LANES * c:LANES * (c + 1)] = (
                us_ref[c, pl.ds(j, sub, stride=SUBLANES), :].astype(o_ref.dtype))


def _s5_tables(a_re, a_im, log_step, b_re, b_im, c_re, c_im, sub):
    g = a_re.shape[0]
    n_slab = g * S5_GROUP // S5_SLAB
    gl = S5_SLAB // S5_GROUP
    dt = jnp.exp(log_step.astype(F32))[:, None]
    ar = a_re.astype(F32)
    ai = a_im.astype(F32)
    mag = jnp.exp(ar * dt)
    lb_re = mag * jnp.cos(ai * dt)
    lb_im = mag * jnp.sin(ai * dt)
    den = ar * ar + ai * ai
    nr = lb_re - 1.0
    coef_re = (nr * ar + lb_im * ai) / den
    coef_im = (lb_im * ar - nr * ai) / den
    bb_re = coef_re[..., None] * b_re - coef_im[..., None] * b_im
    bb_im = coef_re[..., None] * b_im + coef_im[..., None] * b_re
    eye = jnp.eye(gl, dtype=F32)

    def b_slab(t):
        t = t.reshape(n_slab, gl, S5_STATE, S5_GROUP)
        return jnp.einsum("kgpc,gh->kgchp", t, eye).reshape(n_slab, S5_SLAB, gl * S5_STATE)

    def c_slab(t):
        t = t.reshape(n_slab, gl, S5_GROUP, S5_STATE)
        return jnp.einsum("kgcp,gh->kgphc", t, eye).reshape(n_slab, gl * S5_STATE, S5_SLAB)

    b_dense = jnp.concatenate([b_slab(bb_re), b_slab(bb_im)], axis=2).astype(BF16)
    c_dense = jnp.concatenate([c_slab(c_re.astype(F32)), -c_slab(c_im.astype(F32))], axis=1).astype(BF16)

    def flat(t):
        return t.reshape(n_slab, gl * S5_STATE)

    def power(k):
        kk = k.astype(F32)[None, :, None]
        m = jnp.exp(flat(ar * dt)[:, None, :] * kk)
        ph = flat(ai * dt)[:, None, :] * kk
        return jnp.concatenate([m * jnp.cos(ph), m * jnp.sin(ph)], axis=2)

    lam = jnp.repeat(power(jnp.array([1])), SUBLANES, axis=1)
    ptab = jnp.repeat(power(jnp.arange(1, sub + 1)), SUBLANES, axis=1)
    apow = jnp.repeat(power(jnp.array([sub, 2 * sub, 4 * sub])), SUBLANES, axis=1)
    return b_dense, c_dense, lam, ptab, apow


def _s5_mixer(u, a_re, a_im, log_step, b_re, b_im, c_re, c_im, d_skip, sub=S5_SUB):
    seq, d = u.shape
    rows = SUBLANES * sub
    n_slab = d // S5_SLAB
    n2 = 2 * S5_SLAB_STATES
    b_dense, c_dense, lam, ptab, apow = _s5_tables(a_re, a_im, log_step, b_re, b_im, c_re, c_im, sub)
    return pl.pallas_call(
        functools.partial(_s5_kernel, sub=sub),
        out_shape=jax.ShapeDtypeStruct((seq, d), BF16),
        grid=(n_slab, seq // rows),
        in_specs=[
            pl.BlockSpec((rows, S5_SLAB), lambda k, c: (c, k)),
            pl.BlockSpec((1, S5_SLAB, n2), lambda k, c: (k, 0, 0)),
            pl.BlockSpec((1, n2, S5_SLAB), lambda k, c: (k, 0, 0)),
            pl.BlockSpec((1, SUBLANES, n2), lambda k, c: (k, 0, 0)),
            pl.BlockSpec((1, rows, n2), lambda k, c: (k, 0, 0)),
            pl.BlockSpec((1, 3 * SUBLANES, n2), lambda k, c: (k, 0, 0)),
            pl.BlockSpec((1, S5_SLAB), lambda k, c: (0, k)),
        ],
        out_specs=pl.BlockSpec((rows, S5_SLAB), lambda k, c: (c, k)),
        scratch_shapes=[pltpu.VMEM((rows, n2), F32),
                        pltpu.VMEM((SUBLANES, n2), F32),
                        pltpu.VMEM((rows, S5_SLAB), F32),
                        pltpu.VMEM((S5_SLAB // LANES, rows, LANES), F32)],
        compiler_params=_params("arbitrary", "arbitrary"),
        name="s5_scan",
    )(u, b_dense, c_dense, lam, ptab, apow, d_skip.reshape(1, d).astype(F32))


def _cache_weights(first, pairs):
    @pl.when(first)
    def _():
        for src, dst in pairs:
            dst[...] = src[...].astype(BF16)


def _glu_kernel(a_ref, wa_ref, wb_ref, r_ref, o_ref, wa_s, wb_s):
    _cache_weights(pl.program_id(1) == 0, ((wa_ref, wa_s), (wb_ref, wb_s)))
    a = a_ref[...]
    va = jnp.dot(a, wa_s[...], preferred_element_type=F32)
    vb = jnp.dot(a, wb_s[...], preferred_element_type=F32)
    o_ref[...] = r_ref[...] + va * jax.nn.sigmoid(vb)


def _glu_residual(a, w, res, tm=512, tn=512):
    m, k = a.shape
    n = w.shape[1] // 2
    nb = n // tn
    return pl.pallas_call(
        _glu_kernel,
        out_shape=jax.ShapeDtypeStruct((m, n), F32),
        grid=(nb, m // tm),
        in_specs=[pl.BlockSpec((tm, k), lambda j, i: (i, 0)),
                  pl.BlockSpec((k, tn), lambda j, i: (0, j)),
                  pl.BlockSpec((k, tn), lambda j, i: (0, j + nb)),
                  pl.BlockSpec((tm, tn), lambda j, i: (i, j))],
        out_specs=pl.BlockSpec((tm, tn), lambda j, i: (i, j)),
        scratch_shapes=[pltpu.VMEM((k, tn), BF16), pltpu.VMEM((k, tn), BF16)],
        compiler_params=_params("arbitrary", "arbitrary"),
        name="glu_residual",
    )(a, w, w, res)


def _swiglu_up_kernel(a_ref, wg_ref, wu_ref, o_ref, wg_s, wu_s):
    _cache_weights(pl.program_id(1) == 0, ((wg_ref, wg_s), (wu_ref, wu_s)))
    a = a_ref[...]
    vg = jnp.dot(a, wg_s[...], preferred_element_type=F32)
    vu = jnp.dot(a, wu_s[...], preferred_element_type=F32)
    o_ref[...] = (jax.nn.silu(vg) * vu).astype(o_ref.dtype)


def _swiglu_up(a, w_gu, tm=512, tn=512):
    m, k = a.shape
    f = w_gu.shape[1] // 2
    nb = f // tn
    return pl.pallas_call(
        _swiglu_up_kernel,
        out_shape=jax.ShapeDtypeStruct((m, f), BF16),
        grid=(nb, m // tm),
        in_specs=[pl.BlockSpec((tm, k), lambda j, i: (i, 0)),
                  pl.BlockSpec((k, tn), lambda j, i: (0, j)),
                  pl.BlockSpec((k, tn), lambda j, i: (0, j + nb))],
        out_specs=pl.BlockSpec((tm, tn), lambda j, i: (i, j)),
        scratch_shapes=[pltpu.VMEM((k, tn), BF16), pltpu.VMEM((k, tn), BF16)],
        compiler_params=_params("arbitrary", "arbitrary"),
        name="swiglu_up",
    )(a, w_gu, w_gu)


def _mm_res_kernel(a_ref, w_ref, r_ref, o_ref, w_s):
    _cache_weights(pl.program_id(1) == 0, ((w_ref, w_s),))
    o_ref[...] = r_ref[...] + jnp.dot(a_ref[...], w_s[...], preferred_element_type=F32)


def _matmul_residual(a, w, res, tm=512, tn=512):
    m, k = a.shape
    n = w.shape[1]
    return pl.pallas_call(
        _mm_res_kernel,
        out_shape=jax.ShapeDtypeStruct((m, n), F32),
        grid=(n // tn, m // tm),
        in_specs=[pl.BlockSpec((tm, k), lambda j, i: (i, 0)),
                  pl.BlockSpec((k, tn), lambda j, i: (0, j)),
                  pl.BlockSpec((tm, tn), lambda j, i: (i, j))],
        out_specs=pl.BlockSpec((tm, tn), lambda j, i: (i, j)),
        scratch_shapes=[pltpu.VMEM((k, tn), BF16)],
        compiler_params=_params("arbitrary", "arbitrary"),
        name="matmul_residual",
    )(a, w, res)


def _rope_kernel(pos_ref, inv_ref, cos_ref, sa_ref, sb_ref):
    ang = pos_ref[...].astype(F32) * inv_ref[...]
    c = jnp.cos(ang)
    s = jnp.sin(ang)
    lane = lax.broadcasted_iota(jnp.int32, ang.shape, 1)
    first_half = lane < ROPE_DIM // 2
    cos_ref[...] = c
    sa_ref[...] = jnp.where(first_half, -s, 0.0)
    sb_ref[...] = jnp.where(first_half, 0.0, s)


def _rope_tables(pos, tm=512):
    n = pos.shape[0]
    tm = min(tm, n)
    half = ROPE_DIM // 2
    inv = jnp.power(ROPE_THETA, -jnp.arange(half, dtype=F32) / half)
    inv = jnp.concatenate([inv, inv, jnp.zeros((LANES - ROPE_DIM,), F32)]).reshape(1, LANES)
    spec = pl.BlockSpec((tm, LANES), lambda i: (i, 0))
    return pl.pallas_call(
        _rope_kernel,
        out_shape=[jax.ShapeDtypeStruct((n, LANES), F32)] * 3,
        grid=(n // tm,),
        in_specs=[pl.BlockSpec((tm, 1), lambda i: (i, 0)),
                  pl.BlockSpec((1, LANES), lambda i: (0, 0))],
        out_specs=[spec, spec, spec],
        compiler_params=_params("arbitrary"),
        name="rope_tables",
    )(pos.reshape(n, 1), inv)


def _rope(y, c, sa, sb):
    half = ROPE_DIM // 2
    return y * c + pltpu.roll(y, LANES - half, 1) * sa + pltpu.roll(y, half, 1) * sb


def _head_norm(x, gain):
    return x * lax.rsqrt(jnp.mean(x * x, axis=-1, keepdims=True) + RMS_EPS) * gain


def _nsa_proj_kernel(a_ref, w_ref, gain_ref, cos_ref, sa_ref, sb_ref, o_ref, w_s, *, norm_tiles):
    j = pl.program_id(0)
    _cache_weights(pl.program_id(1) == 0, ((w_ref, w_s),))
    acc = jnp.dot(a_ref[...], w_s[...], preferred_element_type=F32)
    is_norm = functools.reduce(jnp.logical_or, [j == t for t in norm_tiles])

    @pl.when(is_norm)
    def _():
        c, sa, sb = cos_ref[...], sa_ref[...], sb_ref[...]
        gain = gain_ref[0]
        for hh in range(acc.shape[1] // HEAD_DIM):
            sl = slice(HEAD_DIM * hh, HEAD_DIM * (hh + 1))
            o_ref[:, sl] = _rope(_head_norm(acc[:, sl], gain), c, sa, sb).astype(o_ref.dtype)

    @pl.when(jnp.logical_not(is_norm))
    def _():
        o_ref[...] = acc.astype(o_ref.dtype)


def _nsa_proj(u, w_in, q_gain, k_gain, rope, tm=512, tn=512):
    m, k = u.shape
    q_dim = GQA_GROUP * N_KV_HEADS * HEAD_DIM
    kv_dim = N_KV_HEADS * HEAD_DIM
    assert kv_dim == tn
    n_q = q_dim // tn
    n_tiles = n_q + 6
    ones = jnp.ones((HEAD_DIM,), F32)
    q_scaled = q_gain.astype(F32) * (HEAD_DIM ** -0.5 * math.log2(math.e))
    gains = jnp.stack([q_scaled] * n_q + [ones, ones, k_gain[1], ones, k_gain[2], ones]).reshape(n_tiles, 1, HEAD_DIM)
    norm_tiles = tuple(range(n_q)) + (n_q + 2, n_q + 4)
    tab = pl.BlockSpec((tm, LANES), lambda j, i: (i, 0))
    return pl.pallas_call(
        functools.partial(_nsa_proj_kernel, norm_tiles=norm_tiles),
        out_shape=jax.ShapeDtypeStruct((m, n_tiles * tn), BF16),
        grid=(n_tiles, m // tm),
        in_specs=[pl.BlockSpec((tm, k), lambda j, i: (i, 0)),
                  pl.BlockSpec((k, tn), lambda j, i: (0, j)),
                  pl.BlockSpec((1, 1, HEAD_DIM), lambda j, i: (j, 0, 0)),
                  tab, tab, tab],
        out_specs=pl.BlockSpec((tm, tn), lambda j, i: (i, j)),
        scratch_shapes=[pltpu.VMEM((k, tn), BF16)],
        compiler_params=_params("arbitrary", "arbitrary"),
        name="nsa_proj",
    )(u, w_in, gains, *rope)


def _gate_kernel(a_ref, w_ref, o_ref):
    o_ref[...] = jax.nn.sigmoid(jnp.dot(a_ref[...], w_ref[...].astype(BF16), preferred_element_type=F32))


def _nsa_gates(u, w_gate, tm=512):
    m, k = u.shape
    n = w_gate.shape[1]
    w_pad = jnp.pad(w_gate, ((0, 0), (0, LANES - n)))
    return pl.pallas_call(
        _gate_kernel,
        out_shape=jax.ShapeDtypeStruct((m, LANES), F32),
        grid=(m // tm,),
        in_specs=[pl.BlockSpec((tm, k), lambda i: (i, 0)),
                  pl.BlockSpec((k, LANES), lambda i: (0, 0))],
        out_specs=pl.BlockSpec((tm, LANES), lambda i: (i, 0)),
        compiler_params=_params("arbitrary"),
        name="nsa_gates",
    )(u, w_pad)


def _compress_kernel(*refs, is_key):
    if is_key:
        ca_ref, cb_ref, pe_ref, w1_ref, w2_ref, gain_ref, cos_ref, sa_ref, sb_ref, o_ref = refs
    else:
        ca_ref, cb_ref, pe_ref, w1_ref, w2_ref, o_ref = refs
    half = w1_ref.shape[0] // 2
    pe = pe_ref[...]
    xa = (ca_ref[0].astype(F32) + pe[:, :half]).astype(BF16)
    xb = (cb_ref[0].astype(F32) + pe[:, half:]).astype(BF16)
    hid = (jnp.dot(xa, w1_ref[:half, :].astype(BF16), preferred_element_type=F32)
           + jnp.dot(xb, w1_ref[half:, :].astype(BF16), preferred_element_type=F32))
    out = jnp.dot(jax.nn.gelu(hid).astype(BF16), w2_ref[...].astype(BF16), preferred_element_type=F32)
    if is_key:
        out = _rope(_head_norm(out, gain_ref[...]), cos_ref[...], sa_ref[...], sb_ref[...])
    o_ref[0] = out.astype(o_ref.dtype)


def _compress(t, pe, w1, w2, key_extras=None):
    seq = t.shape[0]
    nc = seq // CMP_STRIDE
    width = CMP_STRIDE * HEAD_DIM
    ca = t.reshape(nc, CMP_STRIDE, N_KV_HEADS, HEAD_DIM).transpose(2, 0, 1, 3).reshape(N_KV_HEADS, nc, width)
    cb = jnp.concatenate([ca[:, 1:], jnp.zeros((N_KV_HEADS, 1, width), ca.dtype)], axis=1)
    blk = pl.BlockSpec((1, nc, width), lambda h: (h, 0, 0))
    full = lambda a: pl.BlockSpec(a.shape, lambda h: (0,) * a.ndim)
    args = [ca, cb, pe.reshape(1, CMP_BLOCK * HEAD_DIM), w1, w2]
    if key_extras is not None:
        args += list(key_extras)
    return pl.pallas_call(
        functools.partial(_compress_kernel, is_key=key_extras is not None),
        out_shape=jax.ShapeDtypeStruct((N_KV_HEADS, nc, HEAD_DIM), BF16),
        grid=(N_KV_HEADS,),
        in_specs=[blk, blk] + [full(a) for a in args[2:]],
        out_specs=pl.BlockSpec((1, nc, HEAD_DIM), lambda h: (h, 0, 0)),
        compiler_params=_params("arbitrary"),
        name="nsa_compress_k" if key_extras is not None else "nsa_compress_v",
    )(*args)


def _dot_nt(a, b):
    return lax.dot_general(a, b, (((1,), (1,)), ((), ())), preferred_element_type=F32)


def _split3(x):
    hi = x.astype(BF16)
    r1 = x - hi.astype(F32)
    mid = r1.astype(BF16)
    lo = (r1 - mid.astype(F32)).astype(BF16)
    return hi, mid, lo


def _nsa_attn_kernel(q_ref, kc_ref, vct_ref, ks_ref, vst_ref, kw_ref, vwt_ref, gate_ref, blk_ref, o_ref,
                     acc_ref, mix_ref, m_ref, l_ref, qa_ref, s0_ref, s1_ref, sc_ref, sw_ref, *, seq, tk):
    t0 = pl.program_id(1) * Q_BLOCK
    nc = kc_ref.shape[1]
    ns = seq // SEL_BLOCK
    grp = GQA_GROUP
    cols = grp * Q_BLOCK
    sel_shift = int(math.log2(SEL_BLOCK))
    q = q_ref[...].astype(F32)
    qt = jnp.concatenate([q[:, HEAD_DIM * g:HEAD_DIM * (g + 1)].T for g in range(grp)], axis=1).astype(BF16)
    t_row = t0 + lax.broadcasted_iota(jnp.int32, (1, Q_BLOCK), 1)

    def heads(x):
        return jnp.concatenate([x] * grp, axis=1)

    def softmax_cols(s, ok):
        sb = s + heads(jnp.where(ok, 0.0, NEG))
        m = jnp.max(sb, axis=0, keepdims=True)
        e = jnp.exp2(sb - m)
        return e, m, jnp.maximum(jnp.sum(e, axis=0, keepdims=True), 1e-30)

    span = WINDOW + Q_BLOCK
    w0 = pl.multiple_of(jnp.maximum(t0 - WINDOW, 0), Q_BLOCK)
    sc_ref[...] = jnp.dot(kc_ref[0], qt, preferred_element_type=F32)
    sw_ref[...] = jnp.dot(kw_ref[pl.ds(w0, span), :], qt, preferred_element_type=F32)
    gate = gate_ref[0, 0]

    n_idx = lax.broadcasted_iota(jnp.int32, (nc, Q_BLOCK), 0)
    ok_c = (n_idx * CMP_STRIDE + (CMP_BLOCK - 1) <= t_row) & (n_idx < nc - 1)
    e_c, m_c, den_c = softmax_cols(sc_ref[...], ok_c)
    p_c = e_c * jnp.where(m_c > 0.5 * NEG, 1.0 / den_c, 0.0)
    mix_ref[...] = gate[0:1] * jnp.dot(vct_ref[0], p_c.astype(BF16), preferred_element_type=F32)
    imp = p_c[:, :Q_BLOCK]
    for g in range(1, grp):
        imp = imp + p_c[:, Q_BLOCK * g:Q_BLOCK * (g + 1)]

    rel = t_row - (w0 + lax.broadcasted_iota(jnp.int32, (span, Q_BLOCK), 0))
    e_w, _, den_w = softmax_cols(sw_ref[...], (rel >= 0) & (rel < WINDOW))
    mix_ref[...] += gate[2:3] * jnp.dot(vwt_ref[0, :, pl.ds(w0, span)], (e_w * (1.0 / den_w)).astype(BF16),
                                        preferred_element_type=F32)

    ratio = SEL_BLOCK // CMP_STRIDE
    d = (lax.broadcasted_iota(jnp.int32, (ns, nc), 1)
         - ratio * lax.broadcasted_iota(jnp.int32, (ns, nc), 0))
    overlap = jnp.zeros((ns, nc), F32)
    for n in range(CMP_BLOCK // CMP_STRIDE):
        overlap = overlap + jnp.where((d - n >= 0) & (d - n < ratio), 1.0, 0.0)
    overlap = overlap.astype(BF16)
    p_slc = sum(jnp.dot(overlap, part, preferred_element_type=F32) for part in _split3(imp))

    j_idx = lax.broadcasted_iota(jnp.int32, (ns, Q_BLOCK), 0)
    j_f = j_idx.astype(F32)
    dist = jnp.right_shift(t_row, sel_shift) - j_idx
    forced = (j_idx == 0) | ((dist >= 0) & (dist < SEL_LOCAL))
    score = jnp.where(forced, jnp.inf, jnp.where(dist >= 0, p_slc, -jnp.inf))
    sel = jnp.zeros((ns, Q_BLOCK), F32)
    for _ in range(min(SEL_TOPK, ns)):
        top = jnp.max(score, axis=0, keepdims=True)
        idx = jnp.min(jnp.where(score == top, j_f, float(ns)), axis=0, keepdims=True)
        pick = j_f == idx
        sel = jnp.where(pick, 1.0, sel)
        score = jnp.where(pick, -jnp.inf, score)

    qa_ref[...] = jnp.concatenate([qt, heads(jnp.where(sel > 0.0, 0.0, NEG).astype(BF16))], axis=0)
    acc_ref[...] = jnp.zeros_like(acc_ref)
    m_ref[...] = jnp.full_like(m_ref, NEG)
    l_ref[...] = jnp.zeros_like(l_ref)

    def score_tile(kt, dst):
        k0 = pl.multiple_of(kt * tk, tk)
        k_aug = jnp.concatenate([ks_ref[pl.ds(k0, tk), :], blk_ref[pl.ds(k0, tk), :]], axis=1)
        dst[...] = jnp.dot(k_aug, qa_ref[...], preferred_element_type=F32)

    def consume_tile(kt, src, diagonal):
        k0 = pl.multiple_of(kt * tk, tk)
        sb = src[...]
        if diagonal:
            kpos = k0 + lax.broadcasted_iota(jnp.int32, (tk, Q_BLOCK), 0)
            sb = sb + heads(jnp.where(kpos <= t_row, 0.0, NEG))
        m_i = m_ref[...]
        m_new = jnp.maximum(m_i, jnp.max(sb, axis=0, keepdims=True))
        e = jnp.exp2(sb - m_new)
        alpha = jnp.exp2(m_i - m_new)
        m_ref[...] = m_new
        l_ref[...] = alpha * l_ref[...] + jnp.sum(e, axis=0, keepdims=True)
        acc_ref[...] = alpha * acc_ref[...] + jnp.dot(vst_ref[0, :, pl.ds(k0, tk)], e.astype(BF16),
                                                      preferred_element_type=F32)

    last = t0 // tk
    score_tile(0, s0_ref)

    def tile_pair(i, _):
        score_tile(2 * i + 1, s1_ref)
        consume_tile(2 * i, s0_ref, False)
        score_tile(2 * i + 2, s0_ref)
        consume_tile(2 * i + 1, s1_ref, False)
        return 0

    lax.fori_loop(0, last // 2, tile_pair, 0)

    @pl.when(last % 2 == 1)
    def _():
        score_tile(last, s1_ref)
        consume_tile(last - 1, s0_ref, False)
        consume_tile(last, s1_ref, True)

    @pl.when(last % 2 == 0)
    def _():
        consume_tile(last, s0_ref, True)

    o_s = acc_ref[...] * (1.0 / jnp.maximum(l_ref[...], 1e-30))

    mixed = mix_ref[...] + gate_ref[0, 0, 1:2] * o_s
    for g in range(grp):
        o_ref[:, HEAD_DIM * g:HEAD_DIM * (g + 1)] = mixed[:, Q_BLOCK * g:Q_BLOCK * (g + 1)].T.astype(o_ref.dtype)


def _nsa_attention(proj, kcmp, vcmp_t, vsl_t, vw_t, gates, tk=512):
    seq = proj.shape[0]
    tk = min(tk, seq)
    q_dim = GQA_GROUP * N_KV_HEADS * HEAD_DIM
    kv_blocks = N_KV_HEADS
    first = q_dim // HEAD_DIM + 2 * kv_blocks
    nc = kcmp.shape[1]
    cols = GQA_GROUP * Q_BLOCK

    def k_spec(which):
        return pl.BlockSpec((seq, HEAD_DIM), lambda h, qb: (0, first + which * kv_blocks + h))

    def vt_spec(n):
        return pl.BlockSpec((1, HEAD_DIM, n), lambda h, qb: (h, 0, 0))

    q_spec = pl.BlockSpec((Q_BLOCK, GQA_GROUP * HEAD_DIM), lambda h, qb: (qb, h))
    ns = seq // SEL_BLOCK
    key_block = (jnp.arange(seq, dtype=jnp.int32)[:, None] // SEL_BLOCK
                 == jnp.arange(ns, dtype=jnp.int32)[None, :]).astype(BF16)
    return pl.pallas_call(
        functools.partial(_nsa_attn_kernel, seq=seq, tk=tk),
        out_shape=jax.ShapeDtypeStruct((seq, q_dim), BF16),
        grid=(N_KV_HEADS, seq // Q_BLOCK),
        in_specs=[q_spec, pl.BlockSpec((1, nc, HEAD_DIM), lambda h, qb: (h, 0, 0)), vt_spec(nc),
                  k_spec(0), vt_spec(seq), k_spec(2), vt_spec(seq),
                  pl.BlockSpec((1, 1, 3, cols), lambda h, qb: (h, qb, 0, 0)),
                  pl.BlockSpec((seq, ns), lambda h, qb: (0, 0))],
        out_specs=q_spec,
        scratch_shapes=[pltpu.VMEM((HEAD_DIM, cols), F32), pltpu.VMEM((HEAD_DIM, cols), F32),
                        pltpu.VMEM((1, cols), F32), pltpu.VMEM((1, cols), F32),
                        pltpu.VMEM((HEAD_DIM + ns, cols), BF16),
                        pltpu.VMEM((tk, cols), F32), pltpu.VMEM((tk, cols), F32),
                        pltpu.VMEM((nc, cols), F32), pltpu.VMEM((WINDOW + Q_BLOCK, cols), F32)],
        compiler_params=_params("arbitrary", "arbitrary"),
        name="nsa_attention",
    )(proj, kcmp, vcmp_t, proj, vsl_t, proj, vw_t, gates, key_block)


def _nsa_mixer(u, positions, w_in, q_gain, k_gain, pe_k, pe_v, ck_w1, ck_w2, cv_w1, cv_w2):
    seq = u.shape[0]
    q_dim = GQA_GROUP * N_KV_HEADS * HEAD_DIM
    kv_dim = N_KV_HEADS * HEAD_DIM
    n_main = q_dim + 6 * kv_dim
    nc = seq // CMP_STRIDE
    rope = _rope_tables(positions)
    proj = _nsa_proj(u, w_in, q_gain, k_gain, rope)
    gate = _nsa_gates(u, w_in[:, n_main:])
    gates = (gate[:, :3 * N_KV_HEADS * GQA_GROUP].reshape(seq // Q_BLOCK, Q_BLOCK, 3, N_KV_HEADS, GQA_GROUP)
             .transpose(3, 0, 2, 4, 1).reshape(N_KV_HEADS, seq // Q_BLOCK, 3, GQA_GROUP * Q_BLOCK))

    def keys_last(cols):
        return cols.reshape(seq, N_KV_HEADS, HEAD_DIM).transpose(1, 2, 0)

    pos_cmp = jnp.concatenate([positions[CMP_BLOCK - 1::CMP_STRIDE][:nc - 1], jnp.zeros((1,), positions.dtype)])
    rope_cmp = _rope_tables(pos_cmp)
    kcmp = _compress(proj[:, q_dim:q_dim + kv_dim], pe_k, ck_w1, ck_w2,
                     key_extras=(k_gain[0].reshape(1, HEAD_DIM),) + tuple(rope_cmp))
    vcmp = _compress(proj[:, q_dim + kv_dim:q_dim + 2 * kv_dim], pe_v, cv_w1, cv_w2)
    vsl_t = keys_last(proj[:, q_dim + 3 * kv_dim:q_dim + 4 * kv_dim])
    vw_t = keys_last(proj[:, q_dim + 5 * kv_dim:q_dim + 6 * kv_dim])
    return _nsa_attention(proj, kcmp, vcmp.transpose(0, 2, 1), vsl_t, vw_t, gates)


def _router_kernel(x_ref, g_ref, w_ref, b_ref, u_ref, r_ref):
    x = x_ref[...]
    u = x * lax.rsqrt(jnp.mean(x * x, axis=-1, keepdims=True) + RMS_EPS) * g_ref[...]
    u_ref[...] = u
    uh, um, _ = _split3(u)
    wh, wm, _ = _split3(w_ref[...])
    logits = (jnp.dot(uh, wh, preferred_element_type=F32) + jnp.dot(uh, wm, preferred_element_type=F32)
              + jnp.dot(um, wh, preferred_element_type=F32)) + b_ref[...]
    lane = lax.broadcasted_iota(jnp.int32, logits.shape, 1).astype(F32)
    lg = jnp.where(lane < N_EXPERTS, logits, -jnp.inf)
    v1 = jnp.max(lg, axis=-1, keepdims=True)
    i1 = jnp.min(jnp.where(lg == v1, lane, float(LANES)), axis=-1, keepdims=True)
    lg = jnp.where(lane == i1, -jnp.inf, lg)
    v2 = jnp.max(lg, axis=-1, keepdims=True)
    i2 = jnp.min(jnp.where(lg == v2, lane, float(LANES)), axis=-1, keepdims=True)
    e2 = jnp.exp(v2 - v1)
    den = 1.0 + e2
    r_ref[...] = jnp.where(lane == 0, i1, jnp.where(lane == 1, i2, jnp.where(
        lane == 2, 1.0 / den, jnp.where(lane == 3, e2 / den, 0.0))))


def _router(h, gain, w_router, b_router, tm=256):
    m, d = h.shape
    w_pad = jnp.pad(w_router.astype(F32), ((0, 0), (0, LANES - N_EXPERTS)))
    b_pad = jnp.pad(b_router.astype(F32), (0, LANES - N_EXPERTS)).reshape(1, LANES)
    return pl.pallas_call(
        _router_kernel,
        out_shape=[jax.ShapeDtypeStruct((m, d), F32), jax.ShapeDtypeStruct((m, LANES), F32)],
        grid=(m // tm,),
        in_specs=[pl.BlockSpec((tm, d), lambda i: (i, 0)),
                  pl.BlockSpec((1, d), lambda i: (0, 0)),
                  pl.BlockSpec((d, LANES), lambda i: (0, 0)),
                  pl.BlockSpec((1, LANES), lambda i: (0, 0))],
        out_specs=[pl.BlockSpec((tm, d), lambda i: (i, 0)), pl.BlockSpec((tm, LANES), lambda i: (i, 0))],
        compiler_params=_params("arbitrary"),
        name="moe_router",
    )(h, gain.reshape(1, d), w_pad, b_pad)


def _row_copy(src_hbm, row, dst, r, sem):
    return pltpu.make_async_copy(src_hbm.at[pl.ds(row, 1), :], dst.at[pl.ds(r, 1), :], sem)


def _gather_kernel(idx_ref, used_ref, src_hbm, o_ref, buf, sem):
    rows = o_ref.shape[0]
    i = pl.program_id(0)
    n_used = used_ref[0]

    def issue(blk):
        slot = blk % 2

        def start(r, _):
            _row_copy(src_hbm, idx_ref[blk * rows + r], buf.at[slot], r, sem.at[slot]).start()
            return 0

        lax.fori_loop(0, rows, start, 0, unroll=DMA_UNROLL)

    @pl.when(i == 0)
    def _():
        issue(i)

    @pl.when(i + 1 < n_used)
    def _():
        issue(i + 1)

    @pl.when(i < n_used)
    def _():
        slot = i % 2

        def wait(r, _):
            _row_copy(src_hbm, 0, buf.at[slot], r, sem.at[slot]).wait()
            return 0

        lax.fori_loop(0, rows, wait, 0, unroll=DMA_UNROLL)
        o_ref[...] = buf[slot].astype(o_ref.dtype)

    @pl.when(i >= n_used)
    def _():
        o_ref[...] = jnp.zeros_like(o_ref)


def _gather_rows(src, idx, n_used, out_dtype, rows=MOE_ROWS):
    n = idx.shape[0]
    d = src.shape[1]
    return pl.pallas_call(
        _gather_kernel,
        out_shape=jax.ShapeDtypeStruct((n, d), out_dtype),
        grid_spec=pltpu.PrefetchScalarGridSpec(
            num_scalar_prefetch=2,
            grid=(n // rows,),
            in_specs=[pl.BlockSpec(memory_space=pl.ANY)],
            out_specs=pl.BlockSpec((rows, d), lambda i, idx, used: (i, 0)),
            scratch_shapes=[pltpu.VMEM((2, rows, d), src.dtype), pltpu.SemaphoreType.DMA((2,))]),
        compiler_params=_params("arbitrary"),
        name="moe_gather",
    )(idx, n_used, src)


def _block_state(be_ref, used_ref, i):
    changed = (i == 0) | (be_ref[i] != be_ref[jnp.maximum(i - 1, 0)])
    used = i < used_ref[0]
    return used, used & changed


def _last_used(i, used):
    return jnp.minimum(i, used[0] - 1)


def _moe_up_kernel(be_ref, used_ref, x_ref, wg_ref, wu_ref, o_ref, wg_s, wu_s):
    used, first = _block_state(be_ref, used_ref, pl.program_id(1))

    @pl.when(first)
    def _():
        wg_s[...] = wg_ref[0].astype(BF16)
        wu_s[...] = wu_ref[0].astype(BF16)

    @pl.when(used)
    def _():
        a = x_ref[...]
        vg = jnp.dot(a, wg_s[...], preferred_element_type=F32)
        vu = jnp.dot(a, wu_s[...], preferred_element_type=F32)
        o_ref[...] = (jax.nn.silu(vg) * vu).astype(o_ref.dtype)

    @pl.when(jnp.logical_not(used))
    def _():
        o_ref[...] = jnp.zeros_like(o_ref)


def _moe_up(x_rows, blk_e, n_used, w_gu, tn=512, rows=MOE_ROWS):
    n, k = x_rows.shape
    f = w_gu.shape[2] // 2
    nb = f // tn
    return pl.pallas_call(
        _moe_up_kernel,
        out_shape=jax.ShapeDtypeStruct((n, f), BF16),
        grid_spec=pltpu.PrefetchScalarGridSpec(
            num_scalar_prefetch=2,
            grid=(nb, n // rows),
            in_specs=[pl.BlockSpec((rows, k), lambda j, i, be, nu: (_last_used(i, nu), 0)),
                      pl.BlockSpec((1, k, tn), lambda j, i, be, nu: (be[_last_used(i, nu)], 0, j)),
                      pl.BlockSpec((1, k, tn), lambda j, i, be, nu: (be[_last_used(i, nu)], 0, j + nb))],
            out_specs=pl.BlockSpec((rows, tn), lambda j, i, be, nu: (i, j)),
            scratch_shapes=[pltpu.VMEM((k, tn), BF16), pltpu.VMEM((k, tn), BF16)]),
        compiler_params=_params("arbitrary", "arbitrary"),
        name="moe_up",
    )(blk_e, n_used, x_rows, w_gu, w_gu)


def _moe_down_kernel(be_ref, used_ref, a_ref, w_ref, o_ref, w_s):
    used, first = _block_state(be_ref, used_ref, pl.program_id(1))

    @pl.when(first)
    def _():
        w_s[...] = w_ref[0].astype(BF16)

    @pl.when(used)
    def _():
        o_ref[...] = jnp.dot(a_ref[...], w_s[...], preferred_element_type=F32)

    @pl.when(jnp.logical_not(used))
    def _():
        o_ref[...] = jnp.zeros_like(o_ref)


def _moe_down(act, blk_e, n_used, w_down, tn=512, rows=MOE_ROWS):
    n, k = act.shape
    d = w_down.shape[2]
    return pl.pallas_call(
        _moe_down_kernel,
        out_shape=jax.ShapeDtypeStruct((n, d), F32),
        grid_spec=pltpu.PrefetchScalarGridSpec(
            num_scalar_prefetch=2,
            grid=(d // tn, n // rows),
            in_specs=[pl.BlockSpec((rows, k), lambda j, i, be, nu: (_last_used(i, nu), 0)),
                      pl.BlockSpec((1, k, tn), lambda j, i, be, nu: (be[_last_used(i, nu)], 0, j))],
            out_specs=pl.BlockSpec((rows, tn), lambda j, i, be, nu: (i, j)),
            scratch_shapes=[pltpu.VMEM((k, tn), BF16)]),
        compiler_params=_params("arbitrary", "arbitrary"),
        name="moe_down",
    )(blk_e, n_used, act, w_down)


def _combine_kernel(dest_ref, h_ref, r_ref, rows_hbm, o_ref, buf, sem):
    tm = h_ref.shape[0]
    base = pl.program_id(0) * tm

    def start(r, _):
        for k in range(2):
            _row_copy(rows_hbm, dest_ref[2 * (base + r) + k], buf.at[k], r, sem.at[k]).start()
        return 0

    def wait(r, _):
        for k in range(2):
            _row_copy(rows_hbm, 0, buf.at[k], r, sem.at[k]).wait()
        return 0

    lax.fori_loop(0, tm, start, 0, unroll=DMA_UNROLL)
    lax.fori_loop(0, tm, wait, 0, unroll=DMA_UNROLL)
    w = r_ref[...]
    o_ref[...] = h_ref[...] + (w[:, 2:3] * buf[0] + w[:, 3:4] * buf[1])


def _moe_combine(h, route, out_rows, dest, tm=256):
    m, d = h.shape
    return pl.pallas_call(
        _combine_kernel,
        out_shape=jax.ShapeDtypeStruct((m, d), F32),
        grid_spec=pltpu.PrefetchScalarGridSpec(
            num_scalar_prefetch=1,
            grid=(m // tm,),
            in_specs=[pl.BlockSpec((tm, d), lambda i, dest: (i, 0)),
                      pl.BlockSpec((tm, LANES), lambda i, dest: (i, 0)),
                      pl.BlockSpec(memory_space=pl.ANY)],
            out_specs=pl.BlockSpec((tm, d), lambda i, dest: (i, 0)),
            scratch_shapes=[pltpu.VMEM((2, tm, d), F32), pltpu.SemaphoreType.DMA((2,))]),
        compiler_params=_params("arbitrary"),
        name="moe_combine",
    )(dest.reshape(-1), h, route, out_rows)


def _moe_layout(top_e, rows=MOE_ROWS):
    n_tok = top_e.shape[0]
    e_flat = top_e.reshape(-1)
    onehot = (e_flat[:, None] == jnp.arange(N_EXPERTS, dtype=jnp.int32)[None, :]).astype(jnp.int32)
    csum = jnp.cumsum(onehot, axis=0)
    rank = jnp.take_along_axis(csum, e_flat[:, None], axis=1)[:, 0] - 1
    counts = csum[-1]
    padded = (counts + rows - 1) // rows * rows
    pad_end = jnp.cumsum(padded)
    dest = (pad_end - padded)[e_flat] + rank
    n_rows = e_flat.shape[0] + N_EXPERTS * rows
    t_flat = jnp.repeat(jnp.arange(n_tok, dtype=jnp.int32), top_e.shape[1])
    row_tok = jnp.zeros((n_rows,), jnp.int32).at[dest].set(t_flat)
    n_blk = n_rows // rows
    blk_start = jnp.arange(n_blk, dtype=jnp.int32) * rows
    blk_e = jnp.minimum(jnp.sum(blk_start[:, None] >= pad_end[None, :], axis=1), N_EXPERTS - 1).astype(jnp.int32)
    n_used = (pad_end[-1:] // rows).astype(jnp.int32)
    return row_tok, blk_e, n_used, dest.astype(jnp.int32).reshape(n_tok, -1)


def _moe_ffn_residual(h, gain, w_router, b_router, w_gu, w_down):
    u, route = _router(h, gain, w_router, b_router)
    top_e = route[:, :2].astype(jnp.int32)
    row_tok, blk_e, n_used, dest = _moe_layout(top_e)
    x_rows = _gather_rows(u, row_tok, n_used, BF16)
    act = _moe_up(x_rows, blk_e, n_used, w_gu)
    out_rows = _moe_down(act, blk_e, n_used, w_down)
    return _moe_combine(h, route, out_rows, dest)


def kernel(x, positions, norm_mix, norm_ffn, s5_a_re, s5_a_im, s5_log_step, s5_b_re, s5_b_im, s5_c_re, s5_c_im, s5_d, s5_w_glu, nsa_w_in, nsa_q_gain, nsa_k_gain, nsa_pe_k, nsa_pe_v, nsa_ck_w1, nsa_ck_w2, nsa_cv_w1, nsa_cv_w2, nsa_w_out, ffn_w_gu, ffn_w_down, moe_w_router, moe_b_router, moe_w_gu, moe_w_down):
    bsz, seq, d = x.shape
    assert bsz == 1, "the scan and attention kernels take one sequence"
    h = x.reshape(seq, d)
    h = _layer_s5(h, norm_mix[0], norm_ffn[0], s5_a_re[0], s5_a_im[0], s5_log_step[0], s5_b_re[0],
                  s5_b_im[0], s5_c_re[0], s5_c_im[0], s5_d[0], s5_w_glu[0], ffn_w_gu[0], ffn_w_down[0])
    h = _layer_nsa(h, positions[0], norm_mix[1], norm_ffn[1], nsa_w_in[0], nsa_q_gain[0], nsa_k_gain[0],
                   nsa_pe_k[0], nsa_pe_v[0], nsa_ck_w1[0], nsa_ck_w2[0], nsa_cv_w1[0], nsa_cv_w2[0],
                   nsa_w_out[0], moe_w_router[0], moe_b_router[0], moe_w_gu[0], moe_w_down[0])
    return h.reshape(bsz, seq, d)


def _layer_nsa(h, positions, g_mix, g_ffn, w_in, q_gain, k_gain, pe_k, pe_v, ck_w1, ck_w2, cv_w1, cv_w2,
               w_out, w_router, b_router, w_gu, w_down):
    u = _rms_norm(h, g_mix, BF16)
    o = _nsa_mixer(u, positions, w_in, q_gain, k_gain, pe_k, pe_v, ck_w1, ck_w2, cv_w1, cv_w2)
    h = _matmul_residual(o, w_out, h)
    return _moe_ffn_residual(h, g_ffn, w_router, b_router, w_gu, w_down)


def _layer_s5(h, g_mix, g_ffn, a_re, a_im, log_step, b_re, b_im, c_re, c_im, d_skip, w_glu, w_gu, w_down):
    u = _rms_norm(h, g_mix, F32)
    g = _s5_mixer(u, a_re, a_im, log_step, b_re, b_im, c_re, c_im, d_skip)
    h = _glu_residual(g, w_glu, h)
    u = _rms_norm(h, g_ffn, BF16)
    act = _swiglu_up(u, w_gu)
    return _matmul_residual(act, w_down, h)
```

```python
import functools
import math

import jax
import jax.numpy as jnp
from jax import lax
from jax.experimental import pallas as pl
from jax.experimental.pallas import tpu as pltpu

F32 = jnp.float32
BF16 = jnp.bfloat16

RMS_EPS = 1e-6
S5_GROUP = 16
S5_STATE = 64
HEAD_DIM = 128
N_KV_HEADS = 4
GQA_GROUP = 4
ROPE_DIM = 32
ROPE_THETA = 500000.0
CMP_BLOCK = 32
CMP_STRIDE = 16
SEL_BLOCK = 64
SEL_TOPK = 16
SEL_LOCAL = 2
WINDOW = 512
Q_BLOCK = 128
N_EXPERTS = 8
NEG = -1e30

LANES = 128
SUBLANES = 8
VMEM_LIMIT = 56 * 1024 * 1024

S5_SLAB = 256
S5_SLAB_STATES = S5_SLAB // S5_GROUP * S5_STATE
S5_SUB = 64
S5_PAIR = 2
MOE_ROWS = 512
DMA_UNROLL = 8


def _params(*sem):
    return pltpu.CompilerParams(dimension_semantics=sem, vmem_limit_bytes=VMEM_LIMIT)


def _rms_kernel(x_ref, g_ref, o_ref):
    x = x_ref[...]
    ms = jnp.mean(x * x, axis=-1, keepdims=True)
    o_ref[...] = (x * lax.rsqrt(ms + RMS_EPS) * g_ref[...]).astype(o_ref.dtype)


def _rms_norm(x, gain, out_dtype, tm=512):
    m, d = x.shape
    return pl.pallas_call(
        _rms_kernel,
        out_shape=jax.ShapeDtypeStruct((m, d), out_dtype),
        grid=(m // tm,),
        in_specs=[pl.BlockSpec((tm, d), lambda i: (i, 0)),
                  pl.BlockSpec((1, d), lambda i: (0, 0))],
        out_specs=pl.BlockSpec((tm, d), lambda i: (i, 0)),
        compiler_params=_params("arbitrary"),
        name="rms_norm",
    )(x, gain.reshape(1, d))


def _s5_kernel(u_ref, b_ref, c_ref, lam_ref, ptab_ref, apow_ref, d_ref, o_ref,
               xs_ref, carry_ref, up_ref, us_ref, *, sub):
    n = S5_SLAB_STATES
    halves = S5_SLAB // LANES
    row = lax.broadcasted_iota(jnp.int32, (SUBLANES, n), 0)

    @pl.when(pl.program_id(1) == 0)
    def _():
        carry_ref[...] = jnp.zeros_like(carry_ref)

    def in_proj(s):
        for c in range(halves):
            us_ref[s * halves + c] = u_ref[:, S5_SLAB * s + LANES * c:S5_SLAB * s + LANES * (c + 1)]
        for i in range(sub):
            for c in range(halves):
                up_ref[s, SUBLANES * i:SUBLANES * (i + 1), LANES * c:LANES * (c + 1)] = (
                    us_ref[s * halves + c, pl.ds(i, SUBLANES, stride=sub), :])
        xs_ref[s] = jnp.dot(up_ref[s].astype(BF16), b_ref[s], preferred_element_type=F32)

    def scan(s):
        lam = lam_ref[s]
        lr, li = lam[:, :n], lam[:, n:]
        hr = hi = jnp.zeros((SUBLANES, n), F32)
        for i in range(sub):
            rows = slice(SUBLANES * i, SUBLANES * (i + 1))
            x = xs_ref[s, rows, :]
            hr, hi = lr * hr - li * hi + x[:, :n], lr * hi + li * hr + x[:, n:]
            xs_ref[s, rows, :] = jnp.concatenate([hr, hi], axis=1)
        er, ei = hr, hi

        cin = carry_ref[s]
        zr = jnp.where(row == 0, cin[:, :n], pltpu.roll(er, 1, 0))
        zi = jnp.where(row == 0, cin[:, n:], pltpu.roll(ei, 1, 0))
        apow = apow_ref[s]
        for k, d in enumerate((1, 2, 4)):
            ar = apow[SUBLANES * k:SUBLANES * (k + 1), :n]
            ai = apow[SUBLANES * k:SUBLANES * (k + 1), n:]
            sr = pltpu.roll(zr, d, 0)
            si = pltpu.roll(zi, d, 0)
            keep = row >= d
            zr, zi = (zr + jnp.where(keep, ar * sr - ai * si, 0.0),
                      zi + jnp.where(keep, ar * si + ai * sr, 0.0))
        a1r, a1i = apow[:SUBLANES, :n], apow[:SUBLANES, n:]
        nxt_r = a1r * zr - a1i * zi + er
        nxt_i = a1r * zi + a1i * zr + ei
        carry_ref[s] = jnp.concatenate(
            [jnp.broadcast_to(nxt_r[SUBLANES - 1:, :], (SUBLANES, n)),
             jnp.broadcast_to(nxt_i[SUBLANES - 1:, :], (SUBLANES, n))], axis=1)

        for i in range(sub):
            rows = slice(SUBLANES * i, SUBLANES * (i + 1))
            x = xs_ref[s, rows, :]
            p = ptab_ref[s, rows, :]
            pr, pi = p[:, :n], p[:, n:]
            xs_ref[s, rows, :] = jnp.concatenate([x[:, :n] + pr * zr - pi * zi,
                                                  x[:, n:] + pr * zi + pi * zr], axis=1)

    def out_proj(s):
        y = jnp.dot(xs_ref[s].astype(BF16), c_ref[s], preferred_element_type=F32)
        g = jax.nn.gelu(y + d_ref[:, S5_SLAB * s:S5_SLAB * (s + 1)] * up_ref[s])
        for c in range(halves):
            us_ref[s * halves + c] = g[:, LANES * c:LANES * (c + 1)]
        for j in range(SUBLANES):
            for c in range(halves):
                o_ref[sub * j:sub * (j + 1), S5_SLAB * s + LANES * c:S5_SLAB * s + LANES * (c + 1)] = (
                    us_ref[s * halves + c, pl.ds(j, sub, stride=SUBLANES), :].astype(o_ref.dtype))

    for s in range(S5_PAIR):
        in_proj(s)
    for s in range(S5_PAIR):
        scan(s)
        out_proj(s)


def _s5_tables(a_re, a_im, log_step, b_re, b_im, c_re, c_im, sub):
    g = a_re.shape[0]
    n_slab = g * S5_GROUP // S5_SLAB
    gl = S5_SLAB // S5_GROUP
    dt = jnp.exp(log_step.astype(F32))[:, None]
    ar = a_re.astype(F32)
    ai = a_im.astype(F32)
    mag = jnp.exp(ar * dt)
    lb_re = mag * jnp.cos(ai * dt)
    lb_im = mag * jnp.sin(ai * dt)
    den = ar * ar + ai * ai
    nr = lb_re - 1.0
    coef_re = (nr * ar + lb_im * ai) / den
    coef_im = (lb_im * ar - nr * ai) / den
    bb_re = coef_re[..., None] * b_re - coef_im[..., None] * b_im
    bb_im = coef_re[..., None] * b_im + coef_im[..., None] * b_re
    eye = jnp.eye(gl, dtype=F32)

    def b_slab(t):
        t = t.reshape(n_slab, gl, S5_STATE, S5_GROUP)
        return jnp.einsum("kgpc,gh->kgchp", t, eye).reshape(n_slab, S5_SLAB, gl * S5_STATE)

    def c_slab(t):
        t = t.reshape(n_slab, gl, S5_GROUP, S5_STATE)
        return jnp.einsum("kgcp,gh->kgphc", t, eye).reshape(n_slab, gl * S5_STATE, S5_SLAB)

    b_dense = jnp.concatenate([b_slab(bb_re), b_slab(bb_im)], axis=2).astype(BF16)
    c_dense = jnp.concatenate([c_slab(c_re.astype(F32)), -c_slab(c_im.astype(F32))], axis=1).astype(BF16)

    def flat(t):
        return t.reshape(n_slab, gl * S5_STATE)

    def power(k):
        kk = k.astype(F32)[None, :, None]
        m = jnp.exp(flat(ar * dt)[:, None, :] * kk)
        ph = flat(ai * dt)[:, None, :] * kk
        return jnp.concatenate([m * jnp.cos(ph), m * jnp.sin(ph)], axis=2)

    lam = jnp.repeat(power(jnp.array([1])), SUBLANES, axis=1)
    ptab = jnp.repeat(power(jnp.arange(1, sub + 1)), SUBLANES, axis=1)
    apow = jnp.repeat(power(jnp.array([sub, 2 * sub, 4 * sub])), SUBLANES, axis=1)
    return b_dense, c_dense, lam, ptab, apow


def _s5_mixer(u, a_re, a_im, log_step, b_re, b_im, c_re, c_im, d_skip, sub=S5_SUB):
    seq, d = u.shape
    rows = SUBLANES * sub
    n_slab = d // S5_SLAB
    n2 = 2 * S5_SLAB_STATES
    b_dense, c_dense, lam, ptab, apow = _s5_tables(a_re, a_im, log_step, b_re, b_im, c_re, c_im, sub)
    return pl.pallas_call(
        functools.partial(_s5_kernel, sub=sub),
        out_shape=jax.ShapeDtypeStruct((seq, d), BF16),
        grid=(n_slab // S5_PAIR, seq // rows),
        in_specs=[
            pl.BlockSpec((rows, S5_PAIR * S5_SLAB), lambda k, c: (c, k)),
            pl.BlockSpec((S5_PAIR, S5_SLAB, n2), lambda k, c: (k, 0, 0)),
            pl.BlockSpec((S5_PAIR, n2, S5_SLAB), lambda k, c: (k, 0, 0)),
            pl.BlockSpec((S5_PAIR, SUBLANES, n2), lambda k, c: (k, 0, 0)),
            pl.BlockSpec((S5_PAIR, rows, n2), lambda k, c: (k, 0, 0)),
            pl.BlockSpec((S5_PAIR, 3 * SUBLANES, n2), lambda k, c: (k, 0, 0)),
            pl.BlockSpec((1, S5_PAIR * S5_SLAB), lambda k, c: (0, k)),
        ],
        out_specs=pl.BlockSpec((rows, S5_PAIR * S5_SLAB), lambda k, c: (c, k)),
        scratch_shapes=[pltpu.VMEM((S5_PAIR, rows, n2), F32),
                        pltpu.VMEM((S5_PAIR, SUBLANES, n2), F32),
                        pltpu.VMEM((S5_PAIR, rows, S5_SLAB), F32),
                        pltpu.VMEM((S5_PAIR * S5_SLAB // LANES, rows, LANES), F32)],
        compiler_params=_params("arbitrary", "arbitrary"),
        name="s5_scan",
    )(u, b_dense, c_dense, lam, ptab, apow, d_skip.reshape(1, d).astype(F32))


def _cache_weights(first, pairs):
    @pl.when(first)
    def _():
        for src, dst in pairs:
            dst[...] = src[...].astype(BF16)


def _glu_kernel(a_ref, wa_ref, wb_ref, r_ref, o_ref, wa_s, wb_s):
    _cache_weights(pl.program_id(1) == 0, ((wa_ref, wa_s), (wb_ref, wb_s)))
    a = a_ref[...]
    va = jnp.dot(a, wa_s[...], preferred_element_type=F32)
    vb = jnp.dot(a, wb_s[...], preferred_element_type=F32)
    o_ref[...] = r_ref[...] + va * jax.nn.sigmoid(vb)


def _glu_residual(a, w, res, tm=512, tn=512):
    m, k = a.shape
    n = w.shape[1] // 2
    nb = n // tn
    return pl.pallas_call(
        _glu_kernel,
        out_shape=jax.ShapeDtypeStruct((m, n), F32),
        grid=(nb, m // tm),
        in_specs=[pl.BlockSpec((tm, k), lambda j, i: (i, 0)),
                  pl.BlockSpec((k, tn), lambda j, i: (0, j)),
                  pl.BlockSpec((k, tn), lambda j, i: (0, j + nb)),
                  pl.BlockSpec((tm, tn), lambda j, i: (i, j))],
        out_specs=pl.BlockSpec((tm, tn), lambda j, i: (i, j)),
        scratch_shapes=[pltpu.VMEM((k, tn), BF16), pltpu.VMEM((k, tn), BF16)],
        compiler_params=_params("arbitrary", "arbitrary"),
        name="glu_residual",
    )(a, w, w, res)


def _swiglu_up_kernel(a_ref, wg_ref, wu_ref, o_ref, wg_s, wu_s):
    _cache_weights(pl.program_id(1) == 0, ((wg_ref, wg_s), (wu_ref, wu_s)))
    a = a_ref[...]
    vg = jnp.dot(a, wg_s[...], preferred_element_type=F32)
    vu = jnp.dot(a, wu_s[...], preferred_element_type=F32)
    o_ref[...] = (jax.nn.silu(vg) * vu).astype(o_ref.dtype)


def _swiglu_up(a, w_gu, tm=512, tn=512):
    m, k = a.shape
    f = w_gu.shape[1] // 2
    nb = f // tn
    return pl.pallas_call(
        _swiglu_up_kernel,
        out_shape=jax.ShapeDtypeStruct((m, f), BF16),
        grid=(nb, m // tm),
        in_specs=[pl.BlockSpec((tm, k), lambda j, i: (i, 0)),
                  pl.BlockSpec((k, tn), lambda j, i: (0, j)),
                  pl.BlockSpec((k, tn), lambda j, i: (0, j + nb))],
        out_specs=pl.BlockSpec((tm, tn), lambda j, i: (i, j)),
        scratch_shapes=[pltpu.VMEM((k, tn), BF16), pltpu.VMEM((k, tn), BF16)],
        compiler_params=_params("arbitrary", "arbitrary"),
        name="swiglu_up",
    )(a, w_gu, w_gu)


def _mm_res_kernel(a_ref, w_ref, r_ref, o_ref, w_s):
    _cache_weights(pl.program_id(1) == 0, ((w_ref, w_s),))
    o_ref[...] = r_ref[...] + jnp.dot(a_ref[...], w_s[...], preferred_element_type=F32)


def _matmul_residual(a, w, res, tm=512, tn=512):
    m, k = a.shape
    n = w.shape[1]
    return pl.pallas_call(
        _mm_res_kernel,
        out_shape=jax.ShapeDtypeStruct((m, n), F32),
        grid=(n // tn, m // tm),
        in_specs=[pl.BlockSpec((tm, k), lambda j, i: (i, 0)),
                  pl.BlockSpec((k, tn), lambda j, i: (0, j)),
                  pl.BlockSpec((tm, tn), lambda j, i: (i, j))],
        out_specs=pl.BlockSpec((tm, tn), lambda j, i: (i, j)),
        scratch_shapes=[pltpu.VMEM((k, tn), BF16)],
        compiler_params=_params("arbitrary", "arbitrary"),
        name="matmul_residual",
    )(a, w, res)


def _rope_kernel(pos_ref, inv_ref, cos_ref, sa_ref, sb_ref):
    ang = pos_ref[...].astype(F32) * inv_ref[...]
    c = jnp.cos(ang)
    s = jnp.sin(ang)
    lane = lax.broadcasted_iota(jnp.int32, ang.shape, 1)
    first_half = lane < ROPE_DIM // 2
    cos_ref[...] = c
    sa_ref[...] = jnp.where(first_half, -s, 0.0)
    sb_ref[...] = jnp.where(first_half, 0.0, s)


def _rope_tables(pos, tm=512):
    n = pos.shape[0]
    tm = min(tm, n)
    half = ROPE_DIM // 2
    inv = jnp.power(ROPE_THETA, -jnp.arange(half, dtype=F32) / half)
    inv = jnp.concatenate([inv, inv, jnp.zeros((LANES - ROPE_DIM,), F32)]).reshape(1, LANES)
    spec = pl.BlockSpec((tm, LANES), lambda i: (i, 0))
    return pl.pallas_call(
        _rope_kernel,
        out_shape=[jax.ShapeDtypeStruct((n, LANES), F32)] * 3,
        grid=(n // tm,),
        in_specs=[pl.BlockSpec((tm, 1), lambda i: (i, 0)),
                  pl.BlockSpec((1, LANES), lambda i: (0, 0))],
        out_specs=[spec, spec, spec],
        compiler_params=_params("arbitrary"),
        name="rope_tables",
    )(pos.reshape(n, 1), inv)


def _rope(y, c, sa, sb):
    half = ROPE_DIM // 2
    return y * c + pltpu.roll(y, LANES - half, 1) * sa + pltpu.roll(y, half, 1) * sb


def _head_norm(x, gain):
    return x * lax.rsqrt(jnp.mean(x * x, axis=-1, keepdims=True) + RMS_EPS) * gain


def _nsa_proj_kernel(a_ref, w_ref, gain_ref, cos_ref, sa_ref, sb_ref, o_ref, w_s, *, norm_tiles):
    j = pl.program_id(0)
    _cache_weights(pl.program_id(1) == 0, ((w_ref, w_s),))
    acc = jnp.dot(a_ref[...], w_s[...], preferred_element_type=F32)
    is_norm = functools.reduce(jnp.logical_or, [j == t for t in norm_tiles])

    @pl.when(is_norm)
    def _():
        c, sa, sb = cos_ref[...], sa_ref[...], sb_ref[...]
        gain = gain_ref[0]
        for hh in range(acc.shape[1] // HEAD_DIM):
            sl = slice(HEAD_DIM * hh, HEAD_DIM * (hh + 1))
            o_ref[:, sl] = _rope(_head_norm(acc[:, sl], gain), c, sa, sb).astype(o_ref.dtype)

    @pl.when(jnp.logical_not(is_norm))
    def _():
        o_ref[...] = acc.astype(o_ref.dtype)


def _nsa_proj(u, w_in, q_gain, k_gain, rope, tm=512, tn=512):
    m, k = u.shape
    q_dim = GQA_GROUP * N_KV_HEADS * HEAD_DIM
    kv_dim = N_KV_HEADS * HEAD_DIM
    assert kv_dim == tn
    n_q = q_dim // tn
    n_tiles = n_q + 6
    ones = jnp.ones((HEAD_DIM,), F32)
    q_scaled = q_gain.astype(F32) * (HEAD_DIM ** -0.5 * math.log2(math.e))
    gains = jnp.stack([q_scaled] * n_q + [ones, ones, k_gain[1], ones, k_gain[2], ones]).reshape(n_tiles, 1, HEAD_DIM)
    norm_tiles = tuple(range(n_q)) + (n_q + 2, n_q + 4)
    tab = pl.BlockSpec((tm, LANES), lambda j, i: (i, 0))
    return pl.pallas_call(
        functools.partial(_nsa_proj_kernel, norm_tiles=norm_tiles),
        out_shape=jax.ShapeDtypeStruct((m, n_tiles * tn), BF16),
        grid=(n_tiles, m // tm),
        in_specs=[pl.BlockSpec((tm, k), lambda j, i: (i, 0)),
                  pl.BlockSpec((k, tn), lambda j, i: (0, j)),
                  pl.BlockSpec((1, 1, HEAD_DIM), lambda j, i: (j, 0, 0)),
                  tab, tab, tab],
        out_specs=pl.BlockSpec((tm, tn), lambda j, i: (i, j)),
        scratch_shapes=[pltpu.VMEM((k, tn), BF16)],
        compiler_params=_params("arbitrary", "arbitrary"),
        name="nsa_proj",
    )(u, w_in, gains, *rope)


def _gate_kernel(a_ref, w_ref, o_ref):
    o_ref[...] = jax.nn.sigmoid(jnp.dot(a_ref[...], w_ref[...].astype(BF16), preferred_element_type=F32))


def _nsa_gates(u, w_gate, tm=512):
    m, k = u.shape
    n = w_gate.shape[1]
    w_pad = jnp.pad(w_gate, ((0, 0), (0, LANES - n)))
    return pl.pallas_call(
        _gate_kernel,
        out_shape=jax.ShapeDtypeStruct((m, LANES), F32),
        grid=(m // tm,),
        in_specs=[pl.BlockSpec((tm, k), lambda i: (i, 0)),
                  pl.BlockSpec((k, LANES), lambda i: (0, 0))],
        out_specs=pl.BlockSpec((tm, LANES), lambda i: (i, 0)),
        compiler_params=_params("arbitrary"),
        name="nsa_gates",
    )(u, w_pad)


def _compress_kernel(*refs, is_key):
    if is_key:
        ca_ref, cb_ref, pe_ref, w1_ref, w2_ref, gain_ref, cos_ref, sa_ref, sb_ref, o_ref = refs
    else:
        ca_ref, cb_ref, pe_ref, w1_ref, w2_ref, o_ref = refs
    half = w1_ref.shape[0] // 2
    pe = pe_ref[...]
    xa = (ca_ref[0].astype(F32) + pe[:, :half]).astype(BF16)
    xb = (cb_ref[0].astype(F32) + pe[:, half:]).astype(BF16)
    hid = (jnp.dot(xa, w1_ref[:half, :].astype(BF16), preferred_element_type=F32)
           + jnp.dot(xb, w1_ref[half:, :].astype(BF16), preferred_element_type=F32))
    out = jnp.dot(jax.nn.gelu(hid).astype(BF16), w2_ref[...].astype(BF16), preferred_element_type=F32)
    if is_key:
        out = _rope(_head_norm(out, gain_ref[...]), cos_ref[...], sa_ref[...], sb_ref[...])
    o_ref[0] = out.astype(o_ref.dtype)


def _compress(t, pe, w1, w2, key_extras=None):
    seq = t.shape[0]
    nc = seq // CMP_STRIDE
    width = CMP_STRIDE * HEAD_DIM
    ca = t.reshape(nc, CMP_STRIDE, N_KV_HEADS, HEAD_DIM).transpose(2, 0, 1, 3).reshape(N_KV_HEADS, nc, width)
    cb = jnp.concatenate([ca[:, 1:], jnp.zeros((N_KV_HEADS, 1, width), ca.dtype)], axis=1)
    blk = pl.BlockSpec((1, nc, width), lambda h: (h, 0, 0))
    full = lambda a: pl.BlockSpec(a.shape, lambda h: (0,) * a.ndim)
    args = [ca, cb, pe.reshape(1, CMP_BLOCK * HEAD_DIM), w1, w2]
    if key_extras is not None:
        args += list(key_extras)
    return pl.pallas_call(
        functools.partial(_compress_kernel, is_key=key_extras is not None),
        out_shape=jax.ShapeDtypeStruct((N_KV_HEADS, nc, HEAD_DIM), BF16),
        grid=(N_KV_HEADS,),
        in_specs=[blk, blk] + [full(a) for a in args[2:]],
        out_specs=pl.BlockSpec((1, nc, HEAD_DIM), lambda h: (h, 0, 0)),
        compiler_params=_params("arbitrary"),
        name="nsa_compress_k" if key_extras is not None else "nsa_compress_v",
    )(*args)


def _dot_nt(a, b):
    return lax.dot_general(a, b, (((1,), (1,)), ((), ())), preferred_element_type=F32)


def _split3(x):
    hi = x.astype(BF16)
    r1 = x - hi.astype(F32)
    mid = r1.astype(BF16)
    lo = (r1 - mid.astype(F32)).astype(BF16)
    return hi, mid, lo


def _nsa_attn_kernel(q_ref, kc_ref, vct_ref, ks_ref, vst_ref, kw_ref, vwt_ref, gate_ref, blk_ref, o_ref,
                     acc_ref, mix_ref, m_ref, l_ref, qa_ref, s0_ref, s1_ref, sc_ref, sw_ref, *, seq, tk):
    t0 = pl.program_id(1) * Q_BLOCK
    nc = kc_ref.shape[1]
    ns = seq // SEL_BLOCK
    grp = GQA_GROUP
    cols = grp * Q_BLOCK
    sel_shift = int(math.log2(SEL_BLOCK))
    q = q_ref[...].astype(F32)
    qt = jnp.concatenate([q[:, HEAD_DIM * g:HEAD_DIM * (g + 1)].T for g in range(grp)], axis=1).astype(BF16)
    t_row = t0 + lax.broadcasted_iota(jnp.int32, (1, Q_BLOCK), 1)

    def heads(x):
        return jnp.concatenate([x] * grp, axis=1)

    def softmax_cols(s, ok):
        sb = s + heads(jnp.where(ok, 0.0, NEG))
        m = jnp.max(sb, axis=0, keepdims=True)
        e = jnp.exp2(sb - m)
        return e, m, jnp.maximum(jnp.sum(e, axis=0, keepdims=True), 1e-30)

    span = WINDOW + Q_BLOCK
    w0 = pl.multiple_of(jnp.maximum(t0 - WINDOW, 0), Q_BLOCK)
    sc_ref[...] = jnp.dot(kc_ref[0], qt, preferred_element_type=F32)
    sw_ref[...] = jnp.dot(kw_ref[pl.ds(w0, span), :], qt, preferred_element_type=F32)
    gate = gate_ref[0, 0]

    n_idx = lax.broadcasted_iota(jnp.int32, (nc, Q_BLOCK), 0)
    ok_c = (n_idx * CMP_STRIDE + (CMP_BLOCK - 1) <= t_row) & (n_idx < nc - 1)
    e_c, m_c, den_c = softmax_cols(sc_ref[...], ok_c)
    p_c = e_c * jnp.where(m_c > 0.5 * NEG, 1.0 / den_c, 0.0)
    mix_ref[...] = gate[0:1] * jnp.dot(vct_ref[0], p_c.astype(BF16), preferred_element_type=F32)
    imp = p_c[:, :Q_BLOCK]
    for g in range(1, grp):
        imp = imp + p_c[:, Q_BLOCK * g:Q_BLOCK * (g + 1)]

    rel = t_row - (w0 + lax.broadcasted_iota(jnp.int32, (span, Q_BLOCK), 0))
    e_w, _, den_w = softmax_cols(sw_ref[...], (rel >= 0) & (rel < WINDOW))
    mix_ref[...] += gate[2:3] * jnp.dot(vwt_ref[0, :, pl.ds(w0, span)], (e_w * (1.0 / den_w)).astype(BF16),
                                        preferred_element_type=F32)

    ratio = SEL_BLOCK // CMP_STRIDE
    d = (lax.broadcasted_iota(jnp.int32, (ns, nc), 1)
         - ratio * lax.broadcasted_iota(jnp.int32, (ns, nc), 0))
    overlap = jnp.zeros((ns, nc), F32)
    for n in range(CMP_BLOCK // CMP_STRIDE):
        overlap = overlap + jnp.where((d - n >= 0) & (d - n < ratio), 1.0, 0.0)
    overlap = overlap.astype(BF16)
    p_slc = sum(jnp.dot(overlap, part, preferred_element_type=F32) for part in _split3(imp))

    j_idx = lax.broadcasted_iota(jnp.int32, (ns, Q_BLOCK), 0)
    j_f = j_idx.astype(F32)
    dist = jnp.right_shift(t_row, sel_shift) - j_idx
    forced = (j_idx == 0) | ((dist >= 0) & (dist < SEL_LOCAL))
    score = jnp.where(forced, jnp.inf, jnp.where(dist >= 0, p_slc, -jnp.inf))
    sel = jnp.zeros((ns, Q_BLOCK), F32)
    for _ in range(min(SEL_TOPK, ns)):
        top = jnp.max(score, axis=0, keepdims=True)
        idx = jnp.min(jnp.where(score == top, j_f, float(ns)), axis=0, keepdims=True)
        pick = j_f == idx
        sel = jnp.where(pick, 1.0, sel)
        score = jnp.where(pick, -jnp.inf, score)

    qa_ref[...] = jnp.concatenate([qt, heads(jnp.where(sel > 0.0, 0.0, NEG).astype(BF16))], axis=0)
    acc_ref[...] = jnp.zeros_like(acc_ref)
    m_ref[...] = jnp.full_like(m_ref, NEG)
    l_ref[...] = jnp.zeros_like(l_ref)

    def score_tile(kt, dst):
        k0 = pl.multiple_of(kt * tk, tk)
        k_aug = jnp.concatenate([ks_ref[pl.ds(k0, tk), :], blk_ref[pl.ds(k0, tk), :]], axis=1)
        dst[...] = jnp.dot(k_aug, qa_ref[...], preferred_element_type=F32)

    def consume_tile(kt, src, diagonal):
        k0 = pl.multiple_of(kt * tk, tk)
        sb = src[...]
        if diagonal:
            kpos = k0 + lax.broadcasted_iota(jnp.int32, (tk, Q_BLOCK), 0)
            sb = sb + heads(jnp.where(kpos <= t_row, 0.0, NEG))
        m_i = m_ref[...]
        m_new = jnp.maximum(m_i, jnp.max(sb, axis=0, keepdims=True))
        e = jnp.exp2(sb - m_new)
        alpha = jnp.exp2(m_i - m_new)
        m_ref[...] = m_new
        l_ref[...] = alpha * l_ref[...] + jnp.sum(e, axis=0, keepdims=True)
        acc_ref[...] = alpha * acc_ref[...] + jnp.dot(vst_ref[0, :, pl.ds(k0, tk)], e.astype(BF16),
                                                      preferred_element_type=F32)

    last = t0 // tk
    score_tile(0, s0_ref)

    def tile_pair(i, _):
        score_tile(2 * i + 1, s1_ref)
        consume_tile(2 * i, s0_ref, False)
        score_tile(2 * i + 2, s0_ref)
        consume_tile(2 * i + 1, s1_ref, False)
        return 0

    lax.fori_loop(0, last // 2, tile_pair, 0)

    @pl.when(last % 2 == 1)
    def _():
        score_tile(last, s1_ref)
        consume_tile(last - 1, s0_ref, False)
        consume_tile(last, s1_ref, True)

    @pl.when(last % 2 == 0)
    def _():
        consume_tile(last, s0_ref, True)

    o_s = acc_ref[...] * (1.0 / jnp.maximum(l_ref[...], 1e-30))

    mixed = mix_ref[...] + gate_ref[0, 0, 1:2] * o_s
    for g in range(grp):
        o_ref[:, HEAD_DIM * g:HEAD_DIM * (g + 1)] = mixed[:, Q_BLOCK * g:Q_BLOCK * (g + 1)].T.astype(o_ref.dtype)


def _nsa_attention(proj, kcmp, vcmp_t, vsl_t, vw_t, gates, tk=512):
    seq = proj.shape[0]
    tk = min(tk, seq)
    q_dim = GQA_GROUP * N_KV_HEADS * HEAD_DIM
    kv_blocks = N_KV_HEADS
    first = q_dim // HEAD_DIM + 2 * kv_blocks
    nc = kcmp.shape[1]
    cols = GQA_GROUP * Q_BLOCK

    def k_spec(which):
        return pl.BlockSpec((seq, HEAD_DIM), lambda h, qb: (0, first + which * kv_blocks + h))

    def vt_spec(n):
        return pl.BlockSpec((1, HEAD_DIM, n), lambda h, qb: (h, 0, 0))

    q_spec = pl.BlockSpec((Q_BLOCK, GQA_GROUP * HEAD_DIM), lambda h, qb: (qb, h))
    ns = seq // SEL_BLOCK
    key_block = (jnp.arange(seq, dtype=jnp.int32)[:, None] // SEL_BLOCK
                 == jnp.arange(ns, dtype=jnp.int32)[None, :]).astype(BF16)
    return pl.pallas_call(
        functools.partial(_nsa_attn_kernel, seq=seq, tk=tk),
        out_shape=jax.ShapeDtypeStruct((seq, q_dim), BF16),
        grid=(N_KV_HEADS, seq // Q_BLOCK),
        in_specs=[q_spec, pl.BlockSpec((1, nc, HEAD_DIM), lambda h, qb: (h, 0, 0)), vt_spec(nc),
                  k_spec(0), vt_spec(seq), k_spec(2), vt_spec(seq),
                  pl.BlockSpec((1, 1, 3, cols), lambda h, qb: (h, qb, 0, 0)),
                  pl.BlockSpec((seq, ns), lambda h, qb: (0, 0))],
        out_specs=q_spec,
        scratch_shapes=[pltpu.VMEM((HEAD_DIM, cols), F32), pltpu.VMEM((HEAD_DIM, cols), F32),
                        pltpu.VMEM((1, cols), F32), pltpu.VMEM((1, cols), F32),
                        pltpu.VMEM((HEAD_DIM + ns, cols), BF16),
                        pltpu.VMEM((tk, cols), F32), pltpu.VMEM((tk, cols), F32),
                        pltpu.VMEM((nc, cols), F32), pltpu.VMEM((WINDOW + Q_BLOCK, cols), F32)],
        compiler_params=_params("arbitrary", "arbitrary"),
        name="nsa_attention",
    )(proj, kcmp, vcmp_t, proj, vsl_t, proj, vw_t, gates, key_block)


def _nsa_mixer(u, positions, w_in, q_gain, k_gain, pe_k, pe_v, ck_w1, ck_w2, cv_w1, cv_w2):
    seq = u.shape[0]
    q_dim = GQA_GROUP * N_KV_HEADS * HEAD_DIM
    kv_dim = N_KV_HEADS * HEAD_DIM
    n_main = q_dim + 6 * kv_dim
    nc = seq // CMP_STRIDE
    rope = _rope_tables(positions)
    proj = _nsa_proj(u, w_in, q_gain, k_gain, rope)
    gate = _nsa_gates(u, w_in[:, n_main:])
    gates = (gate[:, :3 * N_KV_HEADS * GQA_GROUP].reshape(seq // Q_BLOCK, Q_BLOCK, 3, N_KV_HEADS, GQA_GROUP)
             .transpose(3, 0, 2, 4, 1).reshape(N_KV_HEADS, seq // Q_BLOCK, 3, GQA_GROUP * Q_BLOCK))

    def keys_last(cols):
        return cols.reshape(seq, N_KV_HEADS, HEAD_DIM).transpose(1, 2, 0)

    pos_cmp = jnp.concatenate([positions[CMP_BLOCK - 1::CMP_STRIDE][:nc - 1], jnp.zeros((1,), positions.dtype)])
    rope_cmp = _rope_tables(pos_cmp)
    kcmp = _compress(proj[:, q_dim:q_dim + kv_dim], pe_k, ck_w1, ck_w2,
                     key_extras=(k_gain[0].reshape(1, HEAD_DIM),) + tuple(rope_cmp))
    vcmp = _compress(proj[:, q_dim + kv_dim:q_dim + 2 * kv_dim], pe_v, cv_w1, cv_w2)
    vsl_t = keys_last(proj[:, q_dim + 3 * kv_dim:q_dim + 4 * kv_dim])
    vw_t = keys_last(proj[:, q_dim + 5 * kv_dim:q_dim + 6 * kv_dim])
    return _nsa_attention(proj, kcmp, vcmp.transpose(0, 2, 1), vsl_t, vw_t, gates)


def _router_kernel(x_ref, g_ref, w_ref, b_ref, u_ref, r_ref):
    x = x_ref[...]
    u = x * lax.rsqrt(jnp.mean(x * x, axis=-1, keepdims=True) + RMS_EPS) * g_ref[...]
    u_ref[...] = u
    uh, um, _ = _split3(u)
    wh, wm, _ = _split3(w_ref[...])
    logits = (jnp.dot(uh, wh, preferred_element_type=F32) + jnp.dot(uh, wm, preferred_element_type=F32)
              + jnp.dot(um, wh, preferred_element_type=F32)) + b_ref[...]
    lane = lax.broadcasted_iota(jnp.int32, logits.shape, 1).astype(F32)
    lg = jnp.where(lane < N_EXPERTS, logits, -jnp.inf)
    v1 = jnp.max(lg, axis=-1, keepdims=True)
    i1 = jnp.min(jnp.where(lg == v1, lane, float(LANES)), axis=-1, keepdims=True)
    lg = jnp.where(lane == i1, -jnp.inf, lg)
    v2 = jnp.max(lg, axis=-1, keepdims=True)
    i2 = jnp.min(jnp.where(lg == v2, lane, float(LANES)), axis=-1, keepdims=True)
    e2 = jnp.exp(v2 - v1)
    den = 1.0 + e2
    r_ref[...] = jnp.where(lane == 0, i1, jnp.where(lane == 1, i2, jnp.where(
        lane == 2, 1.0 / den, jnp.where(lane == 3, e2 / den, 0.0))))


def _router(h, gain, w_router, b_router, tm=256):
    m, d = h.shape
    w_pad = jnp.pad(w_router.astype(F32), ((0, 0), (0, LANES - N_EXPERTS)))
    b_pad = jnp.pad(b_router.astype(F32), (0, LANES - N_EXPERTS)).reshape(1, LANES)
    return pl.pallas_call(
        _router_kernel,
        out_shape=[jax.ShapeDtypeStruct((m, d), F32), jax.ShapeDtypeStruct((m, LANES), F32)],
        grid=(m // tm,),
        in_specs=[pl.BlockSpec((tm, d), lambda i: (i, 0)),
                  pl.BlockSpec((1, d), lambda i: (0, 0)),
                  pl.BlockSpec((d, LANES), lambda i: (0, 0)),
                  pl.BlockSpec((1, LANES), lambda i: (0, 0))],
        out_specs=[pl.BlockSpec((tm, d), lambda i: (i, 0)), pl.BlockSpec((tm, LANES), lambda i: (i, 0))],
        compiler_params=_params("arbitrary"),
        name="moe_router",
    )(h, gain.reshape(1, d), w_pad, b_pad)


def _row_copy(src_hbm, row, dst, r, sem):
    return pltpu.make_async_copy(src_hbm.at[pl.ds(row, 1), :], dst.at[pl.ds(r, 1), :], sem)


def _gather_kernel(idx_ref, used_ref, src_hbm, o_ref, buf, sem):
    rows = o_ref.shape[0]
    i = pl.program_id(0)
    n_used = used_ref[0]

    def issue(blk):
        slot = blk % 2

        def start(r, _):
            _row_copy(src_hbm, idx_ref[blk * rows + r], buf.at[slot], r, sem.at[slot]).start()
            return 0

        lax.fori_loop(0, rows, start, 0, unroll=DMA_UNROLL)

    @pl.when(i == 0)
    def _():
        issue(i)

    @pl.when(i + 1 < n_used)
    def _():
        issue(i + 1)

    @pl.when(i < n_used)
    def _():
        slot = i % 2

        def wait(r, _):
            _row_copy(src_hbm, 0, buf.at[slot], r, sem.at[slot]).wait()
            return 0

        lax.fori_loop(0, rows, wait, 0, unroll=DMA_UNROLL)
        o_ref[...] = buf[slot].astype(o_ref.dtype)

    @pl.when(i >= n_used)
    def _():
        o_ref[...] = jnp.zeros_like(o_ref)


def _gather_rows(src, idx, n_used, out_dtype, rows=MOE_ROWS):
    n = idx.shape[0]
    d = src.shape[1]
    return pl.pallas_call(
        _gather_kernel,
        out_shape=jax.ShapeDtypeStruct((n, d), out_dtype),
        grid_spec=pltpu.PrefetchScalarGridSpec(
            num_scalar_prefetch=2,
            grid=(n // rows,),
            in_specs=[pl.BlockSpec(memory_space=pl.ANY)],
            out_specs=pl.BlockSpec((rows, d), lambda i, idx, used: (i, 0)),
            scratch_shapes=[pltpu.VMEM((2, rows, d), src.dtype), pltpu.SemaphoreType.DMA((2,))]),
        compiler_params=_params("arbitrary"),
        name="moe_gather",
    )(idx, n_used, src)


def _block_state(be_ref, used_ref, i):
    changed = (i == 0) | (be_ref[i] != be_ref[jnp.maximum(i - 1, 0)])
    used = i < used_ref[0]
    return used, used & changed


def _last_used(i, used):
    return jnp.minimum(i, used[0] - 1)


def _moe_up_kernel(be_ref, used_ref, x_ref, wg_ref, wu_ref, o_ref, wg_s, wu_s):
    used, first = _block_state(be_ref, used_ref, pl.program_id(1))

    @pl.when(first)
    def _():
        wg_s[...] = wg_ref[0].astype(BF16)
        wu_s[...] = wu_ref[0].astype(BF16)

    @pl.when(used)
    def _():
        a = x_ref[...]
        vg = jnp.dot(a, wg_s[...], preferred_element_type=F32)
        vu = jnp.dot(a, wu_s[...], preferred_element_type=F32)
        o_ref[...] = (jax.nn.silu(vg) * vu).astype(o_ref.dtype)

    @pl.when(jnp.logical_not(used))
    def _():
        o_ref[...] = jnp.zeros_like(o_ref)


def _moe_up(x_rows, blk_e, n_used, w_gu, tn=512, rows=MOE_ROWS):
    n, k = x_rows.shape
    f = w_gu.shape[2] // 2
    nb = f // tn
    return pl.pallas_call(
        _moe_up_kernel,
        out_shape=jax.ShapeDtypeStruct((n, f), BF16),
        grid_spec=pltpu.PrefetchScalarGridSpec(
            num_scalar_prefetch=2,
            grid=(nb, n // rows),
            in_specs=[pl.BlockSpec((rows, k), lambda j, i, be, nu: (_last_used(i, nu), 0)),
                      pl.BlockSpec((1, k, tn), lambda j, i, be, nu: (be[_last_used(i, nu)], 0, j)),
                      pl.BlockSpec((1, k, tn), lambda j, i, be, nu: (be[_last_used(i, nu)], 0, j + nb))],
            out_specs=pl.BlockSpec((rows, tn), lambda j, i, be, nu: (i, j)),
            scratch_shapes=[pltpu.VMEM((k, tn), BF16), pltpu.VMEM((k, tn), BF16)]),
        compiler_params=_params("arbitrary", "arbitrary"),
        name="moe_up",
    )(blk_e, n_used, x_rows, w_gu, w_gu)


def _moe_down_kernel(be_ref, used_ref, a_ref, w_ref, o_ref, w_s):
    used, first = _block_state(be_ref, used_ref, pl.program_id(1))

    @pl.when(first)
    def _():
        w_s[...] = w_ref[0].astype(BF16)

    @pl.when(used)
    def _():
        o_ref[...] = jnp.dot(a_ref[...], w_s[...], preferred_element_type=F32)

    @pl.when(jnp.logical_not(used))
    def _():
        o_ref[...] = jnp.zeros_like(o_ref)


def _moe_down(act, blk_e, n_used, w_down, tn=512, rows=MOE_ROWS):
    n, k = act.shape
    d = w_down.shape[2]
    return pl.pallas_call(
        _moe_down_kernel,
        out_shape=jax.ShapeDtypeStruct((n, d), F32),
        grid_spec=pltpu.PrefetchScalarGridSpec(
            num_scalar_prefetch=2,
            grid=(d // tn, n // rows),
            in_specs=[pl.BlockSpec((rows, k), lambda j, i, be, nu: (_last_used(i, nu), 0)),
                      pl.BlockSpec((1, k, tn), lambda j, i, be, nu: (be[_last_used(i, nu)], 0, j))],
            out_specs=pl.BlockSpec((rows, tn), lambda j, i, be, nu: (i, j)),
            scratch_shapes=[pltpu.VMEM((k, tn), BF16)]),
        compiler_params=_params("arbitrary", "arbitrary"),
        name="moe_down",
    )(blk_e, n_used, act, w_down)


def _combine_kernel(dest_ref, h_ref, r_ref, rows_hbm, o_ref, buf, sem):
    tm = h_ref.shape[0]
    base = pl.program_id(0) * tm

    def start(r, _):
        for k in range(2):
            _row_copy(rows_hbm, dest_ref[2 * (base + r) + k], buf.at[k], r, sem.at[k]).start()
        return 0

    def wait(r, _):
        for k in range(2):
            _row_copy(rows_hbm, 0, buf.at[k], r, sem.at[k]).wait()
        return 0

    lax.fori_loop(0, tm, start, 0, unroll=DMA_UNROLL)
    lax.fori_loop(0, tm, wait, 0, unroll=DMA_UNROLL)
    w = r_ref[...]
    o_ref[...] = h_ref[...] + (w[:, 2:3] * buf[0] + w[:, 3:4] * buf[1])


def _moe_combine(h, route, out_rows, dest, tm=256):
    m, d = h.shape
    return pl.pallas_call(
        _combine_kernel,
        out_shape=jax.ShapeDtypeStruct((m, d), F32),
        grid_spec=pltpu.PrefetchScalarGridSpec(
            num_scalar_prefetch=1,
            grid=(m // tm,),
            in_specs=[pl.BlockSpec((tm, d), lambda i, dest: (i, 0)),
                      pl.BlockSpec((tm, LANES), lambda i, dest: (i, 0)),
                      pl.BlockSpec(memory_space=pl.ANY)],
            out_specs=pl.BlockSpec((tm, d), lambda i, dest: (i, 0)),
            scratch_shapes=[pltpu.VMEM((2, tm, d), F32), pltpu.SemaphoreType.DMA((2,))]),
        compiler_params=_params("arbitrary"),
        name="moe_combine",
    )(dest.reshape(-1), h, route, out_rows)


def _moe_layout(top_e, rows=MOE_ROWS):
    n_tok = top_e.shape[0]
    e_flat = top_e.reshape(-1)
    onehot = (e_flat[:, None] == jnp.arange(N_EXPERTS, dtype=jnp.int32)[None, :]).astype(jnp.int32)
    csum = jnp.cumsum(onehot, axis=0)
    rank = jnp.take_along_axis(csum, e_flat[:, None], axis=1)[:, 0] - 1
    counts = csum[-1]
    padded = (counts + rows - 1) // rows * rows
    pad_end = jnp.cumsum(padded)
    dest = (pad_end - padded)[e_flat] + rank
    n_rows = e_flat.shape[0] + N_EXPERTS * rows
    t_flat = jnp.repeat(jnp.arange(n_tok, dtype=jnp.int32), top_e.shape[1])
    row_tok = jnp.zeros((n_rows,), jnp.int32).at[dest].set(t_flat)
    n_blk = n_rows // rows
    blk_start = jnp.arange(n_blk, dtype=jnp.int32) * rows
    blk_e = jnp.minimum(jnp.sum(blk_start[:, None] >= pad_end[None, :], axis=1), N_EXPERTS - 1).astype(jnp.int32)
    n_used = (pad_end[-1:] // rows).astype(jnp.int32)
    return row_tok, blk_e, n_used, dest.astype(jnp.int32).reshape(n_tok, -1)


def _moe_ffn_residual(h, gain, w_router, b_router, w_gu, w_down):
    u, route = _router(h, gain, w_router, b_router)
    top_e = route[:, :2].astype(jnp.int32)
    row_tok, blk_e, n_used, dest = _moe_layout(top_e)
    x_rows = _gather_rows(u, row_tok, n_used, BF16)
    act = _moe_up(x_rows, blk_e, n_used, w_gu)
    out_rows = _moe_down(act, blk_e, n_used, w_down)
    return _moe_combine(h, route, out_rows, dest)


def kernel(x, positions, norm_mix, norm_ffn, s5_a_re, s5_a_im, s5_log_step, s5_b_re, s5_b_im, s5_c_re, s5_c_im, s5_d, s5_w_glu, nsa_w_in, nsa_q_gain, nsa_k_gain, nsa_pe_k, nsa_pe_v, nsa_ck_w1, nsa_ck_w2, nsa_cv_w1, nsa_cv_w2, nsa_w_out, ffn_w_gu, ffn_w_down, moe_w_router, moe_b_router, moe_w_gu, moe_w_down):
    bsz, seq, d = x.shape
    assert bsz == 1, "the scan and attention kernels take one sequence"
    h = x.reshape(seq, d)
    h = _layer_s5(h, norm_mix[0], norm_ffn[0], s5_a_re[0], s5_a_im[0], s5_log_step[0], s5_b_re[0],
                  s5_b_im[0], s5_c_re[0], s5_c_im[0], s5_d[0], s5_w_glu[0], ffn_w_gu[0], ffn_w_down[0])
    h = _layer_nsa(h, positions[0], norm_mix[1], norm_ffn[1], nsa_w_in[0], nsa_q_gain[0], nsa_k_gain[0],
                   nsa_pe_k[0], nsa_pe_v[0], nsa_ck_w1[0], nsa_ck_w2[0], nsa_cv_w1[0], nsa_cv_w2[0],
                   nsa_w_out[0], moe_w_router[0], moe_b_router[0], moe_w_gu[0], moe_w_down[0])
    return h.reshape(bsz, seq, d)


def _layer_nsa(h, positions, g_mix, g_ffn, w_in, q_gain, k_gain, pe_k, pe_v, ck_w1, ck_w2, cv_w1, cv_w2,
               w_out, w_router, b_router, w_gu, w_down):
    u = _rms_norm(h, g_mix, BF16)
    o = _nsa_mixer(u, positions, w_in, q_gain, k_gain, pe_k, pe_v, ck_w1, ck_w2, cv_w1, cv_w2)
    h = _matmul_residual(o, w_out, h)
    return _moe_ffn_residual(h, g_ffn, w_router, b_router, w_gu, w_down)


def _layer_s5(h, g_mix, g_ffn, a_re, a_im, log_step, b_re, b_im, c_re, c_im, d_skip, w_glu, w_gu, w_down):
    u = _rms_norm(h, g_mix, F32)
    g = _s5_mixer(u, a_re, a_im, log_step, b_re, b_im, c_re, c_im, d_skip)
    h = _glu_residual(g, w_glu, h)
    u = _rms_norm(h, g_ffn, BF16)
    act = _swiglu_up(u, w_gu)
    return _matmul_residual(act, w_down, h)
```

```python
import functools
import math

import jax
import jax.numpy as jnp
from jax import lax
from jax.experimental import pallas as pl
from jax.experimental.pallas import tpu as pltpu

F32 = jnp.float32
BF16 = jnp.bfloat16

RMS_EPS = 1e-6
S5_GROUP = 16
S5_STATE = 64
HEAD_DIM = 128
N_KV_HEADS = 4
GQA_GROUP = 4
ROPE_DIM = 32
ROPE_THETA = 500000.0
CMP_BLOCK = 32
CMP_STRIDE = 16
SEL_BLOCK = 64
SEL_TOPK = 16
SEL_LOCAL = 2
WINDOW = 512
Q_BLOCK = 128
N_EXPERTS = 8
NEG = -1e30

LANES = 128
SUBLANES = 8
VMEM_LIMIT = 56 * 1024 * 1024

S5_SLAB = 256
S5_SLAB_STATES = S5_SLAB // S5_GROUP * S5_STATE
S5_SUB = 64
S5_PAIR = 2
MOE_ROWS = 512
DMA_UNROLL = 8


def _params(*sem):
    return pltpu.CompilerParams(dimension_semantics=sem, vmem_limit_bytes=VMEM_LIMIT)


def _rms_kernel(x_ref, g_ref, o_ref):
    x = x_ref[...]
    ms = jnp.mean(x * x, axis=-1, keepdims=True)
    o_ref[...] = (x * lax.rsqrt(ms + RMS_EPS) * g_ref[...]).astype(o_ref.dtype)


def _rms_norm(x, gain, out_dtype, tm=512):
    m, d = x.shape
    return pl.pallas_call(
        _rms_kernel,
        out_shape=jax.ShapeDtypeStruct((m, d), out_dtype),
        grid=(m // tm,),
        in_specs=[pl.BlockSpec((tm, d), lambda i: (i, 0)),
                  pl.BlockSpec((1, d), lambda i: (0, 0))],
        out_specs=pl.BlockSpec((tm, d), lambda i: (i, 0)),
        compiler_params=_params("arbitrary"),
        name="rms_norm",
    )(x, gain.reshape(1, d))


def _s5_kernel(u_ref, b_ref, c_ref, lam_ref, ptab_ref, apow_ref, d_ref, o_ref,
               xs_ref, carry_ref, up_ref, us_ref, *, sub):
    n = S5_SLAB_STATES
    halves = S5_SLAB // LANES
    row = lax.broadcasted_iota(jnp.int32, (SUBLANES, n), 0)

    @pl.when(pl.program_id(1) == 0)
    def _():
        carry_ref[...] = jnp.zeros_like(carry_ref)

    def in_proj(s):
        for c in range(halves):
            us_ref[s * halves + c] = u_ref[:, S5_SLAB * s + LANES * c:S5_SLAB * s + LANES * (c + 1)]
        for i in range(sub):
            for c in range(halves):
                up_ref[s, SUBLANES * i:SUBLANES * (i + 1), LANES * c:LANES * (c + 1)] = (
                    us_ref[s * halves + c, pl.ds(i, SUBLANES, stride=sub), :])
        xs_ref[s] = jnp.dot(up_ref[s].astype(BF16), b_ref[s], preferred_element_type=F32)

    def scan(s):
        lam = lam_ref[s]
        lr, li = lam[:, :n], lam[:, n:]
        hr = hi = jnp.zeros((SUBLANES, n), F32)
        for i in range(sub):
            rows = slice(SUBLANES * i, SUBLANES * (i + 1))
            x = xs_ref[s, rows, :]
            hr, hi = lr * hr - li * hi + x[:, :n], lr * hi + li * hr + x[:, n:]
            xs_ref[s, rows, :] = jnp.concatenate([hr, hi], axis=1)
        er, ei = hr, hi

        cin = carry_ref[s]
        zr = jnp.where(row == 0, cin[:, :n], pltpu.roll(er, 1, 0))
        zi = jnp.where(row == 0, cin[:, n:], pltpu.roll(ei, 1, 0))
        apow = apow_ref[s]
        for k, d in enumerate((1, 2, 4)):
            ar = apow[SUBLANES * k:SUBLANES * (k + 1), :n]
            ai = apow[SUBLANES * k:SUBLANES * (k + 1), n:]
            sr = pltpu.roll(zr, d, 0)
            si = pltpu.roll(zi, d, 0)
            keep = row >= d
            zr, zi = (zr + jnp.where(keep, ar * sr - ai * si, 0.0),
                      zi + jnp.where(keep, ar * si + ai * sr, 0.0))
        a1r, a1i = apow[:SUBLANES, :n], apow[:SUBLANES, n:]
        nxt_r = a1r * zr - a1i * zi + er
        nxt_i = a1r * zi + a1i * zr + ei
        carry_ref[s] = jnp.concatenate(
            [jnp.broadcast_to(nxt_r[SUBLANES - 1:, :], (SUBLANES, n)),
             jnp.broadcast_to(nxt_i[SUBLANES - 1:, :], (SUBLANES, n))], axis=1)

        for i in range(sub):
            rows = slice(SUBLANES * i, SUBLANES * (i + 1))
            x = xs_ref[s, rows, :]
            p = ptab_ref[s, rows, :]
            pr, pi = p[:, :n], p[:, n:]
            xs_ref[s, rows, :] = jnp.concatenate([x[:, :n] + pr * zr - pi * zi,
                                                  x[:, n:] + pr * zi + pi * zr], axis=1)

    def out_proj(s):
        y = jnp.dot(xs_ref[s].astype(BF16), c_ref[s], preferred_element_type=F32)
        g = jax.nn.gelu(y + d_ref[:, S5_SLAB * s:S5_SLAB * (s + 1)] * up_ref[s])
        for c in range(halves):
            us_ref[s * halves + c] = g[:, LANES * c:LANES * (c + 1)]
        for j in range(SUBLANES):
            for c in range(halves):
                o_ref[sub * j:sub * (j + 1), S5_SLAB * s + LANES * c:S5_SLAB * s + LANES * (c + 1)] = (
                    us_ref[s * halves + c, pl.ds(j, sub, stride=SUBLANES), :].astype(o_ref.dtype))

    for s in range(S5_PAIR):
        in_proj(s)
    for s in range(S5_PAIR):
        scan(s)
        out_proj(s)


def _s5_tables(a_re, a_im, log_step, b_re, b_im, c_re, c_im, sub):
    g = a_re.shape[0]
    n_slab = g * S5_GROUP // S5_SLAB
    gl = S5_SLAB // S5_GROUP
    dt = jnp.exp(log_step.astype(F32))[:, None]
    ar = a_re.astype(F32)
    ai = a_im.astype(F32)
    mag = jnp.exp(ar * dt)
    lb_re = mag * jnp.cos(ai * dt)
    lb_im = mag * jnp.sin(ai * dt)
    den = ar * ar + ai * ai
    nr = lb_re - 1.0
    coef_re = (nr * ar + lb_im * ai) / den
    coef_im = (lb_im * ar - nr * ai) / den
    bb_re = coef_re[..., None] * b_re - coef_im[..., None] * b_im
    bb_im = coef_re[..., None] * b_im + coef_im[..., None] * b_re
    eye = jnp.eye(gl, dtype=F32)

    def b_slab(t):
        t = t.reshape(n_slab, gl, S5_STATE, S5_GROUP)
        return jnp.einsum("kgpc,gh->kgchp", t, eye).reshape(n_slab, S5_SLAB, gl * S5_STATE)

    def c_slab(t):
        t = t.reshape(n_slab, gl, S5_GROUP, S5_STATE)
        return jnp.einsum("kgcp,gh->kgphc", t, eye).reshape(n_slab, gl * S5_STATE, S5_SLAB)

    b_dense = jnp.concatenate([b_slab(bb_re), b_slab(bb_im)], axis=2).astype(BF16)
    c_dense = jnp.concatenate([c_slab(c_re.astype(F32)), -c_slab(c_im.astype(F32))], axis=1).astype(BF16)

    def flat(t):
        return t.reshape(n_slab, gl * S5_STATE)

    def power(k):
        kk = k.astype(F32)[None, :, None]
        m = jnp.exp(flat(ar * dt)[:, None, :] * kk)
        ph = flat(ai * dt)[:, None, :] * kk
        return jnp.concatenate([m * jnp.cos(ph), m * jnp.sin(ph)], axis=2)

    lam = jnp.repeat(power(jnp.array([1])), SUBLANES, axis=1)
    ptab = jnp.repeat(power(jnp.arange(1, sub + 1)), SUBLANES, axis=1)
    apow = jnp.repeat(power(jnp.array([sub, 2 * sub, 4 * sub])), SUBLANES, axis=1)
    return b_dense, c_dense, lam, ptab, apow


def _s5_mixer(u, a_re, a_im, log_step, b_re, b_im, c_re, c_im, d_skip, sub=S5_SUB):
    seq, d = u.shape
    rows = SUBLANES * sub
    n_slab = d // S5_SLAB
    n2 = 2 * S5_SLAB_STATES
    b_dense, c_dense, lam, ptab, apow = _s5_tables(a_re, a_im, log_step, b_re, b_im, c_re, c_im, sub)
    return pl.pallas_call(
        functools.partial(_s5_kernel, sub=sub),
        out_shape=jax.ShapeDtypeStruct((seq, d), BF16),
        grid=(n_slab // S5_PAIR, seq // rows),
        in_specs=[
            pl.BlockSpec((rows, S5_PAIR * S5_SLAB), lambda k, c: (c, k)),
            pl.BlockSpec((S5_PAIR, S5_SLAB, n2), lambda k, c: (k, 0, 0)),
            pl.BlockSpec((S5_PAIR, n2, S5_SLAB), lambda k, c: (k, 0, 0)),
            pl.BlockSpec((S5_PAIR, SUBLANES, n2), lambda k, c: (k, 0, 0)),
            pl.BlockSpec((S5_PAIR, rows, n2), lambda k, c: (k, 0, 0)),
            pl.BlockSpec((S5_PAIR, 3 * SUBLANES, n2), lambda k, c: (k, 0, 0)),
            pl.BlockSpec((1, S5_PAIR * S5_SLAB), lambda k, c: (0, k)),
        ],
        out_specs=pl.BlockSpec((rows, S5_PAIR * S5_SLAB), lambda k, c: (c, k)),
        scratch_shapes=[pltpu.VMEM((S5_PAIR, rows, n2), F32),
                        pltpu.VMEM((S5_PAIR, SUBLANES, n2), F32),
                        pltpu.VMEM((S5_PAIR, rows, S5_SLAB), F32),
                        pltpu.VMEM((S5_PAIR * S5_SLAB // LANES, rows, LANES), F32)],
        compiler_params=_params("arbitrary", "arbitrary"),
        name="s5_scan",
    )(u, b_dense, c_dense, lam, ptab, apow, d_skip.reshape(1, d).astype(F32))


def _cache_weights(first, pairs):
    @pl.when(first)
    def _():
        for src, dst in pairs:
            dst[...] = src[...].astype(BF16)


def _glu_kernel(a_ref, wa_ref, wb_ref, r_ref, o_ref, wa_s, wb_s):
    _cache_weights(pl.program_id(1) == 0, ((wa_ref, wa_s), (wb_ref, wb_s)))
    a = a_ref[...]
    va = jnp.dot(a, wa_s[...], preferred_element_type=F32)
    vb = jnp.dot(a, wb_s[...], preferred_element_type=F32)
    o_ref[...] = r_ref[...] + va * jax.nn.sigmoid(vb)


def _glu_residual(a, w, res, tm=512, tn=512):
    m, k = a.shape
    n = w.shape[1] // 2
    nb = n // tn
    return pl.pallas_call(
        _glu_kernel,
        out_shape=jax.ShapeDtypeStruct((m, n), F32),
        grid=(nb, m // tm),
        in_specs=[pl.BlockSpec((tm, k), lambda j, i: (i, 0)),
                  pl.BlockSpec((k, tn), lambda j, i: (0, j)),
                  pl.BlockSpec((k, tn), lambda j, i: (0, j + nb)),
                  pl.BlockSpec((tm, tn), lambda j, i: (i, j))],
        out_specs=pl.BlockSpec((tm, tn), lambda j, i: (i, j)),
        scratch_shapes=[pltpu.VMEM((k, tn), BF16), pltpu.VMEM((k, tn), BF16)],
        compiler_params=_params("arbitrary", "arbitrary"),
        name="glu_residual",
    )(a, w, w, res)


def _swiglu_up_kernel(a_ref, wg_ref, wu_ref, o_ref, wg_s, wu_s):
    _cache_weights(pl.program_id(1) == 0, ((wg_ref, wg_s), (wu_ref, wu_s)))
    a = a_ref[...]
    vg = jnp.dot(a, wg_s[...], preferred_element_type=F32)
    vu = jnp.dot(a, wu_s[...], preferred_element_type=F32)
    o_ref[...] = (jax.nn.silu(vg) * vu).astype(o_ref.dtype)


def _swiglu_up(a, w_gu, tm=512, tn=512):
    m, k = a.shape
    f = w_gu.shape[1] // 2
    nb = f // tn
    return pl.pallas_call(
        _swiglu_up_kernel,
        out_shape=jax.ShapeDtypeStruct((m, f), BF16),
        grid=(nb, m // tm),
        in_specs=[pl.BlockSpec((tm, k), lambda j, i: (i, 0)),
                  pl.BlockSpec((k, tn), lambda j, i: (0, j)),
                  pl.BlockSpec((k, tn), lambda j, i: (0, j + nb))],
        out_specs=pl.BlockSpec((tm, tn), lambda j, i: (i, j)),
        scratch_shapes=[pltpu.VMEM((k, tn), BF16), pltpu.VMEM((k, tn), BF16)],
        compiler_params=_params("arbitrary", "arbitrary"),
        name="swiglu_up",
    )(a, w_gu, w_gu)


def _mm_res_kernel(a_ref, w_ref, r_ref, o_ref, w_s):
    _cache_weights(pl.program_id(1) == 0, ((w_ref, w_s),))
    o_ref[...] = r_ref[...] + jnp.dot(a_ref[...], w_s[...], preferred_element_type=F32)


def _matmul_residual(a, w, res, tm=512, tn=512):
    m, k = a.shape
    n = w.shape[1]
    return pl.pallas_call(
        _mm_res_kernel,
        out_shape=jax.ShapeDtypeStruct((m, n), F32),
        grid=(n // tn, m // tm),
        in_specs=[pl.BlockSpec((tm, k), lambda j, i: (i, 0)),
                  pl.BlockSpec((k, tn), lambda j, i: (0, j)),
                  pl.BlockSpec((tm, tn), lambda j, i: (i, j))],
        out_specs=pl.BlockSpec((tm, tn), lambda j, i: (i, j)),
        scratch_shapes=[pltpu.VMEM((k, tn), BF16)],
        compiler_params=_params("arbitrary", "arbitrary"),
        name="matmul_residual",
    )(a, w, res)


def _rope_kernel(pos_ref, inv_ref, cos_ref, sa_ref, sb_ref):
    ang = pos_ref[...].astype(F32) * inv_ref[...]
    c = jnp.cos(ang)
    s = jnp.sin(ang)
    lane = lax.broadcasted_iota(jnp.int32, ang.shape, 1)
    first_half = lane < ROPE_DIM // 2
    cos_ref[...] = c
    sa_ref[...] = jnp.where(first_half, -s, 0.0)
    sb_ref[...] = jnp.where(first_half, 0.0, s)


def _rope_tables(pos, tm=512):
    n = pos.shape[0]
    tm = min(tm, n)
    half = ROPE_DIM // 2
    inv = jnp.power(ROPE_THETA, -jnp.arange(half, dtype=F32) / half)
    inv = jnp.concatenate([inv, inv, jnp.zeros((LANES - ROPE_DIM,), F32)]).reshape(1, LANES)
    spec = pl.BlockSpec((tm, LANES), lambda i: (i, 0))
    return pl.pallas_call(
        _rope_kernel,
        out_shape=[jax.ShapeDtypeStruct((n, LANES), F32)] * 3,
        grid=(n // tm,),
        in_specs=[pl.BlockSpec((tm, 1), lambda i: (i, 0)),
                  pl.BlockSpec((1, LANES), lambda i: (0, 0))],
        out_specs=[spec, spec, spec],
        compiler_params=_params("arbitrary"),
        name="rope_tables",
    )(pos.reshape(n, 1), inv)


def _rope(y, c, sa, sb):
    half = ROPE_DIM // 2
    return y * c + pltpu.roll(y, LANES - half, 1) * sa + pltpu.roll(y, half, 1) * sb


def _head_norm(x, gain):
    return x * lax.rsqrt(jnp.mean(x * x, axis=-1, keepdims=True) + RMS_EPS) * gain


def _nsa_proj_kernel(a_ref, w_ref, gain_ref, cos_ref, sa_ref, sb_ref, o_ref, w_s, *, norm_tiles):
    j = pl.program_id(0)
    _cache_weights(pl.program_id(1) == 0, ((w_ref, w_s),))
    acc = jnp.dot(a_ref[...], w_s[...], preferred_element_type=F32)
    is_norm = functools.reduce(jnp.logical_or, [j == t for t in norm_tiles])

    @pl.when(is_norm)
    def _():
        c, sa, sb = cos_ref[...], sa_ref[...], sb_ref[...]
        gain = gain_ref[0]
        for hh in range(acc.shape[1] // HEAD_DIM):
            sl = slice(HEAD_DIM * hh, HEAD_DIM * (hh + 1))
            o_ref[:, sl] = _rope(_head_norm(acc[:, sl], gain), c, sa, sb).astype(o_ref.dtype)

    @pl.when(jnp.logical_not(is_norm))
    def _():
        o_ref[...] = acc.astype(o_ref.dtype)


def _nsa_proj(u, w_in, q_gain, k_gain, rope, tm=512, tn=512):
    m, k = u.shape
    q_dim = GQA_GROUP * N_KV_HEADS * HEAD_DIM
    kv_dim = N_KV_HEADS * HEAD_DIM
    assert kv_dim == tn
    n_q = q_dim // tn
    n_tiles = n_q + 6
    ones = jnp.ones((HEAD_DIM,), F32)
    q_scaled = q_gain.astype(F32) * (HEAD_DIM ** -0.5 * math.log2(math.e))
    gains = jnp.stack([q_scaled] * n_q + [ones, ones, k_gain[1], ones, k_gain[2], ones]).reshape(n_tiles, 1, HEAD_DIM)
    norm_tiles = tuple(range(n_q)) + (n_q + 2, n_q + 4)
    tab = pl.BlockSpec((tm, LANES), lambda j, i: (i, 0))
    return pl.pallas_call(
        functools.partial(_nsa_proj_kernel, norm_tiles=norm_tiles),
        out_shape=jax.ShapeDtypeStruct((m, n_tiles * tn), BF16),
        grid=(n_tiles, m // tm),
        in_specs=[pl.BlockSpec((tm, k), lambda j, i: (i, 0)),
                  pl.BlockSpec((k, tn), lambda j, i: (0, j)),
                  pl.BlockSpec((1, 1, HEAD_DIM), lambda j, i: (j, 0, 0)),
                  tab, tab, tab],
        out_specs=pl.BlockSpec((tm, tn), lambda j, i: (i, j)),
        scratch_shapes=[pltpu.VMEM((k, tn), BF16)],
        compiler_params=_params("arbitrary", "arbitrary"),
        name="nsa_proj",
    )(u, w_in, gains, *rope)


def _gate_kernel(a_ref, w_ref, o_ref):
    o_ref[...] = jax.nn.sigmoid(jnp.dot(a_ref[...], w_ref[...].astype(BF16), preferred_element_type=F32))


def _nsa_gates(u, w_gate, tm=512):
    m, k = u.shape
    n = w_gate.shape[1]
    w_pad = jnp.pad(w_gate, ((0, 0), (0, LANES - n)))
    return pl.pallas_call(
        _gate_kernel,
        out_shape=jax.ShapeDtypeStruct((m, LANES), F32),
        grid=(m // tm,),
        in_specs=[pl.BlockSpec((tm, k), lambda i: (i, 0)),
                  pl.BlockSpec((k, LANES), lambda i: (0, 0))],
        out_specs=pl.BlockSpec((tm, LANES), lambda i: (i, 0)),
        compiler_params=_params("arbitrary"),
        name="nsa_gates",
    )(u, w_pad)


def _compress_kernel(*refs, is_key):
    if is_key:
        ca_ref, cb_ref, pe_ref, w1_ref, w2_ref, gain_ref, cos_ref, sa_ref, sb_ref, o_ref = refs
    else:
        ca_ref, cb_ref, pe_ref, w1_ref, w2_ref, o_ref = refs
    half = w1_ref.shape[0] // 2
    pe = pe_ref[...]
    xa = (ca_ref[0].astype(F32) + pe[:, :half]).astype(BF16)
    xb = (cb_ref[0].astype(F32) + pe[:, half:]).astype(BF16)
    hid = (jnp.dot(xa, w1_ref[:half, :].astype(BF16), preferred_element_type=F32)
           + jnp.dot(xb, w1_ref[half:, :].astype(BF16), preferred_element_type=F32))
    out = jnp.dot(jax.nn.gelu(hid).astype(BF16), w2_ref[...].astype(BF16), preferred_element_type=F32)
    if is_key:
        out = _rope(_head_norm(out, gain_ref[...]), cos_ref[...], sa_ref[...], sb_ref[...])
    o_ref[0] = out.astype(o_ref.dtype)


def _compress(t, pe, w1, w2, key_extras=None):
    seq = t.shape[0]
    nc = seq // CMP_STRIDE
    width = CMP_STRIDE * HEAD_DIM
    ca = t.reshape(nc, CMP_STRIDE, N_KV_HEADS, HEAD_DIM).transpose(2, 0, 1, 3).reshape(N_KV_HEADS, nc, width)
    cb = jnp.concatenate([ca[:, 1:], jnp.zeros((N_KV_HEADS, 1, width), ca.dtype)], axis=1)
    blk = pl.BlockSpec((1, nc, width), lambda h: (h, 0, 0))
    full = lambda a: pl.BlockSpec(a.shape, lambda h: (0,) * a.ndim)
    args = [ca, cb, pe.reshape(1, CMP_BLOCK * HEAD_DIM), w1, w2]
    if key_extras is not None:
        args += list(key_extras)
    return pl.pallas_call(
        functools.partial(_compress_kernel, is_key=key_extras is not None),
        out_shape=jax.ShapeDtypeStruct((N_KV_HEADS, nc, HEAD_DIM), BF16),
        grid=(N_KV_HEADS,),
        in_specs=[blk, blk] + [full(a) for a in args[2:]],
        out_specs=pl.BlockSpec((1, nc, HEAD_DIM), lambda h: (h, 0, 0)),
        compiler_params=_params("arbitrary"),
        name="nsa_compress_k" if key_extras is not None else "nsa_compress_v",
    )(*args)


def _dot_nt(a, b):
    return lax.dot_general(a, b, (((1,), (1,)), ((), ())), preferred_element_type=F32)


def _split3(x):
    hi = x.astype(BF16)
    r1 = x - hi.astype(F32)
    mid = r1.astype(BF16)
    lo = (r1 - mid.astype(F32)).astype(BF16)
    return hi, mid, lo


def _nsa_attn_kernel(q_ref, kc_ref, vct_ref, ks_ref, vst_ref, kw_ref, vwt_ref, gate_ref, blk_ref, o_ref,
                     acc_ref, mix_ref, m_ref, l_ref, qa_ref, s0_ref, s1_ref, pc_ref, sw_ref, pw_ref,
                     *, seq, tk):
    t0 = pl.program_id(1) * Q_BLOCK
    nc = kc_ref.shape[1]
    ns = seq // SEL_BLOCK
    grp = GQA_GROUP
    cols = grp * Q_BLOCK
    sel_shift = int(math.log2(SEL_BLOCK))
    q = q_ref[...].astype(F32)
    qt = jnp.concatenate([q[:, HEAD_DIM * g:HEAD_DIM * (g + 1)].T for g in range(grp)], axis=1).astype(BF16)
    t_row = t0 + lax.broadcasted_iota(jnp.int32, (1, Q_BLOCK), 1)

    def heads(x):
        return jnp.concatenate([x] * grp, axis=1)

    span = WINDOW + Q_BLOCK
    w0 = pl.multiple_of(jnp.maximum(t0 - WINDOW, 0), Q_BLOCK)
    gate = gate_ref[0, 0]
    n_idx = lax.broadcasted_iota(jnp.int32, (nc, Q_BLOCK), 0)
    ok_c = (n_idx * CMP_STRIDE + (CMP_BLOCK - 1) <= t_row) & (n_idx < nc - 1)
    sb_c = jnp.dot(kc_ref[0], qt, preferred_element_type=F32) + heads(jnp.where(ok_c, 0.0, NEG))
    m_c = jnp.max(sb_c, axis=0, keepdims=True)
    e_c = jnp.exp2(sb_c - m_c)
    den_c = jnp.maximum(jnp.sum(e_c, axis=0, keepdims=True), 1e-30)
    p_c = e_c * jnp.where(m_c > 0.5 * NEG, 1.0 / den_c, 0.0)
    pc_ref[...] = p_c.astype(BF16)
    imp = p_c[:, :Q_BLOCK]
    for g in range(1, grp):
        imp = imp + p_c[:, Q_BLOCK * g:Q_BLOCK * (g + 1)]

    ratio = SEL_BLOCK // CMP_STRIDE
    d = (lax.broadcasted_iota(jnp.int32, (ns, nc), 1)
         - ratio * lax.broadcasted_iota(jnp.int32, (ns, nc), 0))
    overlap = jnp.zeros((ns, nc), F32)
    for n in range(CMP_BLOCK // CMP_STRIDE):
        overlap = overlap + jnp.where((d - n >= 0) & (d - n < ratio), 1.0, 0.0)
    overlap = overlap.astype(BF16)
    p_slc = sum(jnp.dot(overlap, part, preferred_element_type=F32) for part in _split3(imp))

    mix_ref[...] = gate[0:1] * jnp.dot(vct_ref[0], pc_ref[...], preferred_element_type=F32)
    rel = t_row - (w0 + lax.broadcasted_iota(jnp.int32, (span, Q_BLOCK), 0))
    sw_ref[...] = (jnp.dot(kw_ref[pl.ds(w0, span), :], qt, preferred_element_type=F32)
                   + heads(jnp.where((rel >= 0) & (rel < WINDOW), 0.0, NEG)))
    m_w = jnp.max(sw_ref[...], axis=0, keepdims=True)

    j_idx = lax.broadcasted_iota(jnp.int32, (ns, Q_BLOCK), 0)
    j_f = j_idx.astype(F32)
    dist = jnp.right_shift(t_row, sel_shift) - j_idx
    forced = (j_idx == 0) | ((dist >= 0) & (dist < SEL_LOCAL))
    score = jnp.where(forced, jnp.inf, jnp.where(dist >= 0, p_slc, -jnp.inf))
    sel = jnp.zeros((ns, Q_BLOCK), F32)
    k_top = min(SEL_TOPK, ns)
    pack = 2 * SUBLANES
    cuts = [span // pack * r // k_top * pack for r in range(k_top + 1)]
    den_w = jnp.zeros((1, cols), F32)
    for r in range(k_top):
        top = jnp.max(score, axis=0, keepdims=True)
        idx = jnp.min(jnp.where(score == top, j_f, float(ns)), axis=0, keepdims=True)
        pick = j_f == idx
        sel = jnp.where(pick, 1.0, sel)
        score = jnp.where(pick, -jnp.inf, score)
        if cuts[r + 1] > cuts[r]:
            e_w = jnp.exp2(sw_ref[cuts[r]:cuts[r + 1], :] - m_w)
            den_w = den_w + jnp.sum(e_w, axis=0, keepdims=True)
            pw_ref[cuts[r]:cuts[r + 1], :] = e_w.astype(BF16)
    o_w = jnp.dot(vwt_ref[0, :, pl.ds(w0, span)], pw_ref[...], preferred_element_type=F32)
    mix_ref[...] += (gate[2:3] * (1.0 / den_w)) * o_w

    qa_ref[...] = jnp.concatenate([qt, heads(jnp.where(sel > 0.0, 0.0, NEG).astype(BF16))], axis=0)
    acc_ref[...] = jnp.zeros_like(acc_ref)
    m_ref[...] = jnp.full_like(m_ref, NEG)
    l_ref[...] = jnp.zeros_like(l_ref)

    def score_tile(kt, dst):
        k0 = pl.multiple_of(kt * tk, tk)
        k_aug = jnp.concatenate([ks_ref[pl.ds(k0, tk), :], blk_ref[pl.ds(k0, tk), :]], axis=1)
        dst[...] = jnp.dot(k_aug, qa_ref[...], preferred_element_type=F32)

    def consume_tile(kt, src, diagonal):
        k0 = pl.multiple_of(kt * tk, tk)
        sb = src[...]
        if diagonal:
            kpos = k0 + lax.broadcasted_iota(jnp.int32, (tk, Q_BLOCK), 0)
            sb = sb + heads(jnp.where(kpos <= t_row, 0.0, NEG))
        m_i = m_ref[...]
        m_new = jnp.maximum(m_i, jnp.max(sb, axis=0, keepdims=True))
        e = jnp.exp2(sb - m_new)
        alpha = jnp.exp2(m_i - m_new)
        m_ref[...] = m_new
        l_ref[...] = alpha * l_ref[...] + jnp.sum(e, axis=0, keepdims=True)
        acc_ref[...] = alpha * acc_ref[...] + jnp.dot(vst_ref[0, :, pl.ds(k0, tk)], e.astype(BF16),
                                                      preferred_element_type=F32)

    last = t0 // tk
    score_tile(0, s0_ref)

    def tile_pair(i, _):
        score_tile(2 * i + 1, s1_ref)
        consume_tile(2 * i, s0_ref, False)
        score_tile(2 * i + 2, s0_ref)
        consume_tile(2 * i + 1, s1_ref, False)
        return 0

    lax.fori_loop(0, last // 2, tile_pair, 0)

    @pl.when(last % 2 == 1)
    def _():
        score_tile(last, s1_ref)
        consume_tile(last - 1, s0_ref, False)
        consume_tile(last, s1_ref, True)

    @pl.when(last % 2 == 0)
    def _():
        consume_tile(last, s0_ref, True)

    o_s = acc_ref[...] * (1.0 / jnp.maximum(l_ref[...], 1e-30))

    mixed = mix_ref[...] + gate_ref[0, 0, 1:2] * o_s
    for g in range(grp):
        o_ref[:, HEAD_DIM * g:HEAD_DIM * (g + 1)] = mixed[:, Q_BLOCK * g:Q_BLOCK * (g + 1)].T.astype(o_ref.dtype)


def _nsa_attention(proj, kcmp, vcmp_t, vsl_t, vw_t, gates, tk=512):
    seq = proj.shape[0]
    tk = min(tk, seq)
    q_dim = GQA_GROUP * N_KV_HEADS * HEAD_DIM
    kv_blocks = N_KV_HEADS
    first = q_dim // HEAD_DIM + 2 * kv_blocks
    nc = kcmp.shape[1]
    cols = GQA_GROUP * Q_BLOCK

    def k_spec(which):
        return pl.BlockSpec((seq, HEAD_DIM), lambda h, qb: (0, first + which * kv_blocks + h))

    def vt_spec(n):
        return pl.BlockSpec((1, HEAD_DIM, n), lambda h, qb: (h, 0, 0))

    q_spec = pl.BlockSpec((Q_BLOCK, GQA_GROUP * HEAD_DIM), lambda h, qb: (qb, h))
    ns = seq // SEL_BLOCK
    key_block = (jnp.arange(seq, dtype=jnp.int32)[:, None] // SEL_BLOCK
                 == jnp.arange(ns, dtype=jnp.int32)[None, :]).astype(BF16)
    return pl.pallas_call(
        functools.partial(_nsa_attn_kernel, seq=seq, tk=tk),
        out_shape=jax.ShapeDtypeStruct((seq, q_dim), BF16),
        grid=(N_KV_HEADS, seq // Q_BLOCK),
        in_specs=[q_spec, pl.BlockSpec((1, nc, HEAD_DIM), lambda h, qb: (h, 0, 0)), vt_spec(nc),
                  k_spec(0), vt_spec(seq), k_spec(2), vt_spec(seq),
                  pl.BlockSpec((1, 1, 3, cols), lambda h, qb: (h, qb, 0, 0)),
                  pl.BlockSpec((seq, ns), lambda h, qb: (0, 0))],
        out_specs=q_spec,
        scratch_shapes=[pltpu.VMEM((HEAD_DIM, cols), F32), pltpu.VMEM((HEAD_DIM, cols), F32),
                        pltpu.VMEM((1, cols), F32), pltpu.VMEM((1, cols), F32),
                        pltpu.VMEM((HEAD_DIM + ns, cols), BF16),
                        pltpu.VMEM((tk, cols), F32), pltpu.VMEM((tk, cols), F32),
                        pltpu.VMEM((nc, cols), BF16), pltpu.VMEM((WINDOW + Q_BLOCK, cols), F32),
                        pltpu.VMEM((WINDOW + Q_BLOCK, cols), BF16)],
        compiler_params=_params("arbitrary", "arbitrary"),
        name="nsa_attention",
    )(proj, kcmp, vcmp_t, proj, vsl_t, proj, vw_t, gates, key_block)


def _nsa_mixer(u, positions, w_in, q_gain, k_gain, pe_k, pe_v, ck_w1, ck_w2, cv_w1, cv_w2):
    seq = u.shape[0]
    q_dim = GQA_GROUP * N_KV_HEADS * HEAD_DIM
    kv_dim = N_KV_HEADS * HEAD_DIM
    n_main = q_dim + 6 * kv_dim
    nc = seq // CMP_STRIDE
    rope = _rope_tables(positions)
    proj = _nsa_proj(u, w_in, q_gain, k_gain, rope)
    gate = _nsa_gates(u, w_in[:, n_main:])
    gates = (gate[:, :3 * N_KV_HEADS * GQA_GROUP].reshape(seq // Q_BLOCK, Q_BLOCK, 3, N_KV_HEADS, GQA_GROUP)
             .transpose(3, 0, 2, 4, 1).reshape(N_KV_HEADS, seq // Q_BLOCK, 3, GQA_GROUP * Q_BLOCK))

    def keys_last(cols):
        return cols.reshape(seq, N_KV_HEADS, HEAD_DIM).transpose(1, 2, 0)

    pos_cmp = jnp.concatenate([positions[CMP_BLOCK - 1::CMP_STRIDE][:nc - 1], jnp.zeros((1,), positions.dtype)])
    rope_cmp = _rope_tables(pos_cmp)
    kcmp = _compress(proj[:, q_dim:q_dim + kv_dim], pe_k, ck_w1, ck_w2,
                     key_extras=(k_gain[0].reshape(1, HEAD_DIM),) + tuple(rope_cmp))
    vcmp = _compress(proj[:, q_dim + kv_dim:q_dim + 2 * kv_dim], pe_v, cv_w1, cv_w2)
    vsl_t = keys_last(proj[:, q_dim + 3 * kv_dim:q_dim + 4 * kv_dim])
    vw_t = keys_last(proj[:, q_dim + 5 * kv_dim:q_dim + 6 * kv_dim])
    return _nsa_attention(proj, kcmp, vcmp.transpose(0, 2, 1), vsl_t, vw_t, gates)


def _router_kernel(x_ref, g_ref, w_ref, b_ref, u_ref, r_ref):
    x = x_ref[...]
    u = x * lax.rsqrt(jnp.mean(x * x, axis=-1, keepdims=True) + RMS_EPS) * g_ref[...]
    u_ref[...] = u
    uh, um, _ = _split3(u)
    wh, wm, _ = _split3(w_ref[...])
    logits = (jnp.dot(uh, wh, preferred_element_type=F32) + jnp.dot(uh, wm, preferred_element_type=F32)
              + jnp.dot(um, wh, preferred_element_type=F32)) + b_ref[...]
    lane = lax.broadcasted_iota(jnp.int32, logits.shape, 1).astype(F32)
    lg = jnp.where(lane < N_EXPERTS, logits, -jnp.inf)
    v1 = jnp.max(lg, axis=-1, keepdims=True)
    i1 = jnp.min(jnp.where(lg == v1, lane, float(LANES)), axis=-1, keepdims=True)
    lg = jnp.where(lane == i1, -jnp.inf, lg)
    v2 = jnp.max(lg, axis=-1, keepdims=True)
    i2 = jnp.min(jnp.where(lg == v2, lane, float(LANES)), axis=-1, keepdims=True)
    e2 = jnp.exp(v2 - v1)
    den = 1.0 + e2
    r_ref[...] = jnp.where(lane == 0, i1, jnp.where(lane == 1, i2, jnp.where(
        lane == 2, 1.0 / den, jnp.where(lane == 3, e2 / den, 0.0))))


def _router(h, gain, w_router, b_router, tm=256):
    m, d = h.shape
    w_pad = jnp.pad(w_router.astype(F32), ((0, 0), (0, LANES - N_EXPERTS)))
    b_pad = jnp.pad(b_router.astype(F32), (0, LANES - N_EXPERTS)).reshape(1, LANES)
    return pl.pallas_call(
        _router_kernel,
        out_shape=[jax.ShapeDtypeStruct((m, d), F32), jax.ShapeDtypeStruct((m, LANES), F32)],
        grid=(m // tm,),
        in_specs=[pl.BlockSpec((tm, d), lambda i: (i, 0)),
                  pl.BlockSpec((1, d), lambda i: (0, 0)),
                  pl.BlockSpec((d, LANES), lambda i: (0, 0)),
                  pl.BlockSpec((1, LANES), lambda i: (0, 0))],
        out_specs=[pl.BlockSpec((tm, d), lambda i: (i, 0)), pl.BlockSpec((tm, LANES), lambda i: (i, 0))],
        compiler_params=_params("arbitrary"),
        name="moe_router",
    )(h, gain.reshape(1, d), w_pad, b_pad)


def _row_copy(src_hbm, row, dst, r, sem):
    return pltpu.make_async_copy(src_hbm.at[pl.ds(row, 1), :], dst.at[pl.ds(r, 1), :], sem)


def _gather_kernel(idx_ref, used_ref, src_hbm, o_ref, buf, sem):
    rows = o_ref.shape[0]
    i = pl.program_id(0)
    n_used = used_ref[0]

    def issue(blk):
        slot = blk % 2

        def start(r, _):
            _row_copy(src_hbm, idx_ref[blk * rows + r], buf.at[slot], r, sem.at[slot]).start()
            return 0

        lax.fori_loop(0, rows, start, 0, unroll=DMA_UNROLL)

    @pl.when(i == 0)
    def _():
        issue(i)

    @pl.when(i + 1 < n_used)
    def _():
        issue(i + 1)

    @pl.when(i < n_used)
    def _():
        slot = i % 2

        def wait(r, _):
            _row_copy(src_hbm, 0, buf.at[slot], r, sem.at[slot]).wait()
            return 0

        lax.fori_loop(0, rows, wait, 0, unroll=DMA_UNROLL)
        o_ref[...] = buf[slot].astype(o_ref.dtype)

    @pl.when(i >= n_used)
    def _():
        o_ref[...] = jnp.zeros_like(o_ref)


def _gather_rows(src, idx, n_used, out_dtype, rows=MOE_ROWS):
    n = idx.shape[0]
    d = src.shape[1]
    return pl.pallas_call(
        _gather_kernel,
        out_shape=jax.ShapeDtypeStruct((n, d), out_dtype),
        grid_spec=pltpu.PrefetchScalarGridSpec(
            num_scalar_prefetch=2,
            grid=(n // rows,),
            in_specs=[pl.BlockSpec(memory_space=pl.ANY)],
            out_specs=pl.BlockSpec((rows, d), lambda i, idx, used: (i, 0)),
            scratch_shapes=[pltpu.VMEM((2, rows, d), src.dtype), pltpu.SemaphoreType.DMA((2,))]),
        compiler_params=_params("arbitrary"),
        name="moe_gather",
    )(idx, n_used, src)


def _block_state(be_ref, used_ref, i):
    changed = (i == 0) | (be_ref[i] != be_ref[jnp.maximum(i - 1, 0)])
    used = i < used_ref[0]
    return used, used & changed


def _last_used(i, used):
    return jnp.minimum(i, used[0] - 1)


def _stream_expert_weights(first, be_ref, rix_ref, rune_ref, nrun_ref, n_tiles, copies, cast):
    j = pl.program_id(0)
    i = pl.program_id(1)

    @pl.when(first)
    def _():
        rix = rix_ref[i]
        n_runs = nrun_ref[0]
        g = j * n_runs + rix
        slot = g % 2

        @pl.when(g == 0)
        def _():
            for c in copies(be_ref[i], j, slot):
                c.start()

        more = rix + 1 < n_runs
        e_next = jnp.where(more, rune_ref[jnp.minimum(rix + 1, N_EXPERTS - 1)], rune_ref[0])
        j_next = jnp.where(more, j, j + 1)

        @pl.when(j_next < n_tiles)
        def _():
            for c in copies(e_next, j_next, 1 - slot):
                c.start()

        for c in copies(be_ref[i], j, slot):
            c.wait()
        cast(slot)


def _moe_up_kernel(be_ref, used_ref, rix_ref, rune_ref, nrun_ref, x_ref, w_hbm, o_ref, wbuf, wg_s, wu_s, sem,
                   *, nb, tn):
    used, first = _block_state(be_ref, used_ref, pl.program_id(1))

    def copies(e, jj, slot):
        return [pltpu.make_async_copy(w_hbm.at[e, :, pl.ds(pl.multiple_of((jj + m * nb) * tn, tn), tn)],
                                      wbuf.at[slot, m], sem.at[slot, m]) for m in range(2)]

    def cast(slot):
        wg_s[...] = wbuf[slot, 0].astype(BF16)
        wu_s[...] = wbuf[slot, 1].astype(BF16)

    _stream_expert_weights(first, be_ref, rix_ref, rune_ref, nrun_ref, nb, copies, cast)

    @pl.when(used)
    def _():
        a = x_ref[...]
        vg = jnp.dot(a, wg_s[...], preferred_element_type=F32)
        vu = jnp.dot(a, wu_s[...], preferred_element_type=F32)
        o_ref[...] = (jax.nn.silu(vg) * vu).astype(o_ref.dtype)

    @pl.when(jnp.logical_not(used))
    def _():
        o_ref[...] = jnp.zeros_like(o_ref)


def _moe_up(x_rows, sched, w_gu, tn=512, rows=MOE_ROWS):
    n, k = x_rows.shape
    f = w_gu.shape[2] // 2
    nb = f // tn
    return pl.pallas_call(
        functools.partial(_moe_up_kernel, nb=nb, tn=tn),
        out_shape=jax.ShapeDtypeStruct((n, f), BF16),
        grid_spec=pltpu.PrefetchScalarGridSpec(
            num_scalar_prefetch=len(sched),
            grid=(nb, n // rows),
            in_specs=[pl.BlockSpec((rows, k), lambda j, i, be, nu, *_: (_last_used(i, nu), 0)),
                      pl.BlockSpec(memory_space=pl.ANY)],
            out_specs=pl.BlockSpec((rows, tn), lambda j, i, *_: (i, j)),
            scratch_shapes=[pltpu.VMEM((2, 2, k, tn), F32), pltpu.VMEM((k, tn), BF16),
                            pltpu.VMEM((k, tn), BF16), pltpu.SemaphoreType.DMA((2, 2))]),
        compiler_params=_params("arbitrary", "arbitrary"),
        name="moe_up",
    )(*sched, x_rows, w_gu)


def _moe_down_kernel(be_ref, used_ref, rix_ref, rune_ref, nrun_ref, a_ref, w_hbm, o_ref, wbuf, w_s, sem,
                     *, nb, tn):
    used, first = _block_state(be_ref, used_ref, pl.program_id(1))

    def copies(e, jj, slot):
        return [pltpu.make_async_copy(w_hbm.at[e, :, pl.ds(pl.multiple_of(jj * tn, tn), tn)],
                                      wbuf.at[slot], sem.at[slot])]

    def cast(slot):
        w_s[...] = wbuf[slot].astype(BF16)

    _stream_expert_weights(first, be_ref, rix_ref, rune_ref, nrun_ref, nb, copies, cast)

    @pl.when(used)
    def _():
        o_ref[...] = jnp.dot(a_ref[...], w_s[...], preferred_element_type=F32)

    @pl.when(jnp.logical_not(used))
    def _():
        o_ref[...] = jnp.zeros_like(o_ref)


def _moe_down(act, sched, w_down, tn=512, rows=MOE_ROWS):
    n, k = act.shape
    d = w_down.shape[2]
    nb = d // tn
    return pl.pallas_call(
        functools.partial(_moe_down_kernel, nb=nb, tn=tn),
        out_shape=jax.ShapeDtypeStruct((n, d), F32),
        grid_spec=pltpu.PrefetchScalarGridSpec(
            num_scalar_prefetch=len(sched),
            grid=(nb, n // rows),
            in_specs=[pl.BlockSpec((rows, k), lambda j, i, be, nu, *_: (_last_used(i, nu), 0)),
                      pl.BlockSpec(memory_space=pl.ANY)],
            out_specs=pl.BlockSpec((rows, tn), lambda j, i, *_: (i, j)),
            scratch_shapes=[pltpu.VMEM((2, k, tn), F32), pltpu.VMEM((k, tn), BF16),
                            pltpu.SemaphoreType.DMA((2,))]),
        compiler_params=_params("arbitrary", "arbitrary"),
        name="moe_down",
    )(*sched, act, w_down)


def _combine_kernel(dest_ref, h_ref, r_ref, rows_hbm, o_ref, buf, sem):
    tm = h_ref.shape[0]
    base = pl.program_id(0) * tm

    def start(r, _):
        for k in range(2):
            _row_copy(rows_hbm, dest_ref[2 * (base + r) + k], buf.at[k], r, sem.at[k]).start()
        return 0

    def wait(r, _):
        for k in range(2):
            _row_copy(rows_hbm, 0, buf.at[k], r, sem.at[k]).wait()
        return 0

    lax.fori_loop(0, tm, start, 0, unroll=DMA_UNROLL)
    lax.fori_loop(0, tm, wait, 0, unroll=DMA_UNROLL)
    w = r_ref[...]
    o_ref[...] = h_ref[...] + (w[:, 2:3] * buf[0] + w[:, 3:4] * buf[1])


def _moe_combine(h, route, out_rows, dest, tm=256):
    m, d = h.shape
    return pl.pallas_call(
        _combine_kernel,
        out_shape=jax.ShapeDtypeStruct((m, d), F32),
        grid_spec=pltpu.PrefetchScalarGridSpec(
            num_scalar_prefetch=1,
            grid=(m // tm,),
            in_specs=[pl.BlockSpec((tm, d), lambda i, dest: (i, 0)),
                      pl.BlockSpec((tm, LANES), lambda i, dest: (i, 0)),
                      pl.BlockSpec(memory_space=pl.ANY)],
            out_specs=pl.BlockSpec((tm, d), lambda i, dest: (i, 0)),
            scratch_shapes=[pltpu.VMEM((2, tm, d), F32), pltpu.SemaphoreType.DMA((2,))]),
        compiler_params=_params("arbitrary"),
        name="moe_combine",
    )(dest.reshape(-1), h, route, out_rows)


def _moe_layout(top_e, rows=MOE_ROWS):
    n_tok = top_e.shape[0]
    e_flat = top_e.reshape(-1)
    onehot = (e_flat[:, None] == jnp.arange(N_EXPERTS, dtype=jnp.int32)[None, :]).astype(jnp.int32)
    csum = jnp.cumsum(onehot, axis=0)
    rank = jnp.take_along_axis(csum, e_flat[:, None], axis=1)[:, 0] - 1
    counts = csum[-1]
    padded = (counts + rows - 1) // rows * rows
    pad_end = jnp.cumsum(padded)
    dest = (pad_end - padded)[e_flat] + rank
    n_rows = e_flat.shape[0] + N_EXPERTS * rows
    t_flat = jnp.repeat(jnp.arange(n_tok, dtype=jnp.int32), top_e.shape[1])
    row_tok = jnp.zeros((n_rows,), jnp.int32).at[dest].set(t_flat)
    n_blk = n_rows // rows
    blk_start = jnp.arange(n_blk, dtype=jnp.int32) * rows
    blk_e = jnp.minimum(jnp.sum(blk_start[:, None] >= pad_end[None, :], axis=1), N_EXPERTS - 1).astype(jnp.int32)
    n_used = (pad_end[-1:] // rows).astype(jnp.int32)
    first = (jnp.arange(n_blk) < n_used[0]) & (blk_e != jnp.concatenate([blk_e[:1] - 1, blk_e[:-1]]))
    run_ix = (jnp.cumsum(first) - 1).astype(jnp.int32)
    in_run = first[:, None] & (run_ix[:, None] == jnp.arange(N_EXPERTS, dtype=jnp.int32)[None, :])
    run_e = jnp.sum(jnp.where(in_run, blk_e[:, None], 0), axis=0).astype(jnp.int32)
    n_runs = jnp.sum(first).astype(jnp.int32).reshape(1)
    sched = (blk_e, n_used, run_ix, run_e, n_runs)
    return row_tok, sched, dest.astype(jnp.int32).reshape(n_tok, -1)


def _moe_ffn_residual(h, gain, w_router, b_router, w_gu, w_down):
    u, route = _router(h, gain, w_router, b_router)
    top_e = route[:, :2].astype(jnp.int32)
    row_tok, sched, dest = _moe_layout(top_e)
    x_rows = _gather_rows(u, row_tok, sched[1], BF16)
    act = _moe_up(x_rows, sched, w_gu)
    out_rows = _moe_down(act, sched, w_down)
    return _moe_combine(h, route, out_rows, dest)


def kernel(x, positions, norm_mix, norm_ffn, s5_a_re, s5_a_im, s5_log_step, s5_b_re, s5_b_im, s5_c_re, s5_c_im, s5_d, s5_w_glu, nsa_w_in, nsa_q_gain, nsa_k_gain, nsa_pe_k, nsa_pe_v, nsa_ck_w1, nsa_ck_w2, nsa_cv_w1, nsa_cv_w2, nsa_w_out, ffn_w_gu, ffn_w_down, moe_w_router, moe_b_router, moe_w_gu, moe_w_down):
    bsz, seq, d = x.shape
    assert bsz == 1, "the scan and attention kernels take one sequence"
    h = x.reshape(seq, d)
    h = _layer_s5(h, norm_mix[0], norm_ffn[0], s5_a_re[0], s5_a_im[0], s5_log_step[0], s5_b_re[0],
                  s5_b_im[0], s5_c_re[0], s5_c_im[0], s5_d[0], s5_w_glu[0], ffn_w_gu[0], ffn_w_down[0])
    h = _layer_nsa(h, positions[0], norm_mix[1], norm_ffn[1], nsa_w_in[0], nsa_q_gain[0], nsa_k_gain[0],
                   nsa_pe_k[0], nsa_pe_v[0], nsa_ck_w1[0], nsa_ck_w2[0], nsa_cv_w1[0], nsa_cv_w2[0],
                   nsa_w_out[0], moe_w_router[0], moe_b_router[0], moe_w_gu[0], moe_w_down[0])
    return h.reshape(bsz, seq, d)


def _layer_nsa(h, positions, g_mix, g_ffn, w_in, q_gain, k_gain, pe_k, pe_v, ck_w1, ck_w2, cv_w1, cv_w2,
               w_out, w_router, b_router, w_gu, w_down):
    u = _rms_norm(h, g_mix, BF16)
    o = _nsa_mixer(u, positions, w_in, q_gain, k_gain, pe_k, pe_v, ck_w1, ck_w2, cv_w1, cv_w2)
    h = _matmul_residual(o, w_out, h)
    return _moe_ffn_residual(h, g_ffn, w_router, b_router, w_gu, w_down)


def _layer_s5(h, g_mix, g_ffn, a_re, a_im, log_step, b_re, b_im, c_re, c_im, d_skip, w_glu, w_gu, w_down):
    u = _rms_norm(h, g_mix, F32)
    g = _s5_mixer(u, a_re, a_im, log_step, b_re, b_im, c_re, c_im, d_skip)
    h = _glu_residual(g, w_glu, h)
    u = _rms_norm(h, g_ffn, BF16)
    act = _swiglu_up(u, w_gu)
    return _matmul_residual(act, w_down, h)
```

```python
import functools
import math

import jax
import jax.numpy as jnp
from jax import lax
from jax.experimental import pallas as pl
from jax.experimental.pallas import tpu as pltpu

F32 = jnp.float32
BF16 = jnp.bfloat16

RMS_EPS = 1e-6
S5_GROUP = 16
S5_STATE = 64
HEAD_DIM = 128
N_KV_HEADS = 4
GQA_GROUP = 4
ROPE_DIM = 32
ROPE_THETA = 500000.0
CMP_BLOCK = 32
CMP_STRIDE = 16
SEL_BLOCK = 64
SEL_TOPK = 16
SEL_LOCAL = 2
WINDOW = 512
Q_BLOCK = 128
N_EXPERTS = 8
NEG = -1e30

LANES = 128
SUBLANES = 8
VMEM_LIMIT = 56 * 1024 * 1024

S5_SLAB = 256
S5_SLAB_STATES = S5_SLAB // S5_GROUP * S5_STATE
S5_SUB = 64
S5_PAIR = 2
MOE_ROWS = 512
DMA_UNROLL = 8


def _params(*sem):
    return pltpu.CompilerParams(dimension_semantics=sem, vmem_limit_bytes=VMEM_LIMIT)


def _rms_kernel(x_ref, g_ref, o_ref):
    x = x_ref[...]
    ms = jnp.mean(x * x, axis=-1, keepdims=True)
    o_ref[...] = (x * lax.rsqrt(ms + RMS_EPS) * g_ref[...]).astype(o_ref.dtype)


def _rms_norm(x, gain, out_dtype, tm=512):
    m, d = x.shape
    return pl.pallas_call(
        _rms_kernel,
        out_shape=jax.ShapeDtypeStruct((m, d), out_dtype),
        grid=(m // tm,),
        in_specs=[pl.BlockSpec((tm, d), lambda i: (i, 0)),
                  pl.BlockSpec((1, d), lambda i: (0, 0))],
        out_specs=pl.BlockSpec((tm, d), lambda i: (i, 0)),
        compiler_params=_params("arbitrary"),
        name="rms_norm",
    )(x, gain.reshape(1, d))


def _s5_kernel(u_ref, b_ref, c_ref, lam_ref, ptab_ref, apow_ref, d_ref, o_ref,
               xs_ref, carry_ref, up_ref, us_ref, *, sub):
    n = S5_SLAB_STATES
    halves = S5_SLAB // LANES
    row = lax.broadcasted_iota(jnp.int32, (SUBLANES, n), 0)

    @pl.when(pl.program_id(1) == 0)
    def _():
        carry_ref[...] = jnp.zeros_like(carry_ref)

    def in_proj(s):
        for c in range(halves):
            us_ref[s * halves + c] = u_ref[:, S5_SLAB * s + LANES * c:S5_SLAB * s + LANES * (c + 1)]
        for i in range(sub):
            for c in range(halves):
                up_ref[s, SUBLANES * i:SUBLANES * (i + 1), LANES * c:LANES * (c + 1)] = (
                    us_ref[s * halves + c, pl.ds(i, SUBLANES, stride=sub), :])
        xs_ref[s] = jnp.dot(up_ref[s].astype(BF16), b_ref[s], preferred_element_type=F32)

    def scan(s):
        lam = lam_ref[s]
        lr, li = lam[:, :n], lam[:, n:]
        hr = hi = jnp.zeros((SUBLANES, n), F32)
        for i in range(sub):
            rows = slice(SUBLANES * i, SUBLANES * (i + 1))
            x = xs_ref[s, rows, :]
            hr, hi = lr * hr - li * hi + x[:, :n], lr * hi + li * hr + x[:, n:]
            xs_ref[s, rows, :] = jnp.concatenate([hr, hi], axis=1)
        er, ei = hr, hi

        cin = carry_ref[s]
        zr = jnp.where(row == 0, cin[:, :n], pltpu.roll(er, 1, 0))
        zi = jnp.where(row == 0, cin[:, n:], pltpu.roll(ei, 1, 0))
        apow = apow_ref[s]
        for k, d in enumerate((1, 2, 4)):
            ar = apow[SUBLANES * k:SUBLANES * (k + 1), :n]
            ai = apow[SUBLANES * k:SUBLANES * (k + 1), n:]
            sr = pltpu.roll(zr, d, 0)
            si = pltpu.roll(zi, d, 0)
            keep = row >= d
            zr, zi = (zr + jnp.where(keep, ar * sr - ai * si, 0.0),
                      zi + jnp.where(keep, ar * si + ai * sr, 0.0))
        a1r, a1i = apow[:SUBLANES, :n], apow[:SUBLANES, n:]
        nxt_r = a1r * zr - a1i * zi + er
        nxt_i = a1r * zi + a1i * zr + ei
        carry_ref[s] = jnp.concatenate(
            [jnp.broadcast_to(nxt_r[SUBLANES - 1:, :], (SUBLANES, n)),
             jnp.broadcast_to(nxt_i[SUBLANES - 1:, :], (SUBLANES, n))], axis=1)

        for i in range(sub):
            rows = slice(SUBLANES * i, SUBLANES * (i + 1))
            x = xs_ref[s, rows, :]
            p = ptab_ref[s, rows, :]
            pr, pi = p[:, :n], p[:, n:]
            xs_ref[s, rows, :] = jnp.concatenate([x[:, :n] + pr * zr - pi * zi,
                                                  x[:, n:] + pr * zi + pi * zr], axis=1)

    def out_proj(s):
        y = jnp.dot(xs_ref[s].astype(BF16), c_ref[s], preferred_element_type=F32)
        g = jax.nn.gelu(y + d_ref[:, S5_SLAB * s:S5_SLAB * (s + 1)] * up_ref[s])
        for c in range(halves):
            us_ref[s * halves + c] = g[:, LANES * c:LANES * (c + 1)]
        for j in range(SUBLANES):
            for c in range(halves):
                o_ref[sub * j:sub * (j + 1), S5_SLAB * s + LANES * c:S5_SLAB * s + LANES * (c + 1)] = (
                    us_ref[s * halves + c, pl.ds(j, sub, stride=SUBLANES), :].astype(o_ref.dtype))

    for s in range(S5_PAIR):
        in_proj(s)
    for s in range(S5_PAIR):
        scan(s)
        out_proj(s)


def _s5_tables(a_re, a_im, log_step, b_re, b_im, c_re, c_im, sub):
    g = a_re.shape[0]
    n_slab = g * S5_GROUP // S5_SLAB
    gl = S5_SLAB // S5_GROUP
    dt = jnp.exp(log_step.astype(F32))[:, None]
    ar = a_re.astype(F32)
    ai = a_im.astype(F32)
    mag = jnp.exp(ar * dt)
    lb_re = mag * jnp.cos(ai * dt)
    lb_im = mag * jnp.sin(ai * dt)
    den = ar * ar + ai * ai
    nr = lb_re - 1.0
    coef_re = (nr * ar + lb_im * ai) / den
    coef_im = (lb_im * ar - nr * ai) / den
    bb_re = coef_re[..., None] * b_re - coef_im[..., None] * b_im
    bb_im = coef_re[..., None] * b_im + coef_im[..., None] * b_re
    eye = jnp.eye(gl, dtype=F32)

    def b_slab(t):
        t = t.reshape(n_slab, gl, S5_STATE, S5_GROUP)
        return jnp.einsum("kgpc,gh->kgchp", t, eye).reshape(n_slab, S5_SLAB, gl * S5_STATE)

    def c_slab(t):
        t = t.reshape(n_slab, gl, S5_GROUP, S5_STATE)
        return jnp.einsum("kgcp,gh->kgphc", t, eye).reshape(n_slab, gl * S5_STATE, S5_SLAB)

    b_dense = jnp.concatenate([b_slab(bb_re), b_slab(bb_im)], axis=2).astype(BF16)
    c_dense = jnp.concatenate([c_slab(c_re.astype(F32)), -c_slab(c_im.astype(F32))], axis=1).astype(BF16)

    def flat(t):
        return t.reshape(n_slab, gl * S5_STATE)

    def power(k):
        kk = k.astype(F32)[None, :, None]
        m = jnp.exp(flat(ar * dt)[:, None, :] * kk)
        ph = flat(ai * dt)[:, None, :] * kk
        return jnp.concatenate([m * jnp.cos(ph), m * jnp.sin(ph)], axis=2)

    lam = jnp.repeat(power(jnp.array([1])), SUBLANES, axis=1)
    ptab = jnp.repeat(power(jnp.arange(1, sub + 1)), SUBLANES, axis=1)
    apow = jnp.repeat(power(jnp.array([sub, 2 * sub, 4 * sub])), SUBLANES, axis=1)
    return b_dense, c_dense, lam, ptab, apow


def _s5_mixer(u, a_re, a_im, log_step, b_re, b_im, c_re, c_im, d_skip, sub=S5_SUB):
    seq, d = u.shape
    rows = SUBLANES * sub
    n_slab = d // S5_SLAB
    n2 = 2 * S5_SLAB_STATES
    b_dense, c_dense, lam, ptab, apow = _s5_tables(a_re, a_im, log_step, b_re, b_im, c_re, c_im, sub)
    return pl.pallas_call(
        functools.partial(_s5_kernel, sub=sub),
        out_shape=jax.ShapeDtypeStruct((seq, d), BF16),
        grid=(n_slab // S5_PAIR, seq // rows),
        in_specs=[
            pl.BlockSpec((rows, S5_PAIR * S5_SLAB), lambda k, c: (c, k)),
            pl.BlockSpec((S5_PAIR, S5_SLAB, n2), lambda k, c: (k, 0, 0)),
            pl.BlockSpec((S5_PAIR, n2, S5_SLAB), lambda k, c: (k, 0, 0)),
            pl.BlockSpec((S5_PAIR, SUBLANES, n2), lambda k, c: (k, 0, 0)),
            pl.BlockSpec((S5_PAIR, rows, n2), lambda k, c: (k, 0, 0)),
            pl.BlockSpec((S5_PAIR, 3 * SUBLANES, n2), lambda k, c: (k, 0, 0)),
            pl.BlockSpec((1, S5_PAIR * S5_SLAB), lambda k, c: (0, k)),
        ],
        out_specs=pl.BlockSpec((rows, S5_PAIR * S5_SLAB), lambda k, c: (c, k)),
        scratch_shapes=[pltpu.VMEM((S5_PAIR, rows, n2), F32),
                        pltpu.VMEM((S5_PAIR, SUBLANES, n2), F32),
                        pltpu.VMEM((S5_PAIR, rows, S5_SLAB), F32),
                        pltpu.VMEM((S5_PAIR * S5_SLAB // LANES, rows, LANES), F32)],
        compiler_params=_params("arbitrary", "arbitrary"),
        name="s5_scan",
    )(u, b_dense, c_dense, lam, ptab, apow, d_skip.reshape(1, d).astype(F32))


def _cache_weights(first, pairs):
    @pl.when(first)
    def _():
        for src, dst in pairs:
            dst[...] = src[...].astype(BF16)


def _glu_kernel(a_ref, wa_ref, wb_ref, r_ref, o_ref, wa_s, wb_s):
    _cache_weights(pl.program_id(1) == 0, ((wa_ref, wa_s), (wb_ref, wb_s)))
    a = a_ref[...]
    va = jnp.dot(a, wa_s[...], preferred_element_type=F32)
    vb = jnp.dot(a, wb_s[...], preferred_element_type=F32)
    o_ref[...] = r_ref[...] + va * jax.nn.sigmoid(vb)


def _glu_residual(a, w, res, tm=512, tn=512):
    m, k = a.shape
    n = w.shape[1] // 2
    nb = n // tn
    return pl.pallas_call(
        _glu_kernel,
        out_shape=jax.ShapeDtypeStruct((m, n), F32),
        grid=(nb, m // tm),
        in_specs=[pl.BlockSpec((tm, k), lambda j, i: (i, 0)),
                  pl.BlockSpec((k, tn), lambda j, i: (0, j)),
                  pl.BlockSpec((k, tn), lambda j, i: (0, j + nb)),
                  pl.BlockSpec((tm, tn), lambda j, i: (i, j))],
        out_specs=pl.BlockSpec((tm, tn), lambda j, i: (i, j)),
        scratch_shapes=[pltpu.VMEM((k, tn), BF16), pltpu.VMEM((k, tn), BF16)],
        compiler_params=_params("arbitrary", "arbitrary"),
        name="glu_residual",
    )(a, w, w, res)


def _swiglu_up_kernel(a_ref, wg_ref, wu_ref, o_ref, wg_s, wu_s):
    _cache_weights(pl.program_id(1) == 0, ((wg_ref, wg_s), (wu_ref, wu_s)))
    a = a_ref[...]
    vg = jnp.dot(a, wg_s[...], preferred_element_type=F32)
    vu = jnp.dot(a, wu_s[...], preferred_element_type=F32)
    o_ref[...] = (jax.nn.silu(vg) * vu).astype(o_ref.dtype)


def _swiglu_up(a, w_gu, tm=512, tn=512):
    m, k = a.shape
    f = w_gu.shape[1] // 2
    nb = f // tn
    return pl.pallas_call(
        _swiglu_up_kernel,
        out_shape=jax.ShapeDtypeStruct((m, f), BF16),
        grid=(nb, m // tm),
        in_specs=[pl.BlockSpec((tm, k), lambda j, i: (i, 0)),
                  pl.BlockSpec((k, tn), lambda j, i: (0, j)),
                  pl.BlockSpec((k, tn), lambda j, i: (0, j + nb))],
        out_specs=pl.BlockSpec((tm, tn), lambda j, i: (i, j)),
        scratch_shapes=[pltpu.VMEM((k, tn), BF16), pltpu.VMEM((k, tn), BF16)],
        compiler_params=_params("arbitrary", "arbitrary"),
        name="swiglu_up",
    )(a, w_gu, w_gu)


def _mm_res_kernel(a_ref, w_ref, r_ref, o_ref, w_s):
    _cache_weights(pl.program_id(1) == 0, ((w_ref, w_s),))
    o_ref[...] = r_ref[...] + jnp.dot(a_ref[...], w_s[...], preferred_element_type=F32)


def _matmul_residual(a, w, res, tm=512, tn=512):
    m, k = a.shape
    n = w.shape[1]
    return pl.pallas_call(
        _mm_res_kernel,
        out_shape=jax.ShapeDtypeStruct((m, n), F32),
        grid=(n // tn, m // tm),
        in_specs=[pl.BlockSpec((tm, k), lambda j, i: (i, 0)),
                  pl.BlockSpec((k, tn), lambda j, i: (0, j)),
                  pl.BlockSpec((tm, tn), lambda j, i: (i, j))],
        out_specs=pl.BlockSpec((tm, tn), lambda j, i: (i, j)),
        scratch_shapes=[pltpu.VMEM((k, tn), BF16)],
        compiler_params=_params("arbitrary", "arbitrary"),
        name="matmul_residual",
    )(a, w, res)


def _rope_kernel(pos_ref, inv_ref, cos_ref, sa_ref, sb_ref):
    ang = pos_ref[...].astype(F32) * inv_ref[...]
    c = jnp.cos(ang)
    s = jnp.sin(ang)
    lane = lax.broadcasted_iota(jnp.int32, ang.shape, 1)
    first_half = lane < ROPE_DIM // 2
    cos_ref[...] = c
    sa_ref[...] = jnp.where(first_half, -s, 0.0)
    sb_ref[...] = jnp.where(first_half, 0.0, s)


def _rope_tables(pos, tm=512):
    n = pos.shape[0]
    tm = min(tm, n)
    half = ROPE_DIM // 2
    inv = jnp.power(ROPE_THETA, -jnp.arange(half, dtype=F32) / half)
    inv = jnp.concatenate([inv, inv, jnp.zeros((LANES - ROPE_DIM,), F32)]).reshape(1, LANES)
    spec = pl.BlockSpec((tm, LANES), lambda i: (i, 0))
    return pl.pallas_call(
        _rope_kernel,
        out_shape=[jax.ShapeDtypeStruct((n, LANES), F32)] * 3,
        grid=(n // tm,),
        in_specs=[pl.BlockSpec((tm, 1), lambda i: (i, 0)),
                  pl.BlockSpec((1, LANES), lambda i: (0, 0))],
        out_specs=[spec, spec, spec],
        compiler_params=_params("arbitrary"),
        name="rope_tables",
    )(pos.reshape(n, 1), inv)


def _rope(y, c, sa, sb):
    half = ROPE_DIM // 2
    return y * c + pltpu.roll(y, LANES - half, 1) * sa + pltpu.roll(y, half, 1) * sb


def _head_norm(x, gain):
    return x * lax.rsqrt(jnp.mean(x * x, axis=-1, keepdims=True) + RMS_EPS) * gain


def _nsa_proj_kernel(a_ref, w_ref, gain_ref, cos_ref, sa_ref, sb_ref, o_ref, w_s, *, norm_tiles, sub_rows):
    j = pl.program_id(0)
    _cache_weights(pl.program_id(1) == 0, ((w_ref, w_s),))
    is_norm = functools.reduce(jnp.logical_or, [j == t for t in norm_tiles])

    @pl.when(is_norm)
    def _():
        gain = gain_ref[0]
        for r0 in range(0, a_ref.shape[0], sub_rows):
            rows = slice(r0, r0 + sub_rows)
            acc = jnp.dot(a_ref[rows, :], w_s[...], preferred_element_type=F32)
            c, sa, sb = cos_ref[rows, :], sa_ref[rows, :], sb_ref[rows, :]
            for hh in range(acc.shape[1] // HEAD_DIM):
                sl = slice(HEAD_DIM * hh, HEAD_DIM * (hh + 1))
                o_ref[rows, sl] = _rope(_head_norm(acc[:, sl], gain), c, sa, sb).astype(o_ref.dtype)

    @pl.when(jnp.logical_not(is_norm))
    def _():
        o_ref[...] = jnp.dot(a_ref[...], w_s[...], preferred_element_type=F32).astype(o_ref.dtype)


def _nsa_proj(u, w_in, q_gain, k_gain, rope, tm=1024, tn=512, sub_rows=256):
    m, k = u.shape
    q_dim = GQA_GROUP * N_KV_HEADS * HEAD_DIM
    kv_dim = N_KV_HEADS * HEAD_DIM
    assert kv_dim == tn
    n_q = q_dim // tn
    n_tiles = n_q + 6
    ones = jnp.ones((HEAD_DIM,), F32)
    q_scaled = q_gain.astype(F32) * (HEAD_DIM ** -0.5 * math.log2(math.e))
    gains = jnp.stack([q_scaled] * n_q + [ones, ones, k_gain[1], ones, k_gain[2], ones]).reshape(n_tiles, 1, HEAD_DIM)
    norm_tiles = tuple(range(n_q)) + (n_q + 2, n_q + 4)
    tab = pl.BlockSpec((tm, LANES), lambda j, i: (i, 0))
    return pl.pallas_call(
        functools.partial(_nsa_proj_kernel, norm_tiles=norm_tiles, sub_rows=min(sub_rows, tm)),
        out_shape=jax.ShapeDtypeStruct((m, n_tiles * tn), BF16),
        grid=(n_tiles, m // tm),
        in_specs=[pl.BlockSpec((tm, k), lambda j, i: (i, 0)),
                  pl.BlockSpec((k, tn), lambda j, i: (0, j)),
                  pl.BlockSpec((1, 1, HEAD_DIM), lambda j, i: (j, 0, 0)),
                  tab, tab, tab],
        out_specs=pl.BlockSpec((tm, tn), lambda j, i: (i, j)),
        scratch_shapes=[pltpu.VMEM((k, tn), BF16)],
        compiler_params=_params("arbitrary", "arbitrary"),
        name="nsa_proj",
    )(u, w_in, gains, *rope)


def _gate_kernel(a_ref, w_ref, o_ref):
    o_ref[...] = jax.nn.sigmoid(jnp.dot(a_ref[...], w_ref[...].astype(BF16), preferred_element_type=F32))


def _nsa_gates(u, w_gate, tm=512):
    m, k = u.shape
    n = w_gate.shape[1]
    w_pad = jnp.pad(w_gate, ((0, 0), (0, LANES - n)))
    return pl.pallas_call(
        _gate_kernel,
        out_shape=jax.ShapeDtypeStruct((m, LANES), F32),
        grid=(m // tm,),
        in_specs=[pl.BlockSpec((tm, k), lambda i: (i, 0)),
                  pl.BlockSpec((k, LANES), lambda i: (0, 0))],
        out_specs=pl.BlockSpec((tm, LANES), lambda i: (i, 0)),
        compiler_params=_params("arbitrary"),
        name="nsa_gates",
    )(u, w_pad)


def _compress_kernel(*refs, is_key):
    if is_key:
        ca_ref, cb_ref, pe_ref, w1_ref, w2_ref, gain_ref, cos_ref, sa_ref, sb_ref, o_ref = refs
    else:
        ca_ref, cb_ref, pe_ref, w1_ref, w2_ref, o_ref = refs
    half = w1_ref.shape[0] // 2
    pe = pe_ref[...]
    xa = (ca_ref[0].astype(F32) + pe[:, :half]).astype(BF16)
    xb = (cb_ref[0].astype(F32) + pe[:, half:]).astype(BF16)
    hid = (jnp.dot(xa, w1_ref[:half, :].astype(BF16), preferred_element_type=F32)
           + jnp.dot(xb, w1_ref[half:, :].astype(BF16), preferred_element_type=F32))
    out = jnp.dot(jax.nn.gelu(hid).astype(BF16), w2_ref[...].astype(BF16), preferred_element_type=F32)
    if is_key:
        out = _rope(_head_norm(out, gain_ref[...]), cos_ref[...], sa_ref[...], sb_ref[...])
    o_ref[0] = out.astype(o_ref.dtype)


def _compress(t, pe, w1, w2, key_extras=None):
    seq = t.shape[0]
    nc = seq // CMP_STRIDE
    width = CMP_STRIDE * HEAD_DIM
    ca = t.reshape(nc, CMP_STRIDE, N_KV_HEADS, HEAD_DIM).transpose(2, 0, 1, 3).reshape(N_KV_HEADS, nc, width)
    cb = jnp.concatenate([ca[:, 1:], jnp.zeros((N_KV_HEADS, 1, width), ca.dtype)], axis=1)
    blk = pl.BlockSpec((1, nc, width), lambda h: (h, 0, 0))
    full = lambda a: pl.BlockSpec(a.shape, lambda h: (0,) * a.ndim)
    args = [ca, cb, pe.reshape(1, CMP_BLOCK * HEAD_DIM), w1, w2]
    if key_extras is not None:
        args += list(key_extras)
    return pl.pallas_call(
        functools.partial(_compress_kernel, is_key=key_extras is not None),
        out_shape=jax.ShapeDtypeStruct((N_KV_HEADS, nc, HEAD_DIM), BF16),
        grid=(N_KV_HEADS,),
        in_specs=[blk, blk] + [full(a) for a in args[2:]],
        out_specs=pl.BlockSpec((1, nc, HEAD_DIM), lambda h: (h, 0, 0)),
        compiler_params=_params("arbitrary"),
        name="nsa_compress_k" if key_extras is not None else "nsa_compress_v",
    )(*args)


def _dot_nt(a, b):
    return lax.dot_general(a, b, (((1,), (1,)), ((), ())), preferred_element_type=F32)


def _split3(x):
    hi = x.astype(BF16)
    r1 = x - hi.astype(F32)
    mid = r1.astype(BF16)
    lo = (r1 - mid.astype(F32)).astype(BF16)
    return hi, mid, lo


def _nsa_attn_kernel(q_ref, kc_ref, vct_ref, ks_ref, vst_ref, kw_ref, vwt_ref, gate_ref, blk_ref, o_ref,
                     acc_ref, mix_ref, m_ref, l_ref, qa_ref, s0_ref, s1_ref, pc_ref, sw_ref, pw_ref,
                     *, seq, tk):
    t0 = pl.program_id(1) * Q_BLOCK
    nc = kc_ref.shape[1]
    ns = seq // SEL_BLOCK
    grp = GQA_GROUP
    cols = grp * Q_BLOCK
    sel_shift = int(math.log2(SEL_BLOCK))
    q = q_ref[...].astype(F32)
    qt = jnp.concatenate([q[:, HEAD_DIM * g:HEAD_DIM * (g + 1)].T for g in range(grp)], axis=1).astype(BF16)
    t_row = t0 + lax.broadcasted_iota(jnp.int32, (1, Q_BLOCK), 1)

    def heads(x):
        return jnp.concatenate([x] * grp, axis=1)

    span = WINDOW + Q_BLOCK
    w0 = pl.multiple_of(jnp.maximum(t0 - WINDOW, 0), Q_BLOCK)
    gate = gate_ref[0, 0]
    n_idx = lax.broadcasted_iota(jnp.int32, (nc, Q_BLOCK), 0)
    ok_c = (n_idx * CMP_STRIDE + (CMP_BLOCK - 1) <= t_row) & (n_idx < nc - 1)
    sb_c = jnp.dot(kc_ref[0], qt, preferred_element_type=F32) + heads(jnp.where(ok_c, 0.0, NEG))
    m_c = jnp.max(sb_c, axis=0, keepdims=True)
    e_c = jnp.exp2(sb_c - m_c)
    den_c = jnp.maximum(jnp.sum(e_c, axis=0, keepdims=True), 1e-30)
    p_c = e_c * jnp.where(m_c > 0.5 * NEG, 1.0 / den_c, 0.0)
    pc_ref[...] = p_c.astype(BF16)
    imp = p_c[:, :Q_BLOCK]
    for g in range(1, grp):
        imp = imp + p_c[:, Q_BLOCK * g:Q_BLOCK * (g + 1)]

    ratio = SEL_BLOCK // CMP_STRIDE
    d = (lax.broadcasted_iota(jnp.int32, (ns, nc), 1)
         - ratio * lax.broadcasted_iota(jnp.int32, (ns, nc), 0))
    overlap = jnp.zeros((ns, nc), F32)
    for n in range(CMP_BLOCK // CMP_STRIDE):
        overlap = overlap + jnp.where((d - n >= 0) & (d - n < ratio), 1.0, 0.0)
    overlap = overlap.astype(BF16)
    p_slc = sum(jnp.dot(overlap, part, preferred_element_type=F32) for part in _split3(imp))

    mix_ref[...] = gate[0:1] * jnp.dot(vct_ref[0], pc_ref[...], preferred_element_type=F32)
    rel = t_row - (w0 + lax.broadcasted_iota(jnp.int32, (span, Q_BLOCK), 0))
    sw_ref[...] = (jnp.dot(kw_ref[pl.ds(w0, span), :], qt, preferred_element_type=F32)
                   + heads(jnp.where((rel >= 0) & (rel < WINDOW), 0.0, NEG)))
    m_w = jnp.max(sw_ref[...], axis=0, keepdims=True)

    j_idx = lax.broadcasted_iota(jnp.int32, (ns, Q_BLOCK), 0)
    j_f = j_idx.astype(F32)
    dist = jnp.right_shift(t_row, sel_shift) - j_idx
    forced = (j_idx == 0) | ((dist >= 0) & (dist < SEL_LOCAL))
    score = jnp.where(forced, jnp.inf, jnp.where(dist >= 0, p_slc, -jnp.inf))
    sel = jnp.zeros((ns, Q_BLOCK), F32)
    k_top = min(SEL_TOPK, ns)
    pack = 2 * SUBLANES
    cuts = [span // pack * r // k_top * pack for r in range(k_top + 1)]
    den_w = jnp.zeros((1, cols), F32)
    for r in range(k_top):
        top = jnp.max(score, axis=0, keepdims=True)
        idx = jnp.min(jnp.where(score == top, j_f, float(ns)), axis=0, keepdims=True)
        pick = j_f == idx
        sel = jnp.where(pick, 1.0, sel)
        score = jnp.where(pick, -jnp.inf, score)
        if cuts[r + 1] > cuts[r]:
            e_w = jnp.exp2(sw_ref[cuts[r]:cuts[r + 1], :] - m_w)
            den_w = den_w + jnp.sum(e_w, axis=0, keepdims=True)
            pw_ref[cuts[r]:cuts[r + 1], :] = e_w.astype(BF16)
    o_w = jnp.dot(vwt_ref[0, :, pl.ds(w0, span)], pw_ref[...], preferred_element_type=F32)
    mix_ref[...] += (gate[2:3] * (1.0 / den_w)) * o_w

    qa_ref[...] = jnp.concatenate([qt, heads(jnp.where(sel > 0.0, 0.0, NEG).astype(BF16))], axis=0)
    acc_ref[...] = jnp.zeros_like(acc_ref)
    m_ref[...] = jnp.full_like(m_ref, NEG)
    l_ref[...] = jnp.zeros_like(l_ref)

    def score_tile(kt, dst):
        k0 = pl.multiple_of(kt * tk, tk)
        k_aug = jnp.concatenate([ks_ref[pl.ds(k0, tk), :], blk_ref[pl.ds(k0, tk), :]], axis=1)
        dst[...] = jnp.dot(k_aug, qa_ref[...], preferred_element_type=F32)

    def consume_tile(kt, src, diagonal):
        k0 = pl.multiple_of(kt * tk, tk)
        sb = src[...]
        if diagonal:
            kpos = k0 + lax.broadcasted_iota(jnp.int32, (tk, Q_BLOCK), 0)
            sb = sb + heads(jnp.where(kpos <= t_row, 0.0, NEG))
        m_i = m_ref[...]
        m_new = jnp.maximum(m_i, jnp.max(sb, axis=0, keepdims=True))
        e = jnp.exp2(sb - m_new)
        alpha = jnp.exp2(m_i - m_new)
        m_ref[...] = m_new
        l_ref[...] = alpha * l_ref[...] + jnp.sum(e, axis=0, keepdims=True)
        acc_ref[...] = alpha * acc_ref[...] + jnp.dot(vst_ref[0, :, pl.ds(k0, tk)], e.astype(BF16),
                                                      preferred_element_type=F32)

    last = t0 // tk
    score_tile(0, s0_ref)

    def tile_pair(i, _):
        score_tile(2 * i + 1, s1_ref)
        consume_tile(2 * i, s0_ref, False)
        score_tile(2 * i + 2, s0_ref)
        consume_tile(2 * i + 1, s1_ref, False)
        return 0

    lax.fori_loop(0, last // 2, tile_pair, 0)

    @pl.when(last % 2 == 1)
    def _():
        score_tile(last, s1_ref)
        consume_tile(last - 1, s0_ref, False)
        consume_tile(last, s1_ref, True)

    @pl.when(last % 2 == 0)
    def _():
        consume_tile(last, s0_ref, True)

    o_s = acc_ref[...] * (1.0 / jnp.maximum(l_ref[...], 1e-30))

    mixed = mix_ref[...] + gate_ref[0, 0, 1:2] * o_s
    for g in range(grp):
        o_ref[:, HEAD_DIM * g:HEAD_DIM * (g + 1)] = mixed[:, Q_BLOCK * g:Q_BLOCK * (g + 1)].T.astype(o_ref.dtype)


def _nsa_attention(proj, kcmp, vcmp_t, vsl_t, vw_t, gates, tk=512):
    seq = proj.shape[0]
    tk = min(tk, seq)
    q_dim = GQA_GROUP * N_KV_HEADS * HEAD_DIM
    kv_blocks = N_KV_HEADS
    first = q_dim // HEAD_DIM + 2 * kv_blocks
    nc = kcmp.shape[1]
    cols = GQA_GROUP * Q_BLOCK

    def k_spec(which):
        return pl.BlockSpec((seq, HEAD_DIM), lambda h, qb: (0, first + which * kv_blocks + h))

    def vt_spec(n):
        return pl.BlockSpec((1, HEAD_DIM, n), lambda h, qb: (h, 0, 0))

    q_spec = pl.BlockSpec((Q_BLOCK, GQA_GROUP * HEAD_DIM), lambda h, qb: (qb, h))
    ns = seq // SEL_BLOCK
    key_block = (jnp.arange(seq, dtype=jnp.int32)[:, None] // SEL_BLOCK
                 == jnp.arange(ns, dtype=jnp.int32)[None, :]).astype(BF16)
    return pl.pallas_call(
        functools.partial(_nsa_attn_kernel, seq=seq, tk=tk),
        out_shape=jax.ShapeDtypeStruct((seq, q_dim), BF16),
        grid=(N_KV_HEADS, seq // Q_BLOCK),
        in_specs=[q_spec, pl.BlockSpec((1, nc, HEAD_DIM), lambda h, qb: (h, 0, 0)), vt_spec(nc),
                  k_spec(0), vt_spec(seq), k_spec(2), vt_spec(seq),
                  pl.BlockSpec((1, 1, 3, cols), lambda h, qb: (h, qb, 0, 0)),
                  pl.BlockSpec((seq, ns), lambda h, qb: (0, 0))],
        out_specs=q_spec,
        scratch_shapes=[pltpu.VMEM((HEAD_DIM, cols), F32), pltpu.VMEM((HEAD_DIM, cols), F32),
                        pltpu.VMEM((1, cols), F32), pltpu.VMEM((1, cols), F32),
                        pltpu.VMEM((HEAD_DIM + ns, cols), BF16),
                        pltpu.VMEM((tk, cols), F32), pltpu.VMEM((tk, cols), F32),
                        pltpu.VMEM((nc, cols), BF16), pltpu.VMEM((WINDOW + Q_BLOCK, cols), F32),
                        pltpu.VMEM((WINDOW + Q_BLOCK, cols), BF16)],
        compiler_params=_params("arbitrary", "arbitrary"),
        name="nsa_attention",
    )(proj, kcmp, vcmp_t, proj, vsl_t, proj, vw_t, gates, key_block)


def _nsa_mixer(u, positions, w_in, q_gain, k_gain, pe_k, pe_v, ck_w1, ck_w2, cv_w1, cv_w2):
    seq = u.shape[0]
    q_dim = GQA_GROUP * N_KV_HEADS * HEAD_DIM
    kv_dim = N_KV_HEADS * HEAD_DIM
    n_main = q_dim + 6 * kv_dim
    nc = seq // CMP_STRIDE
    rope = _rope_tables(positions)
    proj = _nsa_proj(u, w_in, q_gain, k_gain, rope)
    gate = _nsa_gates(u, w_in[:, n_main:])
    gates = (gate[:, :3 * N_KV_HEADS * GQA_GROUP].reshape(seq // Q_BLOCK, Q_BLOCK, 3, N_KV_HEADS, GQA_GROUP)
             .transpose(3, 0, 2, 4, 1).reshape(N_KV_HEADS, seq // Q_BLOCK, 3, GQA_GROUP * Q_BLOCK))

    def keys_last(cols):
        return cols.reshape(seq, N_KV_HEADS, HEAD_DIM).transpose(1, 2, 0)

    pos_cmp = jnp.concatenate([positions[CMP_BLOCK - 1::CMP_STRIDE][:nc - 1], jnp.zeros((1,), positions.dtype)])
    rope_cmp = _rope_tables(pos_cmp)
    kcmp = _compress(proj[:, q_dim:q_dim + kv_dim], pe_k, ck_w1, ck_w2,
                     key_extras=(k_gain[0].reshape(1, HEAD_DIM),) + tuple(rope_cmp))
    vcmp = _compress(proj[:, q_dim + kv_dim:q_dim + 2 * kv_dim], pe_v, cv_w1, cv_w2)
    vsl_t = keys_last(proj[:, q_dim + 3 * kv_dim:q_dim + 4 * kv_dim])
    vw_t = keys_last(proj[:, q_dim + 5 * kv_dim:q_dim + 6 * kv_dim])
    return _nsa_attention(proj, kcmp, vcmp.transpose(0, 2, 1), vsl_t, vw_t, gates)


def _router_kernel(x_ref, g_ref, w_ref, b_ref, u_ref, r_ref):
    x = x_ref[...]
    u = x * lax.rsqrt(jnp.mean(x * x, axis=-1, keepdims=True) + RMS_EPS) * g_ref[...]
    u_ref[...] = u
    uh, um, _ = _split3(u)
    wh, wm, _ = _split3(w_ref[...])
    logits = (jnp.dot(uh, wh, preferred_element_type=F32) + jnp.dot(uh, wm, preferred_element_type=F32)
              + jnp.dot(um, wh, preferred_element_type=F32)) + b_ref[...]
    lane = lax.broadcasted_iota(jnp.int32, logits.shape, 1).astype(F32)
    lg = jnp.where(lane < N_EXPERTS, logits, -jnp.inf)
    v1 = jnp.max(lg, axis=-1, keepdims=True)
    i1 = jnp.min(jnp.where(lg == v1, lane, float(LANES)), axis=-1, keepdims=True)
    lg = jnp.where(lane == i1, -jnp.inf, lg)
    v2 = jnp.max(lg, axis=-1, keepdims=True)
    i2 = jnp.min(jnp.where(lg == v2, lane, float(LANES)), axis=-1, keepdims=True)
    e2 = jnp.exp(v2 - v1)
    den = 1.0 + e2
    r_ref[...] = jnp.where(lane == 0, i1, jnp.where(lane == 1, i2, jnp.where(
        lane == 2, 1.0 / den, jnp.where(lane == 3, e2 / den, 0.0))))


def _router(h, gain, w_router, b_router, tm=256):
    m, d = h.shape
    w_pad = jnp.pad(w_router.astype(F32), ((0, 0), (0, LANES - N_EXPERTS)))
    b_pad = jnp.pad(b_router.astype(F32), (0, LANES - N_EXPERTS)).reshape(1, LANES)
    return pl.pallas_call(
        _router_kernel,
        out_shape=[jax.ShapeDtypeStruct((m, d), F32), jax.ShapeDtypeStruct((m, LANES), F32)],
        grid=(m // tm,),
        in_specs=[pl.BlockSpec((tm, d), lambda i: (i, 0)),
                  pl.BlockSpec((1, d), lambda i: (0, 0)),
                  pl.BlockSpec((d, LANES), lambda i: (0, 0)),
                  pl.BlockSpec((1, LANES), lambda i: (0, 0))],
        out_specs=[pl.BlockSpec((tm, d), lambda i: (i, 0)), pl.BlockSpec((tm, LANES), lambda i: (i, 0))],
        compiler_params=_params("arbitrary"),
        name="moe_router",
    )(h, gain.reshape(1, d), w_pad, b_pad)


def _row_copy(src_hbm, row, dst, r, sem):
    return pltpu.make_async_copy(src_hbm.at[pl.ds(row, 1), :], dst.at[pl.ds(r, 1), :], sem)


def _gather_kernel(idx_ref, used_ref, src_hbm, o_ref, buf, sem):
    rows = o_ref.shape[0]
    i = pl.program_id(0)
    n_used = used_ref[0]

    def issue(blk):
        slot = blk % 2

        def start(g, _):
            for u in range(DMA_UNROLL):
                r = g * DMA_UNROLL + u
                _row_copy(src_hbm, idx_ref[blk * rows + r], buf.at[slot], r, sem.at[slot]).start(priority=u % 2)
            return 0

        lax.fori_loop(0, rows // DMA_UNROLL, start, 0)

    @pl.when(i == 0)
    def _():
        issue(i)

    @pl.when(i + 1 < n_used)
    def _():
        issue(i + 1)

    @pl.when(i < n_used)
    def _():
        slot = i % 2

        def wait(r, _):
            _row_copy(src_hbm, 0, buf.at[slot], r, sem.at[slot]).wait()
            return 0

        lax.fori_loop(0, rows, wait, 0, unroll=DMA_UNROLL)
        o_ref[...] = buf[slot].astype(o_ref.dtype)

    @pl.when(i >= n_used)
    def _():
        o_ref[...] = jnp.zeros_like(o_ref)


def _gather_rows(src, idx, n_used, out_dtype, rows=MOE_ROWS):
    n = idx.shape[0]
    d = src.shape[1]
    return pl.pallas_call(
        _gather_kernel,
        out_shape=jax.ShapeDtypeStruct((n, d), out_dtype),
        grid_spec=pltpu.PrefetchScalarGridSpec(
            num_scalar_prefetch=2,
            grid=(n // rows,),
            in_specs=[pl.BlockSpec(memory_space=pl.ANY)],
            out_specs=pl.BlockSpec((rows, d), lambda i, idx, used: (i, 0)),
            scratch_shapes=[pltpu.VMEM((2, rows, d), src.dtype), pltpu.SemaphoreType.DMA((2,))]),
        compiler_params=_params("arbitrary"),
        name="moe_gather",
    )(idx, n_used, src)


def _block_state(be_ref, used_ref, i):
    changed = (i == 0) | (be_ref[i] != be_ref[jnp.maximum(i - 1, 0)])
    used = i < used_ref[0]
    return used, used & changed


def _last_used(i, used):
    return jnp.minimum(i, used[0] - 1)


def _stream_expert_weights(first, be_ref, rix_ref, rune_ref, nrun_ref, n_tiles, copies, cast):
    j = pl.program_id(0)
    i = pl.program_id(1)

    @pl.when(first)
    def _():
        rix = rix_ref[i]
        n_runs = nrun_ref[0]
        g = j * n_runs + rix
        slot = g % 2

        @pl.when(g == 0)
        def _():
            for c in copies(be_ref[i], j, slot):
                c.start()

        more = rix + 1 < n_runs
        e_next = jnp.where(more, rune_ref[jnp.minimum(rix + 1, N_EXPERTS - 1)], rune_ref[0])
        j_next = jnp.where(more, j, j + 1)

        @pl.when(j_next < n_tiles)
        def _():
            for c in copies(e_next, j_next, 1 - slot):
                c.start()

        for c in copies(be_ref[i], j, slot):
            c.wait()
        cast(slot)


def _moe_up_kernel(be_ref, used_ref, rix_ref, rune_ref, nrun_ref, x_ref, w_hbm, o_ref, wbuf, wg_s, wu_s, sem,
                   *, nb, tn):
    used, first = _block_state(be_ref, used_ref, pl.program_id(1))

    def copies(e, jj, slot):
        return [pltpu.make_async_copy(w_hbm.at[e, :, pl.ds(pl.multiple_of((jj + m * nb) * tn, tn), tn)],
                                      wbuf.at[slot, m], sem.at[slot, m]) for m in range(2)]

    def cast(slot):
        wg_s[...] = wbuf[slot, 0].astype(BF16)
        wu_s[...] = wbuf[slot, 1].astype(BF16)

    _stream_expert_weights(first, be_ref, rix_ref, rune_ref, nrun_ref, nb, copies, cast)

    @pl.when(used)
    def _():
        a = x_ref[...]
        vg = jnp.dot(a, wg_s[...], preferred_element_type=F32)
        vu = jnp.dot(a, wu_s[...], preferred_element_type=F32)
        o_ref[...] = (jax.nn.silu(vg) * vu).astype(o_ref.dtype)

    @pl.when(jnp.logical_not(used))
    def _():
        o_ref[...] = jnp.zeros_like(o_ref)


def _moe_up(x_rows, sched, w_gu, tn=512, rows=MOE_ROWS):
    n, k = x_rows.shape
    f = w_gu.shape[2] // 2
    nb = f // tn
    return pl.pallas_call(
        functools.partial(_moe_up_kernel, nb=nb, tn=tn),
        out_shape=jax.ShapeDtypeStruct((n, f), BF16),
        grid_spec=pltpu.PrefetchScalarGridSpec(
            num_scalar_prefetch=len(sched),
            grid=(nb, n // rows),
            in_specs=[pl.BlockSpec((rows, k), lambda j, i, be, nu, *_: (_last_used(i, nu), 0)),
                      pl.BlockSpec(memory_space=pl.ANY)],
            out_specs=pl.BlockSpec((rows, tn), lambda j, i, *_: (i, j)),
            scratch_shapes=[pltpu.VMEM((2, 2, k, tn), F32), pltpu.VMEM((k, tn), BF16),
                            pltpu.VMEM((k, tn), BF16), pltpu.SemaphoreType.DMA((2, 2))]),
        compiler_params=_params("arbitrary", "arbitrary"),
        name="moe_up",
    )(*sched, x_rows, w_gu)


def _moe_down_kernel(be_ref, used_ref, rix_ref, rune_ref, nrun_ref, a_ref, w_hbm, o_ref, wbuf, w_s, sem,
                     *, nb, tn):
    used, first = _block_state(be_ref, used_ref, pl.program_id(1))

    def copies(e, jj, slot):
        return [pltpu.make_async_copy(w_hbm.at[e, :, pl.ds(pl.multiple_of(jj * tn, tn), tn)],
                                      wbuf.at[slot], sem.at[slot])]

    def cast(slot):
        w_s[...] = wbuf[slot].astype(BF16)

    _stream_expert_weights(first, be_ref, rix_ref, rune_ref, nrun_ref, nb, copies, cast)

    @pl.when(used)
    def _():
        o_ref[...] = jnp.dot(a_ref[...], w_s[...], preferred_element_type=F32)

    @pl.when(jnp.logical_not(used))
    def _():
        o_ref[...] = jnp.zeros_like(o_ref)


def _moe_down(act, sched, w_down, tn=512, rows=MOE_ROWS):
    n, k = act.shape
    d = w_down.shape[2]
    nb = d // tn
    return pl.pallas_call(
        functools.partial(_moe_down_kernel, nb=nb, tn=tn),
        out_shape=jax.ShapeDtypeStruct((n, d), F32),
        grid_spec=pltpu.PrefetchScalarGridSpec(
            num_scalar_prefetch=len(sched),
            grid=(nb, n // rows),
            in_specs=[pl.BlockSpec((rows, k), lambda j, i, be, nu, *_: (_last_used(i, nu), 0)),
                      pl.BlockSpec(memory_space=pl.ANY)],
            out_specs=pl.BlockSpec((rows, tn), lambda j, i, *_: (i, j)),
            scratch_shapes=[pltpu.VMEM((2, k, tn), F32), pltpu.VMEM((k, tn), BF16),
                            pltpu.SemaphoreType.DMA((2,))]),
        compiler_params=_params("arbitrary", "arbitrary"),
        name="moe_down",
    )(*sched, act, w_down)


def _combine_kernel(dest_ref, h_ref, r_ref, rows_hbm, o_ref, buf, sem):
    tm = h_ref.shape[0]
    base = pl.program_id(0) * tm

    def start(r, _):
        for k in range(2):
            _row_copy(rows_hbm, dest_ref[2 * (base + r) + k], buf.at[k], r, sem.at[k]).start(priority=k)
        return 0

    def wait(r, _):
        for k in range(2):
            _row_copy(rows_hbm, 0, buf.at[k], r, sem.at[k]).wait()
        return 0

    lax.fori_loop(0, tm, start, 0, unroll=DMA_UNROLL)
    lax.fori_loop(0, tm, wait, 0, unroll=DMA_UNROLL)
    w = r_ref[...]
    o_ref[...] = h_ref[...] + (w[:, 2:3] * buf[0] + w[:, 3:4] * buf[1])


def _moe_combine(h, route, out_rows, dest, tm=256):
    m, d = h.shape
    return pl.pallas_call(
        _combine_kernel,
        out_shape=jax.ShapeDtypeStruct((m, d), F32),
        grid_spec=pltpu.PrefetchScalarGridSpec(
            num_scalar_prefetch=1,
            grid=(m // tm,),
            in_specs=[pl.BlockSpec((tm, d), lambda i, dest: (i, 0)),
                      pl.BlockSpec((tm, LANES), lambda i, dest: (i, 0)),
                      pl.BlockSpec(memory_space=pl.ANY)],
            out_specs=pl.BlockSpec((tm, d), lambda i, dest: (i, 0)),
            scratch_shapes=[pltpu.VMEM((2, tm, d), F32), pltpu.SemaphoreType.DMA((2,))]),
        compiler_params=_params("arbitrary"),
        name="moe_combine",
    )(dest.reshape(-1), h, route, out_rows)


def _moe_layout(top_e, rows=MOE_ROWS):
    n_tok = top_e.shape[0]
    e_flat = top_e.reshape(-1)
    onehot = (e_flat[:, None] == jnp.arange(N_EXPERTS, dtype=jnp.int32)[None, :]).astype(jnp.int32)
    csum = jnp.cumsum(onehot, axis=0)
    rank = jnp.take_along_axis(csum, e_flat[:, None], axis=1)[:, 0] - 1
    counts = csum[-1]
    padded = (counts + rows - 1) // rows * rows
    pad_end = jnp.cumsum(padded)
    dest = (pad_end - padded)[e_flat] + rank
    n_rows = e_flat.shape[0] + N_EXPERTS * rows
    t_flat = jnp.repeat(jnp.arange(n_tok, dtype=jnp.int32), top_e.shape[1])
    row_tok = jnp.zeros((n_rows,), jnp.int32).at[dest].set(t_flat)
    n_blk = n_rows // rows
    blk_start = jnp.arange(n_blk, dtype=jnp.int32) * rows
    blk_e = jnp.minimum(jnp.sum(blk_start[:, None] >= pad_end[None, :], axis=1), N_EXPERTS - 1).astype(jnp.int32)
    n_used = (pad_end[-1:] // rows).astype(jnp.int32)
    first = (jnp.arange(n_blk) < n_used[0]) & (blk_e != jnp.concatenate([blk_e[:1] - 1, blk_e[:-1]]))
    run_ix = (jnp.cumsum(first) - 1).astype(jnp.int32)
    in_run = first[:, None] & (run_ix[:, None] == jnp.arange(N_EXPERTS, dtype=jnp.int32)[None, :])
    run_e = jnp.sum(jnp.where(in_run, blk_e[:, None], 0), axis=0).astype(jnp.int32)
    n_runs = jnp.sum(first).astype(jnp.int32).reshape(1)
    sched = (blk_e, n_used, run_ix, run_e, n_runs)
    return row_tok, sched, dest.astype(jnp.int32).reshape(n_tok, -1)


def _moe_ffn_residual(h, gain, w_router, b_router, w_gu, w_down):
    u, route = _router(h, gain, w_router, b_router)
    top_e = route[:, :2].astype(jnp.int32)
    row_tok, sched, dest = _moe_layout(top_e)
    x_rows = _gather_rows(u, row_tok, sched[1], BF16)
    act = _moe_up(x_rows, sched, w_gu)
    out_rows = _moe_down(act, sched, w_down)
    return _moe_combine(h, route, out_rows, dest)


def kernel(x, positions, norm_mix, norm_ffn, s5_a_re, s5_a_im, s5_log_step, s5_b_re, s5_b_im, s5_c_re, s5_c_im, s5_d, s5_w_glu, nsa_w_in, nsa_q_gain, nsa_k_gain, nsa_pe_k, nsa_pe_v, nsa_ck_w1, nsa_ck_w2, nsa_cv_w1, nsa_cv_w2, nsa_w_out, ffn_w_gu, ffn_w_down, moe_w_router, moe_b_router, moe_w_gu, moe_w_down):
    bsz, seq, d = x.shape
    assert bsz == 1, "the scan and attention kernels take one sequence"
    h = x.reshape(seq, d)
    h = _layer_s5(h, norm_mix[0], norm_ffn[0], s5_a_re[0], s5_a_im[0], s5_log_step[0], s5_b_re[0],
                  s5_b_im[0], s5_c_re[0], s5_c_im[0], s5_d[0], s5_w_glu[0], ffn_w_gu[0], ffn_w_down[0])
    h = _layer_nsa(h, positions[0], norm_mix[1], norm_ffn[1], nsa_w_in[0], nsa_q_gain[0], nsa_k_gain[0],
                   nsa_pe_k[0], nsa_pe_v[0], nsa_ck_w1[0], nsa_ck_w2[0], nsa_cv_w1[0], nsa_cv_w2[0],
                   nsa_w_out[0], moe_w_router[0], moe_b_router[0], moe_w_gu[0], moe_w_down[0])
    return h.reshape(bsz, seq, d)


def _layer_nsa(h, positions, g_mix, g_ffn, w_in, q_gain, k_gain, pe_k, pe_v, ck_w1, ck_w2, cv_w1, cv_w2,
               w_out, w_router, b_router, w_gu, w_down):
    u = _rms_norm(h, g_mix, BF16)
    o = _nsa_mixer(u, positions, w_in, q_gain, k_gain, pe_k, pe_v, ck_w1, ck_w2, cv_w1, cv_w2)
    h = _matmul_residual(o, w_out, h)
    return _moe_ffn_residual(h, g_ffn, w_router, b_router, w_gu, w_down)


def _layer_s5(h, g_mix, g_ffn, a_re, a_im, log_step, b_re, b_im, c_re, c_im, d_skip, w_glu, w_gu, w_down):
    u = _rms_norm(h, g_mix, F32)
    g = _s5_mixer(u, a_re, a_im, log_step, b_re, b_im, c_re, c_im, d_skip)
    h = _glu_residual(g, w_glu, h)
    u = _rms_norm(h, g_ffn, BF16)
    act = _swiglu_up(u, w_gu)
    return _matmul_residual(act, w_down, h)
```

```python
import functools
import math

import jax
import jax.numpy as jnp
from jax import lax
from jax.experimental import pallas as pl
from jax.experimental.pallas import tpu as pltpu

F32 = jnp.float32
BF16 = jnp.bfloat16

RMS_EPS = 1e-6
S5_GROUP = 16
S5_STATE = 64
HEAD_DIM = 128
N_KV_HEADS = 4
GQA_GROUP = 4
ROPE_DIM = 32
ROPE_THETA = 500000.0
CMP_BLOCK = 32
CMP_STRIDE = 16
SEL_BLOCK = 64
SEL_TOPK = 16
SEL_LOCAL = 2
WINDOW = 512
Q_BLOCK = 128
N_EXPERTS = 8
NEG = -1e30

LANES = 128
SUBLANES = 8
VMEM_LIMIT = 56 * 1024 * 1024

S5_SLAB = 256
S5_SLAB_STATES = S5_SLAB // S5_GROUP * S5_STATE
S5_SUB = 64
S5_PAIR = 2
MOE_ROWS = 512
SUB_ROWS = 256
DMA_UNROLL = 8


def _params(*sem):
    return pltpu.CompilerParams(dimension_semantics=sem, vmem_limit_bytes=VMEM_LIMIT)


def _rms_kernel(x_ref, g_ref, o_ref):
    x = x_ref[...]
    ms = jnp.mean(x * x, axis=-1, keepdims=True)
    o_ref[...] = (x * lax.rsqrt(ms + RMS_EPS) * g_ref[...]).astype(o_ref.dtype)


def _rms_norm(x, gain, out_dtype, tm=512):
    m, d = x.shape
    return pl.pallas_call(
        _rms_kernel,
        out_shape=jax.ShapeDtypeStruct((m, d), out_dtype),
        grid=(m // tm,),
        in_specs=[pl.BlockSpec((tm, d), lambda i: (i, 0)),
                  pl.BlockSpec((1, d), lambda i: (0, 0))],
        out_specs=pl.BlockSpec((tm, d), lambda i: (i, 0)),
        compiler_params=_params("arbitrary"),
        name="rms_norm",
    )(x, gain.reshape(1, d))


def _s5_kernel(u_ref, b_ref, c_ref, lam_ref, ptab_ref, apow_ref, d_ref, o_ref,
               xs_ref, carry_ref, up_ref, us_ref, *, sub):
    n = S5_SLAB_STATES
    halves = S5_SLAB // LANES
    row = lax.broadcasted_iota(jnp.int32, (SUBLANES, n), 0)

    @pl.when(pl.program_id(1) == 0)
    def _():
        carry_ref[...] = jnp.zeros_like(carry_ref)

    def in_proj(s):
        for c in range(halves):
            us_ref[s * halves + c] = u_ref[:, S5_SLAB * s + LANES * c:S5_SLAB * s + LANES * (c + 1)]
        for i in range(sub):
            for c in range(halves):
                up_ref[s, SUBLANES * i:SUBLANES * (i + 1), LANES * c:LANES * (c + 1)] = (
                    us_ref[s * halves + c, pl.ds(i, SUBLANES, stride=sub), :])
        xs_ref[s] = jnp.dot(up_ref[s].astype(BF16), b_ref[s], preferred_element_type=F32)

    def scan(s):
        lam = lam_ref[s]
        lr, li = lam[:, :n], lam[:, n:]
        hr = hi = jnp.zeros((SUBLANES, n), F32)
        for i in range(sub):
            rows = slice(SUBLANES * i, SUBLANES * (i + 1))
            x = xs_ref[s, rows, :]
            hr, hi = lr * hr - li * hi + x[:, :n], lr * hi + li * hr + x[:, n:]
            xs_ref[s, rows, :] = jnp.concatenate([hr, hi], axis=1)
        er, ei = hr, hi

        cin = carry_ref[s]
        zr = jnp.where(row == 0, cin[:, :n], pltpu.roll(er, 1, 0))
        zi = jnp.where(row == 0, cin[:, n:], pltpu.roll(ei, 1, 0))
        apow = apow_ref[s]
        for k, d in enumerate((1, 2, 4)):
            ar = apow[SUBLANES * k:SUBLANES * (k + 1), :n]
            ai = apow[SUBLANES * k:SUBLANES * (k + 1), n:]
            sr = pltpu.roll(zr, d, 0)
            si = pltpu.roll(zi, d, 0)
            keep = row >= d
            zr, zi = (zr + jnp.where(keep, ar * sr - ai * si, 0.0),
                      zi + jnp.where(keep, ar * si + ai * sr, 0.0))
        a1r, a1i = apow[:SUBLANES, :n], apow[:SUBLANES, n:]
        nxt_r = a1r * zr - a1i * zi + er
        nxt_i = a1r * zi + a1i * zr + ei
        carry_ref[s] = jnp.concatenate(
            [jnp.broadcast_to(nxt_r[SUBLANES - 1:, :], (SUBLANES, n)),
             jnp.broadcast_to(nxt_i[SUBLANES - 1:, :], (SUBLANES, n))], axis=1)

        for i in range(sub):
            rows = slice(SUBLANES * i, SUBLANES * (i + 1))
            x = xs_ref[s, rows, :]
            p = ptab_ref[s, rows, :]
            pr, pi = p[:, :n], p[:, n:]
            xs_ref[s, rows, :] = jnp.concatenate([x[:, :n] + pr * zr - pi * zi,
                                                  x[:, n:] + pr * zi + pi * zr], axis=1)

    def out_proj(s):
        y = jnp.dot(xs_ref[s].astype(BF16), c_ref[s], preferred_element_type=F32)
        g = jax.nn.gelu(y + d_ref[:, S5_SLAB * s:S5_SLAB * (s + 1)] * up_ref[s])
        for c in range(halves):
            us_ref[s * halves + c] = g[:, LANES * c:LANES * (c + 1)]
        for j in range(SUBLANES):
            for c in range(halves):
                o_ref[sub * j:sub * (j + 1), S5_SLAB * s + LANES * c:S5_SLAB * s + LANES * (c + 1)] = (
                    us_ref[s * halves + c, pl.ds(j, sub, stride=SUBLANES), :].astype(o_ref.dtype))

    for s in range(S5_PAIR):
        in_proj(s)
    for s in range(S5_PAIR):
        scan(s)
        out_proj(s)


def _s5_tables(a_re, a_im, log_step, b_re, b_im, c_re, c_im, sub):
    g = a_re.shape[0]
    n_slab = g * S5_GROUP // S5_SLAB
    gl = S5_SLAB // S5_GROUP
    dt = jnp.exp(log_step.astype(F32))[:, None]
    ar = a_re.astype(F32)
    ai = a_im.astype(F32)
    mag = jnp.exp(ar * dt)
    lb_re = mag * jnp.cos(ai * dt)
    lb_im = mag * jnp.sin(ai * dt)
    den = ar * ar + ai * ai
    nr = lb_re - 1.0
    coef_re = (nr * ar + lb_im * ai) / den
    coef_im = (lb_im * ar - nr * ai) / den
    bb_re = coef_re[..., None] * b_re - coef_im[..., None] * b_im
    bb_im = coef_re[..., None] * b_im + coef_im[..., None] * b_re
    eye = jnp.eye(gl, dtype=F32)

    def b_slab(t):
        t = t.reshape(n_slab, gl, S5_STATE, S5_GROUP)
        return jnp.einsum("kgpc,gh->kgchp", t, eye).reshape(n_slab, S5_SLAB, gl * S5_STATE)

    def c_slab(t):
        t = t.reshape(n_slab, gl, S5_GROUP, S5_STATE)
        return jnp.einsum("kgcp,gh->kgphc", t, eye).reshape(n_slab, gl * S5_STATE, S5_SLAB)

    b_dense = jnp.concatenate([b_slab(bb_re), b_slab(bb_im)], axis=2).astype(BF16)
    c_dense = jnp.concatenate([c_slab(c_re.astype(F32)), -c_slab(c_im.astype(F32))], axis=1).astype(BF16)

    def flat(t):
        return t.reshape(n_slab, gl * S5_STATE)

    def power(k):
        kk = k.astype(F32)[None, :, None]
        m = jnp.exp(flat(ar * dt)[:, None, :] * kk)
        ph = flat(ai * dt)[:, None, :] * kk
        return jnp.concatenate([m * jnp.cos(ph), m * jnp.sin(ph)], axis=2)

    lam = jnp.repeat(power(jnp.array([1])), SUBLANES, axis=1)
    ptab = jnp.repeat(power(jnp.arange(1, sub + 1)), SUBLANES, axis=1)
    apow = jnp.repeat(power(jnp.array([sub, 2 * sub, 4 * sub])), SUBLANES, axis=1)
    return b_dense, c_dense, lam, ptab, apow


def _s5_mixer(u, a_re, a_im, log_step, b_re, b_im, c_re, c_im, d_skip, sub=S5_SUB):
    seq, d = u.shape
    rows = SUBLANES * sub
    n_slab = d // S5_SLAB
    n2 = 2 * S5_SLAB_STATES
    b_dense, c_dense, lam, ptab, apow = _s5_tables(a_re, a_im, log_step, b_re, b_im, c_re, c_im, sub)
    return pl.pallas_call(
        functools.partial(_s5_kernel, sub=sub),
        out_shape=jax.ShapeDtypeStruct((seq, d), BF16),
        grid=(n_slab // S5_PAIR, seq // rows),
        in_specs=[
            pl.BlockSpec((rows, S5_PAIR * S5_SLAB), lambda k, c: (c, k)),
            pl.BlockSpec((S5_PAIR, S5_SLAB, n2), lambda k, c: (k, 0, 0)),
            pl.BlockSpec((S5_PAIR, n2, S5_SLAB), lambda k, c: (k, 0, 0)),
            pl.BlockSpec((S5_PAIR, SUBLANES, n2), lambda k, c: (k, 0, 0)),
            pl.BlockSpec((S5_PAIR, rows, n2), lambda k, c: (k, 0, 0)),
            pl.BlockSpec((S5_PAIR, 3 * SUBLANES, n2), lambda k, c: (k, 0, 0)),
            pl.BlockSpec((1, S5_PAIR * S5_SLAB), lambda k, c: (0, k)),
        ],
        out_specs=pl.BlockSpec((rows, S5_PAIR * S5_SLAB), lambda k, c: (c, k)),
        scratch_shapes=[pltpu.VMEM((S5_PAIR, rows, n2), F32),
                        pltpu.VMEM((S5_PAIR, SUBLANES, n2), F32),
                        pltpu.VMEM((S5_PAIR, rows, S5_SLAB), F32),
                        pltpu.VMEM((S5_PAIR * S5_SLAB // LANES, rows, LANES), F32)],
        compiler_params=_params("arbitrary", "arbitrary"),
        name="s5_scan",
    )(u, b_dense, c_dense, lam, ptab, apow, d_skip.reshape(1, d).astype(F32))


def _cache_weights(first, pairs):
    @pl.when(first)
    def _():
        for src, dst in pairs:
            dst[...] = src[...].astype(BF16)


def _row_tiles(ref):
    return [slice(r0, r0 + SUB_ROWS) for r0 in range(0, ref.shape[0], SUB_ROWS)]


def _glu_kernel(a_ref, wa_ref, wb_ref, r_ref, o_ref, wa_s, wb_s):
    _cache_weights(pl.program_id(1) == 0, ((wa_ref, wa_s), (wb_ref, wb_s)))
    for rows in _row_tiles(a_ref):
        a = a_ref[rows, :]
        va = jnp.dot(a, wa_s[...], preferred_element_type=F32)
        vb = jnp.dot(a, wb_s[...], preferred_element_type=F32)
        o_ref[rows, :] = r_ref[rows, :] + va * jax.nn.sigmoid(vb)


def _glu_residual(a, w, res, tm=1024, tn=512):
    m, k = a.shape
    n = w.shape[1] // 2
    nb = n // tn
    return pl.pallas_call(
        _glu_kernel,
        out_shape=jax.ShapeDtypeStruct((m, n), F32),
        grid=(nb, m // tm),
        in_specs=[pl.BlockSpec((tm, k), lambda j, i: (i, 0)),
                  pl.BlockSpec((k, tn), lambda j, i: (0, j)),
                  pl.BlockSpec((k, tn), lambda j, i: (0, j + nb)),
                  pl.BlockSpec((tm, tn), lambda j, i: (i, j))],
        out_specs=pl.BlockSpec((tm, tn), lambda j, i: (i, j)),
        scratch_shapes=[pltpu.VMEM((k, tn), BF16), pltpu.VMEM((k, tn), BF16)],
        compiler_params=_params("arbitrary", "arbitrary"),
        name="glu_residual",
    )(a, w, w, res)


def _swiglu_up_kernel(a_ref, wg_ref, wu_ref, o_ref, wg_s, wu_s):
    _cache_weights(pl.program_id(1) == 0, ((wg_ref, wg_s), (wu_ref, wu_s)))
    for rows in _row_tiles(a_ref):
        a = a_ref[rows, :]
        vg = jnp.dot(a, wg_s[...], preferred_element_type=F32)
        vu = jnp.dot(a, wu_s[...], preferred_element_type=F32)
        o_ref[rows, :] = (jax.nn.silu(vg) * vu).astype(o_ref.dtype)


def _swiglu_up(a, w_gu, tm=1024, tn=512):
    m, k = a.shape
    f = w_gu.shape[1] // 2
    nb = f // tn
    return pl.pallas_call(
        _swiglu_up_kernel,
        out_shape=jax.ShapeDtypeStruct((m, f), BF16),
        grid=(nb, m // tm),
        in_specs=[pl.BlockSpec((tm, k), lambda j, i: (i, 0)),
                  pl.BlockSpec((k, tn), lambda j, i: (0, j)),
                  pl.BlockSpec((k, tn), lambda j, i: (0, j + nb))],
        out_specs=pl.BlockSpec((tm, tn), lambda j, i: (i, j)),
        scratch_shapes=[pltpu.VMEM((k, tn), BF16), pltpu.VMEM((k, tn), BF16)],
        compiler_params=_params("arbitrary", "arbitrary"),
        name="swiglu_up",
    )(a, w_gu, w_gu)


def _mm_res_kernel(a_ref, w_ref, r_ref, o_ref, w_s):
    _cache_weights(pl.program_id(1) == 0, ((w_ref, w_s),))
    for rows in _row_tiles(a_ref):
        o_ref[rows, :] = r_ref[rows, :] + jnp.dot(a_ref[rows, :], w_s[...], preferred_element_type=F32)


def _matmul_residual(a, w, res, tm=512, tn=512):
    m, k = a.shape
    n = w.shape[1]
    return pl.pallas_call(
        _mm_res_kernel,
        out_shape=jax.ShapeDtypeStruct((m, n), F32),
        grid=(n // tn, m // tm),
        in_specs=[pl.BlockSpec((tm, k), lambda j, i: (i, 0)),
                  pl.BlockSpec((k, tn), lambda j, i: (0, j)),
                  pl.BlockSpec((tm, tn), lambda j, i: (i, j))],
        out_specs=pl.BlockSpec((tm, tn), lambda j, i: (i, j)),
        scratch_shapes=[pltpu.VMEM((k, tn), BF16)],
        compiler_params=_params("arbitrary", "arbitrary"),
        name="matmul_residual",
    )(a, w, res)


def _rope_kernel(pos_ref, inv_ref, cos_ref, sa_ref, sb_ref):
    ang = pos_ref[...].astype(F32) * inv_ref[...]
    c = jnp.cos(ang)
    s = jnp.sin(ang)
    lane = lax.broadcasted_iota(jnp.int32, ang.shape, 1)
    first_half = lane < ROPE_DIM // 2
    cos_ref[...] = c
    sa_ref[...] = jnp.where(first_half, -s, 0.0)
    sb_ref[...] = jnp.where(first_half, 0.0, s)


def _rope_tables(pos, tm=512):
    n = pos.shape[0]
    tm = min(tm, n)
    half = ROPE_DIM // 2
    inv = jnp.power(ROPE_THETA, -jnp.arange(half, dtype=F32) / half)
    inv = jnp.concatenate([inv, inv, jnp.zeros((LANES - ROPE_DIM,), F32)]).reshape(1, LANES)
    spec = pl.BlockSpec((tm, LANES), lambda i: (i, 0))
    return pl.pallas_call(
        _rope_kernel,
        out_shape=[jax.ShapeDtypeStruct((n, LANES), F32)] * 3,
        grid=(n // tm,),
        in_specs=[pl.BlockSpec((tm, 1), lambda i: (i, 0)),
                  pl.BlockSpec((1, LANES), lambda i: (0, 0))],
        out_specs=[spec, spec, spec],
        compiler_params=_params("arbitrary"),
        name="rope_tables",
    )(pos.reshape(n, 1), inv)


def _rope(y, c, sa, sb):
    half = ROPE_DIM // 2
    return y * c + pltpu.roll(y, LANES - half, 1) * sa + pltpu.roll(y, half, 1) * sb


def _head_norm(x, gain):
    return x * lax.rsqrt(jnp.mean(x * x, axis=-1, keepdims=True) + RMS_EPS) * gain


def _nsa_proj_kernel(a_ref, w_ref, gain_ref, cos_ref, sa_ref, sb_ref, o_ref, w_s, *, norm_tiles, sub_rows):
    j = pl.program_id(0)
    _cache_weights(pl.program_id(1) == 0, ((w_ref, w_s),))
    is_norm = functools.reduce(jnp.logical_or, [j == t for t in norm_tiles])

    @pl.when(is_norm)
    def _():
        gain = gain_ref[0]
        for r0 in range(0, a_ref.shape[0], sub_rows):
            rows = slice(r0, r0 + sub_rows)
            acc = jnp.dot(a_ref[rows, :], w_s[...], preferred_element_type=F32)
            c, sa, sb = cos_ref[rows, :], sa_ref[rows, :], sb_ref[rows, :]
            for hh in range(acc.shape[1] // HEAD_DIM):
                sl = slice(HEAD_DIM * hh, HEAD_DIM * (hh + 1))
                o_ref[rows, sl] = _rope(_head_norm(acc[:, sl], gain), c, sa, sb).astype(o_ref.dtype)

    @pl.when(jnp.logical_not(is_norm))
    def _():
        o_ref[...] = jnp.dot(a_ref[...], w_s[...], preferred_element_type=F32).astype(o_ref.dtype)


def _nsa_proj(u, w_in, q_gain, k_gain, rope, tm=1024, tn=512, sub_rows=256):
    m, k = u.shape
    q_dim = GQA_GROUP * N_KV_HEADS * HEAD_DIM
    kv_dim = N_KV_HEADS * HEAD_DIM
    assert kv_dim == tn
    n_q = q_dim // tn
    n_tiles = n_q + 6
    ones = jnp.ones((HEAD_DIM,), F32)
    q_scaled = q_gain.astype(F32) * (HEAD_DIM ** -0.5 * math.log2(math.e))
    gains = jnp.stack([q_scaled] * n_q + [ones, ones, k_gain[1], ones, k_gain[2], ones]).reshape(n_tiles, 1, HEAD_DIM)
    norm_tiles = tuple(range(n_q)) + (n_q + 2, n_q + 4)
    tab = pl.BlockSpec((tm, LANES), lambda j, i: (i, 0))
    return pl.pallas_call(
        functools.partial(_nsa_proj_kernel, norm_tiles=norm_tiles, sub_rows=min(sub_rows, tm)),
        out_shape=jax.ShapeDtypeStruct((m, n_tiles * tn), BF16),
        grid=(n_tiles, m // tm),
        in_specs=[pl.BlockSpec((tm, k), lambda j, i: (i, 0)),
                  pl.BlockSpec((k, tn), lambda j, i: (0, j)),
                  pl.BlockSpec((1, 1, HEAD_DIM), lambda j, i: (j, 0, 0)),
                  tab, tab, tab],
        out_specs=pl.BlockSpec((tm, tn), lambda j, i: (i, j)),
        scratch_shapes=[pltpu.VMEM((k, tn), BF16)],
        compiler_params=_params("arbitrary", "arbitrary"),
        name="nsa_proj",
    )(u, w_in, gains, *rope)


def _gate_kernel(a_ref, w_ref, o_ref):
    o_ref[...] = jax.nn.sigmoid(jnp.dot(a_ref[...], w_ref[...].astype(BF16), preferred_element_type=F32))


def _nsa_gates(u, w_gate, tm=512):
    m, k = u.shape
    n = w_gate.shape[1]
    w_pad = jnp.pad(w_gate, ((0, 0), (0, LANES - n)))
    return pl.pallas_call(
        _gate_kernel,
        out_shape=jax.ShapeDtypeStruct((m, LANES), F32),
        grid=(m // tm,),
        in_specs=[pl.BlockSpec((tm, k), lambda i: (i, 0)),
                  pl.BlockSpec((k, LANES), lambda i: (0, 0))],
        out_specs=pl.BlockSpec((tm, LANES), lambda i: (i, 0)),
        compiler_params=_params("arbitrary"),
        name="nsa_gates",
    )(u, w_pad)


def _compress_kernel(*refs, is_key):
    if is_key:
        ca_ref, cb_ref, pe_ref, w1_ref, w2_ref, gain_ref, cos_ref, sa_ref, sb_ref, o_ref = refs
    else:
        ca_ref, cb_ref, pe_ref, w1_ref, w2_ref, o_ref = refs
    half = w1_ref.shape[0] // 2
    pe = pe_ref[...]
    xa = (ca_ref[0].astype(F32) + pe[:, :half]).astype(BF16)
    xb = (cb_ref[0].astype(F32) + pe[:, half:]).astype(BF16)
    hid = (jnp.dot(xa, w1_ref[:half, :].astype(BF16), preferred_element_type=F32)
           + jnp.dot(xb, w1_ref[half:, :].astype(BF16), preferred_element_type=F32))
    out = jnp.dot(jax.nn.gelu(hid).astype(BF16), w2_ref[...].astype(BF16), preferred_element_type=F32)
    if is_key:
        out = _rope(_head_norm(out, gain_ref[...]), cos_ref[...], sa_ref[...], sb_ref[...])
    o_ref[0] = out.astype(o_ref.dtype)


def _compress(t, pe, w1, w2, key_extras=None):
    seq = t.shape[0]
    nc = seq // CMP_STRIDE
    width = CMP_STRIDE * HEAD_DIM
    ca = t.reshape(nc, CMP_STRIDE, N_KV_HEADS, HEAD_DIM).transpose(2, 0, 1, 3).reshape(N_KV_HEADS, nc, width)
    cb = jnp.concatenate([ca[:, 1:], jnp.zeros((N_KV_HEADS, 1, width), ca.dtype)], axis=1)
    blk = pl.BlockSpec((1, nc, width), lambda h: (h, 0, 0))
    full = lambda a: pl.BlockSpec(a.shape, lambda h: (0,) * a.ndim)
    args = [ca, cb, pe.reshape(1, CMP_BLOCK * HEAD_DIM), w1, w2]
    if key_extras is not None:
        args += list(key_extras)
    return pl.pallas_call(
        functools.partial(_compress_kernel, is_key=key_extras is not None),
        out_shape=jax.ShapeDtypeStruct((N_KV_HEADS, nc, HEAD_DIM), BF16),
        grid=(N_KV_HEADS,),
        in_specs=[blk, blk] + [full(a) for a in args[2:]],
        out_specs=pl.BlockSpec((1, nc, HEAD_DIM), lambda h: (h, 0, 0)),
        compiler_params=_params("arbitrary"),
        name="nsa_compress_k" if key_extras is not None else "nsa_compress_v",
    )(*args)


def _dot_nt(a, b):
    return lax.dot_general(a, b, (((1,), (1,)), ((), ())), preferred_element_type=F32)


def _split3(x):
    hi = x.astype(BF16)
    r1 = x - hi.astype(F32)
    mid = r1.astype(BF16)
    lo = (r1 - mid.astype(F32)).astype(BF16)
    return hi, mid, lo


def _nsa_attn_kernel(q_ref, kc_ref, vct_ref, ks_ref, vst_ref, kw_ref, vwt_ref, gate_ref, blk_ref, o_ref,
                     acc_ref, mix_ref, m_ref, l_ref, qa_ref, s0_ref, s1_ref, pc_ref, sw_ref, pw_ref,
                     *, seq, tk):
    t0 = pl.program_id(1) * Q_BLOCK
    nc = kc_ref.shape[1]
    ns = seq // SEL_BLOCK
    grp = GQA_GROUP
    cols = grp * Q_BLOCK
    sel_shift = int(math.log2(SEL_BLOCK))
    q = q_ref[...].astype(F32)
    qt = jnp.concatenate([q[:, HEAD_DIM * g:HEAD_DIM * (g + 1)].T for g in range(grp)], axis=1).astype(BF16)
    t_row = t0 + lax.broadcasted_iota(jnp.int32, (1, Q_BLOCK), 1)

    def heads(x):
        return jnp.concatenate([x] * grp, axis=1)

    span = WINDOW + Q_BLOCK
    w0 = pl.multiple_of(jnp.maximum(t0 - WINDOW, 0), Q_BLOCK)
    gate = gate_ref[0, 0]
    n_idx = lax.broadcasted_iota(jnp.int32, (nc, Q_BLOCK), 0)
    ok_c = (n_idx * CMP_STRIDE + (CMP_BLOCK - 1) <= t_row) & (n_idx < nc - 1)
    sb_c = jnp.dot(kc_ref[0], qt, preferred_element_type=F32) + heads(jnp.where(ok_c, 0.0, NEG))
    m_c = jnp.max(sb_c, axis=0, keepdims=True)
    e_c = jnp.exp2(sb_c - m_c)
    den_c = jnp.maximum(jnp.sum(e_c, axis=0, keepdims=True), 1e-30)
    p_c = e_c * jnp.where(m_c > 0.5 * NEG, 1.0 / den_c, 0.0)
    pc_ref[...] = p_c.astype(BF16)
    imp = p_c[:, :Q_BLOCK]
    for g in range(1, grp):
        imp = imp + p_c[:, Q_BLOCK * g:Q_BLOCK * (g + 1)]

    ratio = SEL_BLOCK // CMP_STRIDE
    d = (lax.broadcasted_iota(jnp.int32, (ns, nc), 1)
         - ratio * lax.broadcasted_iota(jnp.int32, (ns, nc), 0))
    overlap = jnp.zeros((ns, nc), F32)
    for n in range(CMP_BLOCK // CMP_STRIDE):
        overlap = overlap + jnp.where((d - n >= 0) & (d - n < ratio), 1.0, 0.0)
    overlap = overlap.astype(BF16)
    p_slc = sum(jnp.dot(overlap, part, preferred_element_type=F32) for part in _split3(imp))

    mix_ref[...] = gate[0:1] * jnp.dot(vct_ref[0], pc_ref[...], preferred_element_type=F32)
    rel = t_row - (w0 + lax.broadcasted_iota(jnp.int32, (span, Q_BLOCK), 0))
    sw_ref[...] = (jnp.dot(kw_ref[pl.ds(w0, span), :], qt, preferred_element_type=F32)
                   + heads(jnp.where((rel >= 0) & (rel < WINDOW), 0.0, NEG)))
    m_w = jnp.max(sw_ref[...], axis=0, keepdims=True)

    j_idx = lax.broadcasted_iota(jnp.int32, (ns, Q_BLOCK), 0)
    j_f = j_idx.astype(F32)
    dist = jnp.right_shift(t_row, sel_shift) - j_idx
    forced = (j_idx == 0) | ((dist >= 0) & (dist < SEL_LOCAL))
    score = jnp.where(forced, jnp.inf, jnp.where(dist >= 0, p_slc, -jnp.inf))
    sel = jnp.zeros((ns, Q_BLOCK), F32)
    k_top = min(SEL_TOPK, ns)
    pack = 2 * SUBLANES
    cuts = [span // pack * r // k_top * pack for r in range(k_top + 1)]
    den_w = jnp.zeros((1, cols), F32)
    for r in range(k_top):
        top = jnp.max(score, axis=0, keepdims=True)
        idx = jnp.min(jnp.where(score == top, j_f, float(ns)), axis=0, keepdims=True)
        pick = j_f == idx
        sel = jnp.where(pick, 1.0, sel)
        score = jnp.where(pick, -jnp.inf, score)
        if cuts[r + 1] > cuts[r]:
            e_w = jnp.exp2(sw_ref[cuts[r]:cuts[r + 1], :] - m_w)
            den_w = den_w + jnp.sum(e_w, axis=0, keepdims=True)
            pw_ref[cuts[r]:cuts[r + 1], :] = e_w.astype(BF16)
    o_w = jnp.dot(vwt_ref[0, :, pl.ds(w0, span)], pw_ref[...], preferred_element_type=F32)
    mix_ref[...] += (gate[2:3] * (1.0 / den_w)) * o_w

    qa_ref[...] = jnp.concatenate([qt, heads(jnp.where(sel > 0.0, 0.0, NEG).astype(BF16))], axis=0)
    acc_ref[...] = jnp.zeros_like(acc_ref)
    m_ref[...] = jnp.full_like(m_ref, NEG)
    l_ref[...] = jnp.zeros_like(l_ref)

    def score_tile(kt, dst):
        k0 = pl.multiple_of(kt * tk, tk)
        k_aug = jnp.concatenate([ks_ref[pl.ds(k0, tk), :], blk_ref[pl.ds(k0, tk), :]], axis=1)
        dst[...] = jnp.dot(k_aug, qa_ref[...], preferred_element_type=F32)

    def consume_tile(kt, src, diagonal):
        k0 = pl.multiple_of(kt * tk, tk)
        sb = src[...]
        if diagonal:
            kpos = k0 + lax.broadcasted_iota(jnp.int32, (tk, Q_BLOCK), 0)
            sb = sb + heads(jnp.where(kpos <= t_row, 0.0, NEG))
        m_i = m_ref[...]
        m_new = jnp.maximum(m_i, jnp.max(sb, axis=0, keepdims=True))
        e = jnp.exp2(sb - m_new)
        alpha = jnp.exp2(m_i - m_new)
        m_ref[...] = m_new
        l_ref[...] = alpha * l_ref[...] + jnp.sum(e, axis=0, keepdims=True)
        acc_ref[...] = alpha * acc_ref[...] + jnp.dot(vst_ref[0, :, pl.ds(k0, tk)], e.astype(BF16),
                                                      preferred_element_type=F32)

    last = t0 // tk
    score_tile(0, s0_ref)

    def tile_pair(i, _):
        score_tile(2 * i + 1, s1_ref)
        consume_tile(2 * i, s0_ref, False)
        score_tile(2 * i + 2, s0_ref)
        consume_tile(2 * i + 1, s1_ref, False)
        return 0

    lax.fori_loop(0, last // 2, tile_pair, 0)

    @pl.when(last % 2 == 1)
    def _():
        score_tile(last, s1_ref)
        consume_tile(last - 1, s0_ref, False)
        consume_tile(last, s1_ref, True)

    @pl.when(last % 2 == 0)
    def _():
        consume_tile(last, s0_ref, True)

    o_s = acc_ref[...] * (1.0 / jnp.maximum(l_ref[...], 1e-30))

    mixed = mix_ref[...] + gate_ref[0, 0, 1:2] * o_s
    for g in range(grp):
        o_ref[:, HEAD_DIM * g:HEAD_DIM * (g + 1)] = mixed[:, Q_BLOCK * g:Q_BLOCK * (g + 1)].T.astype(o_ref.dtype)


def _nsa_attention(proj, kcmp, vcmp_t, vsl_t, vw_t, gates, tk=512):
    seq = proj.shape[0]
    tk = min(tk, seq)
    q_dim = GQA_GROUP * N_KV_HEADS * HEAD_DIM
    kv_blocks = N_KV_HEADS
    first = q_dim // HEAD_DIM + 2 * kv_blocks
    nc = kcmp.shape[1]
    cols = GQA_GROUP * Q_BLOCK

    def k_spec(which):
        return pl.BlockSpec((seq, HEAD_DIM), lambda h, qb: (0, first + which * kv_blocks + h))

    def vt_spec(n):
        return pl.BlockSpec((1, HEAD_DIM, n), lambda h, qb: (h, 0, 0))

    q_spec = pl.BlockSpec((Q_BLOCK, GQA_GROUP * HEAD_DIM), lambda h, qb: (qb, h))
    ns = seq // SEL_BLOCK
    key_block = (jnp.arange(seq, dtype=jnp.int32)[:, None] // SEL_BLOCK
                 == jnp.arange(ns, dtype=jnp.int32)[None, :]).astype(BF16)
    return pl.pallas_call(
        functools.partial(_nsa_attn_kernel, seq=seq, tk=tk),
        out_shape=jax.ShapeDtypeStruct((seq, q_dim), BF16),
        grid=(N_KV_HEADS, seq // Q_BLOCK),
        in_specs=[q_spec, pl.BlockSpec((1, nc, HEAD_DIM), lambda h, qb: (h, 0, 0)), vt_spec(nc),
                  k_spec(0), vt_spec(seq), k_spec(2), vt_spec(seq),
                  pl.BlockSpec((1, 1, 3, cols), lambda h, qb: (h, qb, 0, 0)),
                  pl.BlockSpec((seq, ns), lambda h, qb: (0, 0))],
        out_specs=q_spec,
        scratch_shapes=[pltpu.VMEM((HEAD_DIM, cols), F32), pltpu.VMEM((HEAD_DIM, cols), F32),
                        pltpu.VMEM((1, cols), F32), pltpu.VMEM((1, cols), F32),
                        pltpu.VMEM((HEAD_DIM + ns, cols), BF16),
                        pltpu.VMEM((tk, cols), F32), pltpu.VMEM((tk, cols), F32),
                        pltpu.VMEM((nc, cols), BF16), pltpu.VMEM((WINDOW + Q_BLOCK, cols), F32),
                        pltpu.VMEM((WINDOW + Q_BLOCK, cols), BF16)],
        compiler_params=_params("arbitrary", "arbitrary"),
        name="nsa_attention",
    )(proj, kcmp, vcmp_t, proj, vsl_t, proj, vw_t, gates, key_block)


def _nsa_mixer(u, positions, w_in, q_gain, k_gain, pe_k, pe_v, ck_w1, ck_w2, cv_w1, cv_w2):
    seq = u.shape[0]
    q_dim = GQA_GROUP * N_KV_HEADS * HEAD_DIM
    kv_dim = N_KV_HEADS * HEAD_DIM
    n_main = q_dim + 6 * kv_dim
    nc = seq // CMP_STRIDE
    rope = _rope_tables(positions)
    proj = _nsa_proj(u, w_in, q_gain, k_gain, rope)
    gate = _nsa_gates(u, w_in[:, n_main:])
    gates = (gate[:, :3 * N_KV_HEADS * GQA_GROUP].reshape(seq // Q_BLOCK, Q_BLOCK, 3, N_KV_HEADS, GQA_GROUP)
             .transpose(3, 0, 2, 4, 1).reshape(N_KV_HEADS, seq // Q_BLOCK, 3, GQA_GROUP * Q_BLOCK))

    def keys_last(cols):
        return cols.reshape(seq, N_KV_HEADS, HEAD_DIM).transpose(1, 2, 0)

    pos_cmp = jnp.concatenate([positions[CMP_BLOCK - 1::CMP_STRIDE][:nc - 1], jnp.zeros((1,), positions.dtype)])
    rope_cmp = _rope_tables(pos_cmp)
    kcmp = _compress(proj[:, q_dim:q_dim + kv_dim], pe_k, ck_w1, ck_w2,
                     key_extras=(k_gain[0].reshape(1, HEAD_DIM),) + tuple(rope_cmp))
    vcmp = _compress(proj[:, q_dim + kv_dim:q_dim + 2 * kv_dim], pe_v, cv_w1, cv_w2)
    vsl_t = keys_last(proj[:, q_dim + 3 * kv_dim:q_dim + 4 * kv_dim])
    vw_t = keys_last(proj[:, q_dim + 5 * kv_dim:q_dim + 6 * kv_dim])
    return _nsa_attention(proj, kcmp, vcmp.transpose(0, 2, 1), vsl_t, vw_t, gates)


def _router_kernel(x_ref, g_ref, w_ref, b_ref, u_ref, r_ref):
    x = x_ref[...]
    u = x * lax.rsqrt(jnp.mean(x * x, axis=-1, keepdims=True) + RMS_EPS) * g_ref[...]
    u_ref[...] = u
    uh, um, _ = _split3(u)
    wh, wm, _ = _split3(w_ref[...])
    logits = (jnp.dot(uh, wh, preferred_element_type=F32) + jnp.dot(uh, wm, preferred_element_type=F32)
              + jnp.dot(um, wh, preferred_element_type=F32)) + b_ref[...]
    lane = lax.broadcasted_iota(jnp.int32, logits.shape, 1).astype(F32)
    lg = jnp.where(lane < N_EXPERTS, logits, -jnp.inf)
    v1 = jnp.max(lg, axis=-1, keepdims=True)
    i1 = jnp.min(jnp.where(lg == v1, lane, float(LANES)), axis=-1, keepdims=True)
    lg = jnp.where(lane == i1, -jnp.inf, lg)
    v2 = jnp.max(lg, axis=-1, keepdims=True)
    i2 = jnp.min(jnp.where(lg == v2, lane, float(LANES)), axis=-1, keepdims=True)
    e2 = jnp.exp(v2 - v1)
    den = 1.0 + e2
    r_ref[...] = jnp.where(lane == 0, i1, jnp.where(lane == 1, i2, jnp.where(
        lane == 2, 1.0 / den, jnp.where(lane == 3, e2 / den, 0.0))))


def _router(h, gain, w_router, b_router, tm=256):
    m, d = h.shape
    w_pad = jnp.pad(w_router.astype(F32), ((0, 0), (0, LANES - N_EXPERTS)))
    b_pad = jnp.pad(b_router.astype(F32), (0, LANES - N_EXPERTS)).reshape(1, LANES)
    return pl.pallas_call(
        _router_kernel,
        out_shape=[jax.ShapeDtypeStruct((m, d), F32), jax.ShapeDtypeStruct((m, LANES), F32)],
        grid=(m // tm,),
        in_specs=[pl.BlockSpec((tm, d), lambda i: (i, 0)),
                  pl.BlockSpec((1, d), lambda i: (0, 0)),
                  pl.BlockSpec((d, LANES), lambda i: (0, 0)),
                  pl.BlockSpec((1, LANES), lambda i: (0, 0))],
        out_specs=[pl.BlockSpec((tm, d), lambda i: (i, 0)), pl.BlockSpec((tm, LANES), lambda i: (i, 0))],
        compiler_params=_params("arbitrary"),
        name="moe_router",
    )(h, gain.reshape(1, d), w_pad, b_pad)


def _row_copy(src_hbm, row, dst, r, sem):
    return pltpu.make_async_copy(src_hbm.at[pl.ds(row, 1), :], dst.at[pl.ds(r, 1), :], sem)


def _gather_kernel(idx_ref, used_ref, src_hbm, o_ref, buf, sem):
    rows = o_ref.shape[0]
    i = pl.program_id(0)
    n_used = used_ref[0]

    def issue(blk):
        slot = blk % 2

        def start(r, _):
            _row_copy(src_hbm, idx_ref[blk * rows + r], buf.at[slot], r, sem.at[slot]).start()
            return 0

        lax.fori_loop(0, rows, start, 0, unroll=DMA_UNROLL)

    @pl.when(i == 0)
    def _():
        issue(i)

    @pl.when(i + 1 < n_used)
    def _():
        issue(i + 1)

    @pl.when(i < n_used)
    def _():
        slot = i % 2

        def wait(r, _):
            _row_copy(src_hbm, 0, buf.at[slot], r, sem.at[slot]).wait()
            return 0

        lax.fori_loop(0, rows, wait, 0, unroll=DMA_UNROLL)
        o_ref[...] = buf[slot].astype(o_ref.dtype)

    @pl.when(i >= n_used)
    def _():
        o_ref[...] = jnp.zeros_like(o_ref)


def _gather_rows(src, idx, n_used, out_dtype, rows=MOE_ROWS):
    n = idx.shape[0]
    d = src.shape[1]
    return pl.pallas_call(
        _gather_kernel,
        out_shape=jax.ShapeDtypeStruct((n, d), out_dtype),
        grid_spec=pltpu.PrefetchScalarGridSpec(
            num_scalar_prefetch=2,
            grid=(n // rows,),
            in_specs=[pl.BlockSpec(memory_space=pl.ANY)],
            out_specs=pl.BlockSpec((rows, d), lambda i, idx, used: (i, 0)),
            scratch_shapes=[pltpu.VMEM((2, rows, d), src.dtype), pltpu.SemaphoreType.DMA((2,))]),
        compiler_params=_params("arbitrary"),
        name="moe_gather",
    )(idx, n_used, src)


def _block_state(be_ref, used_ref, i):
    changed = (i == 0) | (be_ref[i] != be_ref[jnp.maximum(i - 1, 0)])
    used = i < used_ref[0]
    return used, used & changed


def _last_used(i, used):
    return jnp.minimum(i, used[0] - 1)


def _stream_expert_weights(first, be_ref, rix_ref, rune_ref, nrun_ref, n_tiles, copies, cast):
    j = pl.program_id(0)
    i = pl.program_id(1)

    @pl.when(first)
    def _():
        rix = rix_ref[i]
        n_runs = nrun_ref[0]
        g = j * n_runs + rix
        slot = g % 2

        @pl.when(g == 0)
        def _():
            for c in copies(be_ref[i], j, slot):
                c.start()

        more = rix + 1 < n_runs
        e_next = jnp.where(more, rune_ref[jnp.minimum(rix + 1, N_EXPERTS - 1)], rune_ref[0])
        j_next = jnp.where(more, j, j + 1)

        @pl.when(j_next < n_tiles)
        def _():
            for c in copies(e_next, j_next, 1 - slot):
                c.start()

        for c in copies(be_ref[i], j, slot):
            c.wait()
        cast(slot)


def _moe_up_kernel(be_ref, used_ref, rix_ref, rune_ref, nrun_ref, x_ref, w_hbm, o_ref, wbuf, wg_s, wu_s, sem,
                   *, nb, tn):
    used, first = _block_state(be_ref, used_ref, pl.program_id(1))

    def copies(e, jj, slot):
        return [pltpu.make_async_copy(w_hbm.at[e, :, pl.ds(pl.multiple_of((jj + m * nb) * tn, tn), tn)],
                                      wbuf.at[slot, m], sem.at[slot, m]) for m in range(2)]

    def cast(slot):
        wg_s[...] = wbuf[slot, 0].astype(BF16)
        wu_s[...] = wbuf[slot, 1].astype(BF16)

    _stream_expert_weights(first, be_ref, rix_ref, rune_ref, nrun_ref, nb, copies, cast)

    @pl.when(used)
    def _():
        for rows in _row_tiles(x_ref):
            a = x_ref[rows, :]
            vg = jnp.dot(a, wg_s[...], preferred_element_type=F32)
            vu = jnp.dot(a, wu_s[...], preferred_element_type=F32)
            o_ref[rows, :] = (jax.nn.silu(vg) * vu).astype(o_ref.dtype)

    @pl.when(jnp.logical_not(used))
    def _():
        o_ref[...] = jnp.zeros_like(o_ref)


def _moe_up(x_rows, sched, w_gu, tn=512, rows=MOE_ROWS):
    n, k = x_rows.shape
    f = w_gu.shape[2] // 2
    nb = f // tn
    return pl.pallas_call(
        functools.partial(_moe_up_kernel, nb=nb, tn=tn),
        out_shape=jax.ShapeDtypeStruct((n, f), BF16),
        grid_spec=pltpu.PrefetchScalarGridSpec(
            num_scalar_prefetch=len(sched),
            grid=(nb, n // rows),
            in_specs=[pl.BlockSpec((rows, k), lambda j, i, be, nu, *_: (_last_used(i, nu), 0)),
                      pl.BlockSpec(memory_space=pl.ANY)],
            out_specs=pl.BlockSpec((rows, tn), lambda j, i, *_: (i, j)),
            scratch_shapes=[pltpu.VMEM((2, 2, k, tn), F32), pltpu.VMEM((k, tn), BF16),
                            pltpu.VMEM((k, tn), BF16), pltpu.SemaphoreType.DMA((2, 2))]),
        compiler_params=_params("arbitrary", "arbitrary"),
        name="moe_up",
    )(*sched, x_rows, w_gu)


def _moe_down_kernel(be_ref, used_ref, rix_ref, rune_ref, nrun_ref, a_ref, w_hbm, o_ref, wbuf, w_s, sem,
                     *, nb, tn):
    used, first = _block_state(be_ref, used_ref, pl.program_id(1))

    def copies(e, jj, slot):
        return [pltpu.make_async_copy(w_hbm.at[e, :, pl.ds(pl.multiple_of(jj * tn, tn), tn)],
                                      wbuf.at[slot], sem.at[slot])]

    def cast(slot):
        w_s[...] = wbuf[slot].astype(BF16)

    _stream_expert_weights(first, be_ref, rix_ref, rune_ref, nrun_ref, nb, copies, cast)

    @pl.when(used)
    def _():
        o_ref[...] = jnp.dot(a_ref[...], w_s[...], preferred_element_type=F32)

    @pl.when(jnp.logical_not(used))
    def _():
        o_ref[...] = jnp.zeros_like(o_ref)


def _moe_down(act, sched, w_down, tn=512, rows=MOE_ROWS):
    n, k = act.shape
    d = w_down.shape[2]
    nb = d // tn
    return pl.pallas_call(
        functools.partial(_moe_down_kernel, nb=nb, tn=tn),
        out_shape=jax.ShapeDtypeStruct((n, d), F32),
        grid_spec=pltpu.PrefetchScalarGridSpec(
            num_scalar_prefetch=len(sched),
            grid=(nb, n // rows),
            in_specs=[pl.BlockSpec((rows, k), lambda j, i, be, nu, *_: (_last_used(i, nu), 0)),
                      pl.BlockSpec(memory_space=pl.ANY)],
            out_specs=pl.BlockSpec((rows, tn), lambda j, i, *_: (i, j)),
            scratch_shapes=[pltpu.VMEM((2, k, tn), F32), pltpu.VMEM((k, tn), BF16),
                            pltpu.SemaphoreType.DMA((2,))]),
        compiler_params=_params("arbitrary", "arbitrary"),
        name="moe_down",
    )(*sched, act, w_down)


def _combine_kernel(dest_ref, h_ref, r_ref, rows_hbm, o_ref, buf, sem):
    tm = h_ref.shape[0]
    base = pl.program_id(0) * tm

    def start(r, _):
        for k in range(2):
            _row_copy(rows_hbm, dest_ref[2 * (base + r) + k], buf.at[k], r, sem.at[k]).start()
        return 0

    def wait(r, _):
        for k in range(2):
            _row_copy(rows_hbm, 0, buf.at[k], r, sem.at[k]).wait()
        return 0

    lax.fori_loop(0, tm, start, 0, unroll=DMA_UNROLL)
    lax.fori_loop(0, tm, wait, 0, unroll=DMA_UNROLL)
    w = r_ref[...]
    o_ref[...] = h_ref[...] + (w[:, 2:3] * buf[0] + w[:, 3:4] * buf[1])


def _moe_combine(h, route, out_rows, dest, tm=256):
    m, d = h.shape
    return pl.pallas_call(
        _combine_kernel,
        out_shape=jax.ShapeDtypeStruct((m, d), F32),
        grid_spec=pltpu.PrefetchScalarGridSpec(
            num_scalar_prefetch=1,
            grid=(m // tm,),
            in_specs=[pl.BlockSpec((tm, d), lambda i, dest: (i, 0)),
                      pl.BlockSpec((tm, LANES), lambda i, dest: (i, 0)),
                      pl.BlockSpec(memory_space=pl.ANY)],
            out_specs=pl.BlockSpec((tm, d), lambda i, dest: (i, 0)),
            scratch_shapes=[pltpu.VMEM((2, tm, d), F32), pltpu.SemaphoreType.DMA((2,))]),
        compiler_params=_params("arbitrary"),
        name="moe_combine",
    )(dest.reshape(-1), h, route, out_rows)


def _moe_layout(top_e, rows=MOE_ROWS):
    n_tok = top_e.shape[0]
    e_flat = top_e.reshape(-1)
    onehot = (e_flat[:, None] == jnp.arange(N_EXPERTS, dtype=jnp.int32)[None, :]).astype(jnp.int32)
    csum = jnp.cumsum(onehot, axis=0)
    rank = jnp.take_along_axis(csum, e_flat[:, None], axis=1)[:, 0] - 1
    counts = csum[-1]
    padded = (counts + rows - 1) // rows * rows
    pad_end = jnp.cumsum(padded)
    dest = (pad_end - padded)[e_flat] + rank
    n_rows = e_flat.shape[0] + N_EXPERTS * rows
    t_flat = jnp.repeat(jnp.arange(n_tok, dtype=jnp.int32), top_e.shape[1])
    row_tok = jnp.zeros((n_rows,), jnp.int32).at[dest].set(t_flat)
    n_blk = n_rows // rows
    blk_start = jnp.arange(n_blk, dtype=jnp.int32) * rows
    blk_e = jnp.minimum(jnp.sum(blk_start[:, None] >= pad_end[None, :], axis=1), N_EXPERTS - 1).astype(jnp.int32)
    n_used = (pad_end[-1:] // rows).astype(jnp.int32)
    first = (jnp.arange(n_blk) < n_used[0]) & (blk_e != jnp.concatenate([blk_e[:1] - 1, blk_e[:-1]]))
    run_ix = (jnp.cumsum(first) - 1).astype(jnp.int32)
    in_run = first[:, None] & (run_ix[:, None] == jnp.arange(N_EXPERTS, dtype=jnp.int32)[None, :])
    run_e = jnp.sum(jnp.where(in_run, blk_e[:, None], 0), axis=0).astype(jnp.int32)
    n_runs = jnp.sum(first).astype(jnp.int32).reshape(1)
    sched = (blk_e, n_used, run_ix, run_e, n_runs)
    return row_tok, sched, dest.astype(jnp.int32).reshape(n_tok, -1)


def _moe_ffn_residual(h, gain, w_router, b_router, w_gu, w_down):
    u, route = _router(h, gain, w_router, b_router)
    top_e = route[:, :2].astype(jnp.int32)
    row_tok, sched, dest = _moe_layout(top_e)
    x_rows = _gather_rows(u, row_tok, sched[1], BF16)
    act = _moe_up(x_rows, sched, w_gu)
    out_rows = _moe_down(act, sched, w_down)
    return _moe_combine(h, route, out_rows, dest)


def kernel(x, positions, norm_mix, norm_ffn, s5_a_re, s5_a_im, s5_log_step, s5_b_re, s5_b_im, s5_c_re, s5_c_im, s5_d, s5_w_glu, nsa_w_in, nsa_q_gain, nsa_k_gain, nsa_pe_k, nsa_pe_v, nsa_ck_w1, nsa_ck_w2, nsa_cv_w1, nsa_cv_w2, nsa_w_out, ffn_w_gu, ffn_w_down, moe_w_router, moe_b_router, moe_w_gu, moe_w_down):
    bsz, seq, d = x.shape
    assert bsz == 1, "the scan and attention kernels take one sequence"
    h = x.reshape(seq, d)
    h = _layer_s5(h, norm_mix[0], norm_ffn[0], s5_a_re[0], s5_a_im[0], s5_log_step[0], s5_b_re[0],
                  s5_b_im[0], s5_c_re[0], s5_c_im[0], s5_d[0], s5_w_glu[0], ffn_w_gu[0], ffn_w_down[0])
    h = _layer_nsa(h, positions[0], norm_mix[1], norm_ffn[1], nsa_w_in[0], nsa_q_gain[0], nsa_k_gain[0],
                   nsa_pe_k[0], nsa_pe_v[0], nsa_ck_w1[0], nsa_ck_w2[0], nsa_cv_w1[0], nsa_cv_w2[0],
                   nsa_w_out[0], moe_w_router[0], moe_b_router[0], moe_w_gu[0], moe_w_down[0])
    return h.reshape(bsz, seq, d)


def _layer_nsa(h, positions, g_mix, g_ffn, w_in, q_gain, k_gain, pe_k, pe_v, ck_w1, ck_w2, cv_w1, cv_w2,
               w_out, w_router, b_router, w_gu, w_down):
    u = _rms_norm(h, g_mix, BF16)
    o = _nsa_mixer(u, positions, w_in, q_gain, k_gain, pe_k, pe_v, ck_w1, ck_w2, cv_w1, cv_w2)
    h = _matmul_residual(o, w_out, h, tm=1024)
    return _moe_ffn_residual(h, g_ffn, w_router, b_router, w_gu, w_down)


def _layer_s5(h, g_mix, g_ffn, a_re, a_im, log_step, b_re, b_im, c_re, c_im, d_skip, w_glu, w_gu, w_down):
    u = _rms_norm(h, g_mix, F32)
    g = _s5_mixer(u, a_re, a_im, log_step, b_re, b_im, c_re, c_im, d_skip)
    h = _glu_residual(g, w_glu, h)
    u = _rms_norm(h, g_ffn, BF16)
    act = _swiglu_up(u, w_gu)
    return _matmul_residual(act, w_down, h)
```

```python
import functools
import math

import jax
import jax.numpy as jnp
from jax import lax
from jax.experimental import pallas as pl
from jax.experimental.pallas import tpu as pltpu

F32 = jnp.float32
BF16 = jnp.bfloat16

RMS_EPS = 1e-6
S5_GROUP = 16
S5_STATE = 64
HEAD_DIM = 128
N_KV_HEADS = 4
GQA_GROUP = 4
ROPE_DIM = 32
ROPE_THETA = 500000.0
CMP_BLOCK = 32
CMP_STRIDE = 16
SEL_BLOCK = 64
SEL_TOPK = 16
SEL_LOCAL = 2
WINDOW = 512
Q_BLOCK = 128
N_EXPERTS = 8
NEG = -1e30

LANES = 128
SUBLANES = 8
VMEM_LIMIT = 56 * 1024 * 1024

S5_SLAB = 256
S5_SLAB_STATES = S5_SLAB // S5_GROUP * S5_STATE
S5_SUB = 64
S5_PAIR = 2
MOE_ROWS = 512
SUB_ROWS = 256
DMA_UNROLL = 8


def _params(*sem):
    return pltpu.CompilerParams(dimension_semantics=sem, vmem_limit_bytes=VMEM_LIMIT)


def _rms_kernel(x_ref, g_ref, o_ref):
    x = x_ref[...]
    ms = jnp.mean(x * x, axis=-1, keepdims=True)
    o_ref[...] = (x * lax.rsqrt(ms + RMS_EPS) * g_ref[...]).astype(o_ref.dtype)


def _rms_norm(x, gain, out_dtype, tm=512):
    m, d = x.shape
    return pl.pallas_call(
        _rms_kernel,
        out_shape=jax.ShapeDtypeStruct((m, d), out_dtype),
        grid=(m // tm,),
        in_specs=[pl.BlockSpec((tm, d), lambda i: (i, 0)),
                  pl.BlockSpec((1, d), lambda i: (0, 0))],
        out_specs=pl.BlockSpec((tm, d), lambda i: (i, 0)),
        compiler_params=_params("arbitrary"),
        name="rms_norm",
    )(x, gain.reshape(1, d))


def _s5_kernel(u_ref, b_ref, c_ref, lam_ref, ptab_ref, apow_ref, d_ref, o_ref,
               xs_ref, carry_ref, up_ref, us_ref, *, sub):
    n = S5_SLAB_STATES
    halves = S5_SLAB // LANES
    row = lax.broadcasted_iota(jnp.int32, (SUBLANES, n), 0)

    @pl.when(pl.program_id(1) == 0)
    def _():
        carry_ref[...] = jnp.zeros_like(carry_ref)

    def in_proj(s):
        for c in range(halves):
            us_ref[s * halves + c] = u_ref[:, S5_SLAB * s + LANES * c:S5_SLAB * s + LANES * (c + 1)]
        for i in range(sub):
            for c in range(halves):
                up_ref[s, SUBLANES * i:SUBLANES * (i + 1), LANES * c:LANES * (c + 1)] = (
                    us_ref[s * halves + c, pl.ds(i, SUBLANES, stride=sub), :])
        xs_ref[s] = jnp.dot(up_ref[s].astype(BF16), b_ref[s], preferred_element_type=F32)

    def scan(s):
        lam = lam_ref[s]
        lr, li = lam[:, :n], lam[:, n:]
        hr = hi = jnp.zeros((SUBLANES, n), F32)
        for i in range(sub):
            rows = slice(SUBLANES * i, SUBLANES * (i + 1))
            x = xs_ref[s, rows, :]
            hr, hi = lr * hr - li * hi + x[:, :n], lr * hi + li * hr + x[:, n:]
            xs_ref[s, rows, :] = jnp.concatenate([hr, hi], axis=1)
        er, ei = hr, hi

        cin = carry_ref[s]
        zr = jnp.where(row == 0, cin[:, :n], pltpu.roll(er, 1, 0))
        zi = jnp.where(row == 0, cin[:, n:], pltpu.roll(ei, 1, 0))
        apow = apow_ref[s]
        for k, d in enumerate((1, 2, 4)):
            ar = apow[SUBLANES * k:SUBLANES * (k + 1), :n]
            ai = apow[SUBLANES * k:SUBLANES * (k + 1), n:]
            sr = pltpu.roll(zr, d, 0)
            si = pltpu.roll(zi, d, 0)
            keep = row >= d
            zr, zi = (zr + jnp.where(keep, ar * sr - ai * si, 0.0),
                      zi + jnp.where(keep, ar * si + ai * sr, 0.0))
        a1r, a1i = apow[:SUBLANES, :n], apow[:SUBLANES, n:]
        nxt_r = a1r * zr - a1i * zi + er
        nxt_i = a1r * zi + a1i * zr + ei
        carry_ref[s] = jnp.concatenate(
            [jnp.broadcast_to(nxt_r[SUBLANES - 1:, :], (SUBLANES, n)),
             jnp.broadcast_to(nxt_i[SUBLANES - 1:, :], (SUBLANES, n))], axis=1)

        for i in range(sub):
            rows = slice(SUBLANES * i, SUBLANES * (i + 1))
            x = xs_ref[s, rows, :]
            p = ptab_ref[s, rows, :]
            pr, pi = p[:, :n], p[:, n:]
            xs_ref[s, rows, :] = jnp.concatenate([x[:, :n] + pr * zr - pi * zi,
                                                  x[:, n:] + pr * zi + pi * zr], axis=1)

    def out_proj(s):
        y = jnp.dot(xs_ref[s].astype(BF16), c_ref[s], preferred_element_type=F32)
        g = jax.nn.gelu(y + d_ref[:, S5_SLAB * s:S5_SLAB * (s + 1)] * up_ref[s])
        for c in range(halves):
            us_ref[s * halves + c] = g[:, LANES * c:LANES * (c + 1)]
        for j in range(SUBLANES):
            for c in range(halves):
                o_ref[sub * j:sub * (j + 1), S5_SLAB * s + LANES * c:S5_SLAB * s + LANES * (c + 1)] = (
                    us_ref[s * halves + c, pl.ds(j, sub, stride=SUBLANES), :].astype(o_ref.dtype))

    for s in range(S5_PAIR):
        in_proj(s)
    for s in range(S5_PAIR):
        scan(s)
        out_proj(s)


def _s5_tables(a_re, a_im, log_step, b_re, b_im, c_re, c_im, sub):
    g = a_re.shape[0]
    n_slab = g * S5_GROUP // S5_SLAB
    gl = S5_SLAB // S5_GROUP
    dt = jnp.exp(log_step.astype(F32))[:, None]
    ar = a_re.astype(F32)
    ai = a_im.astype(F32)
    mag = jnp.exp(ar * dt)
    lb_re = mag * jnp.cos(ai * dt)
    lb_im = mag * jnp.sin(ai * dt)
    den = ar * ar + ai * ai
    nr = lb_re - 1.0
    coef_re = (nr * ar + lb_im * ai) / den
    coef_im = (lb_im * ar - nr * ai) / den
    bb_re = coef_re[..., None] * b_re - coef_im[..., None] * b_im
    bb_im = coef_re[..., None] * b_im + coef_im[..., None] * b_re
    eye = jnp.eye(gl, dtype=F32)

    def b_slab(t):
        t = t.reshape(n_slab, gl, S5_STATE, S5_GROUP)
        return jnp.einsum("kgpc,gh->kgchp", t, eye).reshape(n_slab, S5_SLAB, gl * S5_STATE)

    def c_slab(t):
        t = t.reshape(n_slab, gl, S5_GROUP, S5_STATE)
        return jnp.einsum("kgcp,gh->kgphc", t, eye).reshape(n_slab, gl * S5_STATE, S5_SLAB)

    b_dense = jnp.concatenate([b_slab(bb_re), b_slab(bb_im)], axis=2).astype(BF16)
    c_dense = jnp.concatenate([c_slab(c_re.astype(F32)), -c_slab(c_im.astype(F32))], axis=1).astype(BF16)

    def flat(t):
        return t.reshape(n_slab, gl * S5_STATE)

    def power(k):
        kk = k.astype(F32)[None, :, None]
        m = jnp.exp(flat(ar * dt)[:, None, :] * kk)
        ph = flat(ai * dt)[:, None, :] * kk
        return jnp.concatenate([m * jnp.cos(ph), m * jnp.sin(ph)], axis=2)

    lam = jnp.repeat(power(jnp.array([1])), SUBLANES, axis=1)
    ptab = jnp.repeat(power(jnp.arange(1, sub + 1)), SUBLANES, axis=1)
    apow = jnp.repeat(power(jnp.array([sub, 2 * sub, 4 * sub])), SUBLANES, axis=1)
    return b_dense, c_dense, lam, ptab, apow


def _s5_mixer(u, a_re, a_im, log_step, b_re, b_im, c_re, c_im, d_skip, sub=S5_SUB):
    seq, d = u.shape
    rows = SUBLANES * sub
    n_slab = d // S5_SLAB
    n2 = 2 * S5_SLAB_STATES
    b_dense, c_dense, lam, ptab, apow = _s5_tables(a_re, a_im, log_step, b_re, b_im, c_re, c_im, sub)
    return pl.pallas_call(
        functools.partial(_s5_kernel, sub=sub),
        out_shape=jax.ShapeDtypeStruct((seq, d), BF16),
        grid=(n_slab // S5_PAIR, seq // rows),
        in_specs=[
            pl.BlockSpec((rows, S5_PAIR * S5_SLAB), lambda k, c: (c, k)),
            pl.BlockSpec((S5_PAIR, S5_SLAB, n2), lambda k, c: (k, 0, 0)),
            pl.BlockSpec((S5_PAIR, n2, S5_SLAB), lambda k, c: (k, 0, 0)),
            pl.BlockSpec((S5_PAIR, SUBLANES, n2), lambda k, c: (k, 0, 0)),
            pl.BlockSpec((S5_PAIR, rows, n2), lambda k, c: (k, 0, 0)),
            pl.BlockSpec((S5_PAIR, 3 * SUBLANES, n2), lambda k, c: (k, 0, 0)),
            pl.BlockSpec((1, S5_PAIR * S5_SLAB), lambda k, c: (0, k)),
        ],
        out_specs=pl.BlockSpec((rows, S5_PAIR * S5_SLAB), lambda k, c: (c, k)),
        scratch_shapes=[pltpu.VMEM((S5_PAIR, rows, n2), F32),
                        pltpu.VMEM((S5_PAIR, SUBLANES, n2), F32),
                        pltpu.VMEM((S5_PAIR, rows, S5_SLAB), F32),
                        pltpu.VMEM((S5_PAIR * S5_SLAB // LANES, rows, LANES), F32)],
        compiler_params=_params("arbitrary", "arbitrary"),
        name="s5_scan",
    )(u, b_dense, c_dense, lam, ptab, apow, d_skip.reshape(1, d).astype(F32))


def _cache_weights(first, pairs):
    @pl.when(first)
    def _():
        for src, dst in pairs:
            dst[...] = src[...].astype(BF16)


def _row_tiles(ref):
    return [slice(r0, r0 + SUB_ROWS) for r0 in range(0, ref.shape[0], SUB_ROWS)]


def _glu_kernel(a_ref, wa_ref, wb_ref, r_ref, o_ref, wa_s, wb_s):
    _cache_weights(pl.program_id(1) == 0, ((wa_ref, wa_s), (wb_ref, wb_s)))
    for rows in _row_tiles(a_ref):
        a = a_ref[rows, :]
        va = jnp.dot(a, wa_s[...], preferred_element_type=F32)
        vb = jnp.dot(a, wb_s[...], preferred_element_type=F32)
        o_ref[rows, :] = r_ref[rows, :] + va * jax.nn.sigmoid(vb)


def _glu_residual(a, w, res, tm=1024, tn=512):
    m, k = a.shape
    n = w.shape[1] // 2
    nb = n // tn
    return pl.pallas_call(
        _glu_kernel,
        out_shape=jax.ShapeDtypeStruct((m, n), F32),
        grid=(nb, m // tm),
        in_specs=[pl.BlockSpec((tm, k), lambda j, i: (i, 0)),
                  pl.BlockSpec((k, tn), lambda j, i: (0, j)),
                  pl.BlockSpec((k, tn), lambda j, i: (0, j + nb)),
                  pl.BlockSpec((tm, tn), lambda j, i: (i, j))],
        out_specs=pl.BlockSpec((tm, tn), lambda j, i: (i, j)),
        scratch_shapes=[pltpu.VMEM((k, tn), BF16), pltpu.VMEM((k, tn), BF16)],
        compiler_params=_params("arbitrary", "arbitrary"),
        name="glu_residual",
    )(a, w, w, res)


def _swiglu_up_kernel(a_ref, wg_ref, wu_ref, o_ref, wg_s, wu_s):
    _cache_weights(pl.program_id(1) == 0, ((wg_ref, wg_s), (wu_ref, wu_s)))
    for rows in _row_tiles(a_ref):
        a = a_ref[rows, :]
        vg = jnp.dot(a, wg_s[...], preferred_element_type=F32)
        vu = jnp.dot(a, wu_s[...], preferred_element_type=F32)
        o_ref[rows, :] = (jax.nn.silu(vg) * vu).astype(o_ref.dtype)


def _swiglu_up(a, w_gu, tm=1024, tn=512):
    m, k = a.shape
    f = w_gu.shape[1] // 2
    nb = f // tn
    return pl.pallas_call(
        _swiglu_up_kernel,
        out_shape=jax.ShapeDtypeStruct((m, f), BF16),
        grid=(nb, m // tm),
        in_specs=[pl.BlockSpec((tm, k), lambda j, i: (i, 0)),
                  pl.BlockSpec((k, tn), lambda j, i: (0, j)),
                  pl.BlockSpec((k, tn), lambda j, i: (0, j + nb))],
        out_specs=pl.BlockSpec((tm, tn), lambda j, i: (i, j)),
        scratch_shapes=[pltpu.VMEM((k, tn), BF16), pltpu.VMEM((k, tn), BF16)],
        compiler_params=_params("arbitrary", "arbitrary"),
        name="swiglu_up",
    )(a, w_gu, w_gu)


def _mm_res_kernel(a_ref, w_ref, r_ref, o_ref, w_s):
    _cache_weights(pl.program_id(1) == 0, ((w_ref, w_s),))
    for rows in _row_tiles(a_ref):
        o_ref[rows, :] = r_ref[rows, :] + jnp.dot(a_ref[rows, :], w_s[...], preferred_element_type=F32)


def _matmul_residual(a, w, res, tm=512, tn=512):
    m, k = a.shape
    n = w.shape[1]
    return pl.pallas_call(
        _mm_res_kernel,
        out_shape=jax.ShapeDtypeStruct((m, n), F32),
        grid=(n // tn, m // tm),
        in_specs=[pl.BlockSpec((tm, k), lambda j, i: (i, 0)),
                  pl.BlockSpec((k, tn), lambda j, i: (0, j)),
                  pl.BlockSpec((tm, tn), lambda j, i: (i, j))],
        out_specs=pl.BlockSpec((tm, tn), lambda j, i: (i, j)),
        scratch_shapes=[pltpu.VMEM((k, tn), BF16)],
        compiler_params=_params("arbitrary", "arbitrary"),
        name="matmul_residual",
    )(a, w, res)


def _rope_kernel(pos_ref, inv_ref, cos_ref, sa_ref, sb_ref):
    ang = pos_ref[...].astype(F32) * inv_ref[...]
    c = jnp.cos(ang)
    s = jnp.sin(ang)
    lane = lax.broadcasted_iota(jnp.int32, ang.shape, 1)
    first_half = lane < ROPE_DIM // 2
    cos_ref[...] = c
    sa_ref[...] = jnp.where(first_half, -s, 0.0)
    sb_ref[...] = jnp.where(first_half, 0.0, s)


def _rope_tables(pos, tm=512):
    n = pos.shape[0]
    tm = min(tm, n)
    half = ROPE_DIM // 2
    inv = jnp.power(ROPE_THETA, -jnp.arange(half, dtype=F32) / half)
    inv = jnp.concatenate([inv, inv, jnp.zeros((LANES - ROPE_DIM,), F32)]).reshape(1, LANES)
    spec = pl.BlockSpec((tm, LANES), lambda i: (i, 0))
    return pl.pallas_call(
        _rope_kernel,
        out_shape=[jax.ShapeDtypeStruct((n, LANES), F32)] * 3,
        grid=(n // tm,),
        in_specs=[pl.BlockSpec((tm, 1), lambda i: (i, 0)),
                  pl.BlockSpec((1, LANES), lambda i: (0, 0))],
        out_specs=[spec, spec, spec],
        compiler_params=_params("arbitrary"),
        name="rope_tables",
    )(pos.reshape(n, 1), inv)


def _rope(y, c, sa, sb):
    half = ROPE_DIM // 2
    return y * c + pltpu.roll(y, LANES - half, 1) * sa + pltpu.roll(y, half, 1) * sb


def _head_norm(x, gain):
    return x * lax.rsqrt(jnp.mean(x * x, axis=-1, keepdims=True) + RMS_EPS) * gain


def _nsa_proj_kernel(a_ref, w_ref, gain_ref, cos_ref, sa_ref, sb_ref, o_ref, w_s, *, norm_tiles, sub_rows):
    j = pl.program_id(0)
    _cache_weights(pl.program_id(1) == 0, ((w_ref, w_s),))
    is_norm = functools.reduce(jnp.logical_or, [j == t for t in norm_tiles])

    @pl.when(is_norm)
    def _():
        gain = gain_ref[0]
        for r0 in range(0, a_ref.shape[0], sub_rows):
            rows = slice(r0, r0 + sub_rows)
            acc = jnp.dot(a_ref[rows, :], w_s[...], preferred_element_type=F32)
            c, sa, sb = cos_ref[rows, :], sa_ref[rows, :], sb_ref[rows, :]
            for hh in range(acc.shape[1] // HEAD_DIM):
                sl = slice(HEAD_DIM * hh, HEAD_DIM * (hh + 1))
                o_ref[rows, sl] = _rope(_head_norm(acc[:, sl], gain), c, sa, sb).astype(o_ref.dtype)

    @pl.when(jnp.logical_not(is_norm))
    def _():
        o_ref[...] = jnp.dot(a_ref[...], w_s[...], preferred_element_type=F32).astype(o_ref.dtype)


def _nsa_proj(u, w_in, k_gain, rope, tm=1024, tn=512, sub_rows=256):
    m, k = u.shape
    q_dim = GQA_GROUP * N_KV_HEADS * HEAD_DIM
    kv_dim = N_KV_HEADS * HEAD_DIM
    assert kv_dim == tn
    n_q = q_dim // tn
    n_tiles = 6
    ones = jnp.ones((HEAD_DIM,), F32)
    gains = jnp.stack([ones, ones, k_gain[1], ones, k_gain[2], ones]).reshape(n_tiles, 1, HEAD_DIM)
    tab = pl.BlockSpec((tm, LANES), lambda j, i: (i, 0))
    return pl.pallas_call(
        functools.partial(_nsa_proj_kernel, norm_tiles=(2, 4), sub_rows=min(sub_rows, tm)),
        out_shape=jax.ShapeDtypeStruct((m, n_tiles * tn), BF16),
        grid=(n_tiles, m // tm),
        in_specs=[pl.BlockSpec((tm, k), lambda j, i: (i, 0)),
                  pl.BlockSpec((k, tn), lambda j, i: (0, j + n_q)),
                  pl.BlockSpec((1, 1, HEAD_DIM), lambda j, i: (j, 0, 0)),
                  tab, tab, tab],
        out_specs=pl.BlockSpec((tm, tn), lambda j, i: (i, j)),
        scratch_shapes=[pltpu.VMEM((k, tn), BF16)],
        compiler_params=_params("arbitrary", "arbitrary"),
        name="nsa_proj",
    )(u, w_in, gains, *rope)


def _nsa_q_kernel(a_ref, w_ref, gain_ref, cos_ref, sa_ref, sb_ref, o_ref, w_s, *, sub_rows):
    _cache_weights(pl.program_id(1) == 0, ((w_ref, w_s),))
    gain = gain_ref[...]
    for r0 in range(0, a_ref.shape[0], sub_rows):
        rows = slice(r0, r0 + sub_rows)
        acc = jnp.dot(a_ref[rows, :], w_s[...], preferred_element_type=F32)
        c, sa, sb = cos_ref[rows, :], sa_ref[rows, :], sb_ref[rows, :]
        for g in range(GQA_GROUP):
            y = _rope(_head_norm(acc[:, HEAD_DIM * g:HEAD_DIM * (g + 1)], gain), c, sa, sb)
            for b in range(sub_rows // Q_BLOCK):
                o_ref[0, r0 // Q_BLOCK + b, :, Q_BLOCK * g:Q_BLOCK * (g + 1)] = (
                    y[Q_BLOCK * b:Q_BLOCK * (b + 1), :].T.astype(o_ref.dtype))


def _nsa_q_proj(u, w_in, q_gain, rope, tm=1024, sub_rows=256):
    m, k = u.shape
    tn = GQA_GROUP * HEAD_DIM
    cols = GQA_GROUP * Q_BLOCK
    q_scaled = (q_gain.astype(F32) * (HEAD_DIM ** -0.5 * math.log2(math.e))).reshape(1, HEAD_DIM)
    tab = pl.BlockSpec((tm, LANES), lambda j, i: (i, 0))
    return pl.pallas_call(
        functools.partial(_nsa_q_kernel, sub_rows=min(sub_rows, tm)),
        out_shape=jax.ShapeDtypeStruct((N_KV_HEADS, m // Q_BLOCK, HEAD_DIM, cols), BF16),
        grid=(N_KV_HEADS, m // tm),
        in_specs=[pl.BlockSpec((tm, k), lambda j, i: (i, 0)),
                  pl.BlockSpec((k, tn), lambda j, i: (0, j)),
                  pl.BlockSpec((1, HEAD_DIM), lambda j, i: (0, 0)),
                  tab, tab, tab],
        out_specs=pl.BlockSpec((1, tm // Q_BLOCK, HEAD_DIM, cols), lambda j, i: (j, i, 0, 0)),
        scratch_shapes=[pltpu.VMEM((k, tn), BF16)],
        compiler_params=_params("arbitrary", "arbitrary"),
        name="nsa_q_proj",
    )(u, w_in, q_scaled, *rope)


def _gate_kernel(a_ref, w_ref, o_ref):
    o_ref[...] = jax.nn.sigmoid(jnp.dot(a_ref[...], w_ref[...].astype(BF16), preferred_element_type=F32))


def _nsa_gates(u, w_gate, tm=512):
    m, k = u.shape
    n = w_gate.shape[1]
    w_pad = jnp.pad(w_gate, ((0, 0), (0, LANES - n)))
    return pl.pallas_call(
        _gate_kernel,
        out_shape=jax.ShapeDtypeStruct((m, LANES), F32),
        grid=(m // tm,),
        in_specs=[pl.BlockSpec((tm, k), lambda i: (i, 0)),
                  pl.BlockSpec((k, LANES), lambda i: (0, 0))],
        out_specs=pl.BlockSpec((tm, LANES), lambda i: (i, 0)),
        compiler_params=_params("arbitrary"),
        name="nsa_gates",
    )(u, w_pad)


def _compress_kernel(*refs, is_key):
    if is_key:
        ca_ref, cb_ref, pe_ref, w1_ref, w2_ref, gain_ref, cos_ref, sa_ref, sb_ref, o_ref = refs
    else:
        ca_ref, cb_ref, pe_ref, w1_ref, w2_ref, o_ref = refs
    half = w1_ref.shape[0] // 2
    pe = pe_ref[...]
    xa = (ca_ref[0].astype(F32) + pe[:, :half]).astype(BF16)
    xb = (cb_ref[0].astype(F32) + pe[:, half:]).astype(BF16)
    hid = (jnp.dot(xa, w1_ref[:half, :].astype(BF16), preferred_element_type=F32)
           + jnp.dot(xb, w1_ref[half:, :].astype(BF16), preferred_element_type=F32))
    out = jnp.dot(jax.nn.gelu(hid).astype(BF16), w2_ref[...].astype(BF16), preferred_element_type=F32)
    if is_key:
        out = _rope(_head_norm(out, gain_ref[...]), cos_ref[...], sa_ref[...], sb_ref[...])
    o_ref[0] = out.astype(o_ref.dtype)


def _compress(t, pe, w1, w2, key_extras=None):
    seq = t.shape[0]
    nc = seq // CMP_STRIDE
    width = CMP_STRIDE * HEAD_DIM
    ca = t.reshape(nc, CMP_STRIDE, N_KV_HEADS, HEAD_DIM).transpose(2, 0, 1, 3).reshape(N_KV_HEADS, nc, width)
    cb = jnp.concatenate([ca[:, 1:], jnp.zeros((N_KV_HEADS, 1, width), ca.dtype)], axis=1)
    blk = pl.BlockSpec((1, nc, width), lambda h: (h, 0, 0))
    full = lambda a: pl.BlockSpec(a.shape, lambda h: (0,) * a.ndim)
    args = [ca, cb, pe.reshape(1, CMP_BLOCK * HEAD_DIM), w1, w2]
    if key_extras is not None:
        args += list(key_extras)
    return pl.pallas_call(
        functools.partial(_compress_kernel, is_key=key_extras is not None),
        out_shape=jax.ShapeDtypeStruct((N_KV_HEADS, nc, HEAD_DIM), BF16),
        grid=(N_KV_HEADS,),
        in_specs=[blk, blk] + [full(a) for a in args[2:]],
        out_specs=pl.BlockSpec((1, nc, HEAD_DIM), lambda h: (h, 0, 0)),
        compiler_params=_params("arbitrary"),
        name="nsa_compress_k" if key_extras is not None else "nsa_compress_v",
    )(*args)


def _dot_nt(a, b):
    return lax.dot_general(a, b, (((1,), (1,)), ((), ())), preferred_element_type=F32)


def _split3(x):
    hi = x.astype(BF16)
    r1 = x - hi.astype(F32)
    mid = r1.astype(BF16)
    lo = (r1 - mid.astype(F32)).astype(BF16)
    return hi, mid, lo


def _nsa_attn_kernel(q_ref, kc_ref, vct_ref, ks_ref, vst_ref, kw_ref, vwt_ref, gate_ref, blk_ref, o_ref,
                     acc_ref, mix_ref, m_ref, l_ref, qa_ref, s0_ref, s1_ref, pc_ref, sw_ref, pw_ref,
                     *, seq, tk):
    t0 = pl.program_id(1) * Q_BLOCK
    nc = kc_ref.shape[1]
    ns = seq // SEL_BLOCK
    grp = GQA_GROUP
    cols = grp * Q_BLOCK
    sel_shift = int(math.log2(SEL_BLOCK))
    qt = q_ref[0, 0]
    t_row = t0 + lax.broadcasted_iota(jnp.int32, (1, Q_BLOCK), 1)

    def heads(x):
        return jnp.concatenate([x] * grp, axis=1)

    span = WINDOW + Q_BLOCK
    w0 = pl.multiple_of(jnp.maximum(t0 - WINDOW, 0), Q_BLOCK)
    gate = gate_ref[0, 0]
    n_idx = lax.broadcasted_iota(jnp.int32, (nc, Q_BLOCK), 0)
    ok_c = (n_idx * CMP_STRIDE + (CMP_BLOCK - 1) <= t_row) & (n_idx < nc - 1)
    sb_c = jnp.dot(kc_ref[0], qt, preferred_element_type=F32) + heads(jnp.where(ok_c, 0.0, NEG))
    m_c = jnp.max(sb_c, axis=0, keepdims=True)
    e_c = jnp.exp2(sb_c - m_c)
    den_c = jnp.maximum(jnp.sum(e_c, axis=0, keepdims=True), 1e-30)
    p_c = e_c * jnp.where(m_c > 0.5 * NEG, 1.0 / den_c, 0.0)
    pc_ref[...] = p_c.astype(BF16)
    imp = p_c[:, :Q_BLOCK]
    for g in range(1, grp):
        imp = imp + p_c[:, Q_BLOCK * g:Q_BLOCK * (g + 1)]

    ratio = SEL_BLOCK // CMP_STRIDE
    d = (lax.broadcasted_iota(jnp.int32, (ns, nc), 1)
         - ratio * lax.broadcasted_iota(jnp.int32, (ns, nc), 0))
    overlap = jnp.zeros((ns, nc), F32)
    for n in range(CMP_BLOCK // CMP_STRIDE):
        overlap = overlap + jnp.where((d - n >= 0) & (d - n < ratio), 1.0, 0.0)
    overlap = overlap.astype(BF16)
    p_slc = sum(jnp.dot(overlap, part, preferred_element_type=F32) for part in _split3(imp))

    mix_ref[...] = gate[0:1] * jnp.dot(vct_ref[0], pc_ref[...], preferred_element_type=F32)
    rel = t_row - (w0 + lax.broadcasted_iota(jnp.int32, (span, Q_BLOCK), 0))
    sw_ref[...] = (jnp.dot(kw_ref[pl.ds(w0, span), :], qt, preferred_element_type=F32)
                   + heads(jnp.where((rel >= 0) & (rel < WINDOW), 0.0, NEG)))
    m_w = jnp.max(sw_ref[...], axis=0, keepdims=True)

    j_idx = lax.broadcasted_iota(jnp.int32, (ns, Q_BLOCK), 0)
    j_f = j_idx.astype(F32)
    dist = jnp.right_shift(t_row, sel_shift) - j_idx
    forced = (j_idx == 0) | ((dist >= 0) & (dist < SEL_LOCAL))
    score = jnp.where(forced, jnp.inf, jnp.where(dist >= 0, p_slc, -jnp.inf))
    sel = jnp.zeros((ns, Q_BLOCK), F32)
    k_top = min(SEL_TOPK, ns)
    pack = 2 * SUBLANES
    cuts = [span // pack * r // k_top * pack for r in range(k_top + 1)]
    den_w = jnp.zeros((1, cols), F32)
    for r in range(k_top):
        top = jnp.max(score, axis=0, keepdims=True)
        idx = jnp.min(jnp.where(score == top, j_f, float(ns)), axis=0, keepdims=True)
        pick = j_f == idx
        sel = jnp.where(pick, 1.0, sel)
        score = jnp.where(pick, -jnp.inf, score)
        if cuts[r + 1] > cuts[r]:
            e_w = jnp.exp2(sw_ref[cuts[r]:cuts[r + 1], :] - m_w)
            den_w = den_w + jnp.sum(e_w, axis=0, keepdims=True)
            pw_ref[cuts[r]:cuts[r + 1], :] = e_w.astype(BF16)
    o_w = jnp.dot(vwt_ref[0, :, pl.ds(w0, span)], pw_ref[...], preferred_element_type=F32)
    mix_ref[...] += (gate[2:3] * (1.0 / den_w)) * o_w

    qa_ref[...] = jnp.concatenate([qt, heads(jnp.where(sel > 0.0, 0.0, NEG).astype(BF16))], axis=0)
    acc_ref[...] = jnp.zeros_like(acc_ref)
    m_ref[...] = jnp.full_like(m_ref, NEG)
    l_ref[...] = jnp.zeros_like(l_ref)

    def score_tile(kt, dst):
        k0 = pl.multiple_of(kt * tk, tk)
        k_aug = jnp.concatenate([ks_ref[pl.ds(k0, tk), :], blk_ref[pl.ds(k0, tk), :]], axis=1)
        dst[...] = jnp.dot(k_aug, qa_ref[...], preferred_element_type=F32)

    def consume_tile(kt, src, diagonal):
        k0 = pl.multiple_of(kt * tk, tk)
        sb = src[...]
        if diagonal:
            kpos = k0 + lax.broadcasted_iota(jnp.int32, (tk, Q_BLOCK), 0)
            sb = sb + heads(jnp.where(kpos <= t_row, 0.0, NEG))
        m_i = m_ref[...]
        m_new = jnp.maximum(m_i, jnp.max(sb, axis=0, keepdims=True))
        e = jnp.exp2(sb - m_new)
        alpha = jnp.exp2(m_i - m_new)
        m_ref[...] = m_new
        l_ref[...] = alpha * l_ref[...] + jnp.sum(e, axis=0, keepdims=True)
        acc_ref[...] = alpha * acc_ref[...] + jnp.dot(vst_ref[0, :, pl.ds(k0, tk)], e.astype(BF16),
                                                      preferred_element_type=F32)

    last = t0 // tk
    score_tile(0, s0_ref)

    def tile_pair(i, _):
        score_tile(2 * i + 1, s1_ref)
        consume_tile(2 * i, s0_ref, False)
        score_tile(2 * i + 2, s0_ref)
        consume_tile(2 * i + 1, s1_ref, False)
        return 0

    lax.fori_loop(0, last // 2, tile_pair, 0)

    @pl.when(last % 2 == 1)
    def _():
        score_tile(last, s1_ref)
        consume_tile(last - 1, s0_ref, False)
        consume_tile(last, s1_ref, True)

    @pl.when(last % 2 == 0)
    def _():
        consume_tile(last, s0_ref, True)

    o_s = acc_ref[...] * (1.0 / jnp.maximum(l_ref[...], 1e-30))

    mixed = mix_ref[...] + gate_ref[0, 0, 1:2] * o_s
    for g in range(grp):
        o_ref[:, HEAD_DIM * g:HEAD_DIM * (g + 1)] = mixed[:, Q_BLOCK * g:Q_BLOCK * (g + 1)].T.astype(o_ref.dtype)


def _nsa_attention(q_t, proj, kcmp, vcmp_t, vsl_t, vw_t, gates, tk=1024):
    seq = proj.shape[0]
    tk = min(tk, seq)
    q_dim = GQA_GROUP * N_KV_HEADS * HEAD_DIM
    kv_blocks = N_KV_HEADS
    first = 2 * kv_blocks
    nc = kcmp.shape[1]
    cols = GQA_GROUP * Q_BLOCK

    def k_spec(which):
        return pl.BlockSpec((seq, HEAD_DIM), lambda h, qb: (0, first + which * kv_blocks + h))

    def vt_spec(n):
        return pl.BlockSpec((1, HEAD_DIM, n), lambda h, qb: (h, 0, 0))

    o_spec = pl.BlockSpec((Q_BLOCK, GQA_GROUP * HEAD_DIM), lambda h, qb: (qb, h))
    q_spec = pl.BlockSpec((1, 1, HEAD_DIM, cols), lambda h, qb: (h, qb, 0, 0))
    ns = seq // SEL_BLOCK
    key_block = (jnp.arange(seq, dtype=jnp.int32)[:, None] // SEL_BLOCK
                 == jnp.arange(ns, dtype=jnp.int32)[None, :]).astype(BF16)
    return pl.pallas_call(
        functools.partial(_nsa_attn_kernel, seq=seq, tk=tk),
        out_shape=jax.ShapeDtypeStruct((seq, q_dim), BF16),
        grid=(N_KV_HEADS, seq // Q_BLOCK),
        in_specs=[q_spec, pl.BlockSpec((1, nc, HEAD_DIM), lambda h, qb: (h, 0, 0)), vt_spec(nc),
                  k_spec(0), vt_spec(seq), k_spec(2), vt_spec(seq),
                  pl.BlockSpec((1, 1, 3, cols), lambda h, qb: (h, qb, 0, 0)),
                  pl.BlockSpec((seq, ns), lambda h, qb: (0, 0))],
        out_specs=o_spec,
        scratch_shapes=[pltpu.VMEM((HEAD_DIM, cols), F32), pltpu.VMEM((HEAD_DIM, cols), F32),
                        pltpu.VMEM((1, cols), F32), pltpu.VMEM((1, cols), F32),
                        pltpu.VMEM((HEAD_DIM + ns, cols), BF16),
                        pltpu.VMEM((tk, cols), F32), pltpu.VMEM((tk, cols), F32),
                        pltpu.VMEM((nc, cols), BF16), pltpu.VMEM((WINDOW + Q_BLOCK, cols), F32),
                        pltpu.VMEM((WINDOW + Q_BLOCK, cols), BF16)],
        compiler_params=_params("arbitrary", "arbitrary"),
        name="nsa_attention",
    )(q_t, kcmp, vcmp_t, proj, vsl_t, proj, vw_t, gates, key_block)


def _nsa_mixer(u, positions, w_in, q_gain, k_gain, pe_k, pe_v, ck_w1, ck_w2, cv_w1, cv_w2):
    seq = u.shape[0]
    q_dim = GQA_GROUP * N_KV_HEADS * HEAD_DIM
    kv_dim = N_KV_HEADS * HEAD_DIM
    n_main = q_dim + 6 * kv_dim
    nc = seq // CMP_STRIDE
    rope = _rope_tables(positions)
    q_t = _nsa_q_proj(u, w_in, q_gain, rope)
    proj = _nsa_proj(u, w_in, k_gain, rope)
    gate = _nsa_gates(u, w_in[:, n_main:])
    gates = (gate[:, :3 * N_KV_HEADS * GQA_GROUP].reshape(seq // Q_BLOCK, Q_BLOCK, 3, N_KV_HEADS, GQA_GROUP)
             .transpose(3, 0, 2, 4, 1).reshape(N_KV_HEADS, seq // Q_BLOCK, 3, GQA_GROUP * Q_BLOCK))

    def keys_last(cols):
        return cols.reshape(seq, N_KV_HEADS, HEAD_DIM).transpose(1, 2, 0)

    pos_cmp = jnp.concatenate([positions[CMP_BLOCK - 1::CMP_STRIDE][:nc - 1], jnp.zeros((1,), positions.dtype)])
    rope_cmp = _rope_tables(pos_cmp)
    kcmp = _compress(proj[:, :kv_dim], pe_k, ck_w1, ck_w2,
                     key_extras=(k_gain[0].reshape(1, HEAD_DIM),) + tuple(rope_cmp))
    vcmp = _compress(proj[:, kv_dim:2 * kv_dim], pe_v, cv_w1, cv_w2)
    vsl_t = keys_last(proj[:, 3 * kv_dim:4 * kv_dim])
    vw_t = keys_last(proj[:, 5 * kv_dim:6 * kv_dim])
    return _nsa_attention(q_t, proj, kcmp, vcmp.transpose(0, 2, 1), vsl_t, vw_t, gates)


def _router_kernel(x_ref, g_ref, w_ref, b_ref, u_ref, r_ref):
    x = x_ref[...]
    u = x * lax.rsqrt(jnp.mean(x * x, axis=-1, keepdims=True) + RMS_EPS) * g_ref[...]
    u_ref[...] = u
    uh, um, _ = _split3(u)
    wh, wm, _ = _split3(w_ref[...])
    logits = (jnp.dot(uh, wh, preferred_element_type=F32) + jnp.dot(uh, wm, preferred_element_type=F32)
              + jnp.dot(um, wh, preferred_element_type=F32)) + b_ref[...]
    lane = lax.broadcasted_iota(jnp.int32, logits.shape, 1).astype(F32)
    lg = jnp.where(lane < N_EXPERTS, logits, -jnp.inf)
    v1 = jnp.max(lg, axis=-1, keepdims=True)
    i1 = jnp.min(jnp.where(lg == v1, lane, float(LANES)), axis=-1, keepdims=True)
    lg = jnp.where(lane == i1, -jnp.inf, lg)
    v2 = jnp.max(lg, axis=-1, keepdims=True)
    i2 = jnp.min(jnp.where(lg == v2, lane, float(LANES)), axis=-1, keepdims=True)
    e2 = jnp.exp(v2 - v1)
    den = 1.0 + e2
    r_ref[...] = jnp.where(lane == 0, i1, jnp.where(lane == 1, i2, jnp.where(
        lane == 2, 1.0 / den, jnp.where(lane == 3, e2 / den, 0.0))))


def _router(h, gain, w_router, b_router, tm=256):
    m, d = h.shape
    w_pad = jnp.pad(w_router.astype(F32), ((0, 0), (0, LANES - N_EXPERTS)))
    b_pad = jnp.pad(b_router.astype(F32), (0, LANES - N_EXPERTS)).reshape(1, LANES)
    return pl.pallas_call(
        _router_kernel,
        out_shape=[jax.ShapeDtypeStruct((m, d), F32), jax.ShapeDtypeStruct((m, LANES), F32)],
        grid=(m // tm,),
        in_specs=[pl.BlockSpec((tm, d), lambda i: (i, 0)),
                  pl.BlockSpec((1, d), lambda i: (0, 0)),
                  pl.BlockSpec((d, LANES), lambda i: (0, 0)),
                  pl.BlockSpec((1, LANES), lambda i: (0, 0))],
        out_specs=[pl.BlockSpec((tm, d), lambda i: (i, 0)), pl.BlockSpec((tm, LANES), lambda i: (i, 0))],
        compiler_params=_params("arbitrary"),
        name="moe_router",
    )(h, gain.reshape(1, d), w_pad, b_pad)


def _row_copy(src_hbm, row, dst, r, sem):
    return pltpu.make_async_copy(src_hbm.at[pl.ds(row, 1), :], dst.at[pl.ds(r, 1), :], sem)


def _gather_kernel(idx_ref, used_ref, src_hbm, o_ref, buf, sem):
    rows = o_ref.shape[0]
    i = pl.program_id(0)
    n_used = used_ref[0]

    def issue(blk):
        slot = blk % 2

        def start(r, _):
            _row_copy(src_hbm, idx_ref[blk * rows + r], buf.at[slot], r, sem.at[slot]).start()
            return 0

        lax.fori_loop(0, rows, start, 0, unroll=DMA_UNROLL)

    @pl.when(i == 0)
    def _():
        issue(i)

    @pl.when(i + 1 < n_used)
    def _():
        issue(i + 1)

    @pl.when(i < n_used)
    def _():
        slot = i % 2

        def wait(r, _):
            _row_copy(src_hbm, 0, buf.at[slot], r, sem.at[slot]).wait()
            return 0

        lax.fori_loop(0, rows, wait, 0, unroll=DMA_UNROLL)
        o_ref[...] = buf[slot].astype(o_ref.dtype)

    @pl.when(i >= n_used)
    def _():
        o_ref[...] = jnp.zeros_like(o_ref)


def _gather_rows(src, idx, n_used, out_dtype, rows=MOE_ROWS):
    n = idx.shape[0]
    d = src.shape[1]
    return pl.pallas_call(
        _gather_kernel,
        out_shape=jax.ShapeDtypeStruct((n, d), out_dtype),
        grid_spec=pltpu.PrefetchScalarGridSpec(
            num_scalar_prefetch=2,
            grid=(n // rows,),
            in_specs=[pl.BlockSpec(memory_space=pl.ANY)],
            out_specs=pl.BlockSpec((rows, d), lambda i, idx, used: (i, 0)),
            scratch_shapes=[pltpu.VMEM((2, rows, d), src.dtype), pltpu.SemaphoreType.DMA((2,))]),
        compiler_params=_params("arbitrary"),
        name="moe_gather",
    )(idx, n_used, src)


def _block_state(be_ref, used_ref, i):
    changed = (i == 0) | (be_ref[i] != be_ref[jnp.maximum(i - 1, 0)])
    used = i < used_ref[0]
    return used, used & changed


def _last_used(i, used):
    return jnp.minimum(i, used[0] - 1)


def _stream_expert_weights(first, be_ref, rix_ref, rune_ref, nrun_ref, n_tiles, copies, cast):
    j = pl.program_id(0)
    i = pl.program_id(1)

    @pl.when(first)
    def _():
        rix = rix_ref[i]
        n_runs = nrun_ref[0]
        g = j * n_runs + rix
        slot = g % 2

        @pl.when(g == 0)
        def _():
            for c in copies(be_ref[i], j, slot):
                c.start()

        more = rix + 1 < n_runs
        e_next = jnp.where(more, rune_ref[jnp.minimum(rix + 1, N_EXPERTS - 1)], rune_ref[0])
        j_next = jnp.where(more, j, j + 1)

        @pl.when(j_next < n_tiles)
        def _():
            for c in copies(e_next, j_next, 1 - slot):
                c.start()

        for c in copies(be_ref[i], j, slot):
            c.wait()
        cast(slot)


def _moe_up_kernel(be_ref, used_ref, rix_ref, rune_ref, nrun_ref, x_ref, w_hbm, o_ref, wbuf, wg_s, wu_s, sem,
                   *, nb, tn):
    used, first = _block_state(be_ref, used_ref, pl.program_id(1))

    def copies(e, jj, slot):
        return [pltpu.make_async_copy(w_hbm.at[e, :, pl.ds(pl.multiple_of((jj + m * nb) * tn, tn), tn)],
                                      wbuf.at[slot, m], sem.at[slot, m]) for m in range(2)]

    def cast(slot):
        wg_s[...] = wbuf[slot, 0].astype(BF16)
        wu_s[...] = wbuf[slot, 1].astype(BF16)

    _stream_expert_weights(first, be_ref, rix_ref, rune_ref, nrun_ref, nb, copies, cast)

    @pl.when(used)
    def _():
        for rows in _row_tiles(x_ref):
            a = x_ref[rows, :]
            vg = jnp.dot(a, wg_s[...], preferred_element_type=F32)
            vu = jnp.dot(a, wu_s[...], preferred_element_type=F32)
            o_ref[rows, :] = (jax.nn.silu(vg) * vu).astype(o_ref.dtype)

    @pl.when(jnp.logical_not(used))
    def _():
        o_ref[...] = jnp.zeros_like(o_ref)


def _moe_up(x_rows, sched, w_gu, tn=512, rows=MOE_ROWS):
    n, k = x_rows.shape
    f = w_gu.shape[2] // 2
    nb = f // tn
    return pl.pallas_call(
        functools.partial(_moe_up_kernel, nb=nb, tn=tn),
        out_shape=jax.ShapeDtypeStruct((n, f), BF16),
        grid_spec=pltpu.PrefetchScalarGridSpec(
            num_scalar_prefetch=len(sched),
            grid=(nb, n // rows),
            in_specs=[pl.BlockSpec((rows, k), lambda j, i, be, nu, *_: (_last_used(i, nu), 0)),
                      pl.BlockSpec(memory_space=pl.ANY)],
            out_specs=pl.BlockSpec((rows, tn), lambda j, i, *_: (i, j)),
            scratch_shapes=[pltpu.VMEM((2, 2, k, tn), F32), pltpu.VMEM((k, tn), BF16),
                            pltpu.VMEM((k, tn), BF16), pltpu.SemaphoreType.DMA((2, 2))]),
        compiler_params=_params("arbitrary", "arbitrary"),
        name="moe_up",
    )(*sched, x_rows, w_gu)


def _moe_down_kernel(be_ref, used_ref, rix_ref, rune_ref, nrun_ref, a_ref, w_hbm, o_ref, wbuf, w_s, sem,
                     *, nb, tn):
    used, first = _block_state(be_ref, used_ref, pl.program_id(1))

    def copies(e, jj, slot):
        return [pltpu.make_async_copy(w_hbm.at[e, :, pl.ds(pl.multiple_of(jj * tn, tn), tn)],
                                      wbuf.at[slot], sem.at[slot])]

    def cast(slot):
        w_s[...] = wbuf[slot].astype(BF16)

    _stream_expert_weights(first, be_ref, rix_ref, rune_ref, nrun_ref, nb, copies, cast)

    @pl.when(used)
    def _():
        o_ref[...] = jnp.dot(a_ref[...], w_s[...], preferred_element_type=F32)

    @pl.when(jnp.logical_not(used))
    def _():
        o_ref[...] = jnp.zeros_like(o_ref)


def _moe_down(act, sched, w_down, tn=512, rows=MOE_ROWS):
    n, k = act.shape
    d = w_down.shape[2]
    nb = d // tn
    return pl.pallas_call(
        functools.partial(_moe_down_kernel, nb=nb, tn=tn),
        out_shape=jax.ShapeDtypeStruct((n, d), F32),
        grid_spec=pltpu.PrefetchScalarGridSpec(
            num_scalar_prefetch=len(sched),
            grid=(nb, n // rows),
            in_specs=[pl.BlockSpec((rows, k), lambda j, i, be, nu, *_: (_last_used(i, nu), 0)),
                      pl.BlockSpec(memory_space=pl.ANY)],
            out_specs=pl.BlockSpec((rows, tn), lambda j, i, *_: (i, j)),
            scratch_shapes=[pltpu.VMEM((2, k, tn), F32), pltpu.VMEM((k, tn), BF16),
                            pltpu.SemaphoreType.DMA((2,))]),
        compiler_params=_params("arbitrary", "arbitrary"),
        name="moe_down",
    )(*sched, act, w_down)


def _combine_kernel(dest_ref, h_ref, r_ref, rows_hbm, o_ref, buf, sem):
    tm = h_ref.shape[0]
    base = pl.program_id(0) * tm

    def start(r, _):
        for k in range(2):
            _row_copy(rows_hbm, dest_ref[2 * (base + r) + k], buf.at[k], r, sem.at[k]).start()
        return 0

    def wait(r, _):
        for k in range(2):
            _row_copy(rows_hbm, 0, buf.at[k], r, sem.at[k]).wait()
        return 0

    lax.fori_loop(0, tm, start, 0, unroll=DMA_UNROLL)
    lax.fori_loop(0, tm, wait, 0, unroll=DMA_UNROLL)
    w = r_ref[...]
    o_ref[...] = h_ref[...] + (w[:, 2:3] * buf[0] + w[:, 3:4] * buf[1])


def _moe_combine(h, route, out_rows, dest, tm=256):
    m, d = h.shape
    return pl.pallas_call(
        _combine_kernel,
        out_shape=jax.ShapeDtypeStruct((m, d), F32),
        grid_spec=pltpu.PrefetchScalarGridSpec(
            num_scalar_prefetch=1,
            grid=(m // tm,),
            in_specs=[pl.BlockSpec((tm, d), lambda i, dest: (i, 0)),
                      pl.BlockSpec((tm, LANES), lambda i, dest: (i, 0)),
                      pl.BlockSpec(memory_space=pl.ANY)],
            out_specs=pl.BlockSpec((tm, d), lambda i, dest: (i, 0)),
            scratch_shapes=[pltpu.VMEM((2, tm, d), F32), pltpu.SemaphoreType.DMA((2,))]),
        compiler_params=_params("arbitrary"),
        name="moe_combine",
    )(dest.reshape(-1), h, route, out_rows)


def _moe_layout(top_e, rows=MOE_ROWS):
    n_tok = top_e.shape[0]
    e_flat = top_e.reshape(-1)
    onehot = (e_flat[:, None] == jnp.arange(N_EXPERTS, dtype=jnp.int32)[None, :]).astype(jnp.int32)
    csum = jnp.cumsum(onehot, axis=0)
    rank = jnp.take_along_axis(csum, e_flat[:, None], axis=1)[:, 0] - 1
    counts = csum[-1]
    padded = (counts + rows - 1) // rows * rows
    pad_end = jnp.cumsum(padded)
    dest = (pad_end - padded)[e_flat] + rank
    n_rows = e_flat.shape[0] + N_EXPERTS * rows
    t_flat = jnp.repeat(jnp.arange(n_tok, dtype=jnp.int32), top_e.shape[1])
    row_tok = jnp.zeros((n_rows,), jnp.int32).at[dest].set(t_flat)
    n_blk = n_rows // rows
    blk_start = jnp.arange(n_blk, dtype=jnp.int32) * rows
    blk_e = jnp.minimum(jnp.sum(blk_start[:, None] >= pad_end[None, :], axis=1), N_EXPERTS - 1).astype(jnp.int32)
    n_used = (pad_end[-1:] // rows).astype(jnp.int32)
    first = (jnp.arange(n_blk) < n_used[0]) & (blk_e != jnp.concatenate([blk_e[:1] - 1, blk_e[:-1]]))
    run_ix = (jnp.cumsum(first) - 1).astype(jnp.int32)
    in_run = first[:, None] & (run_ix[:, None] == jnp.arange(N_EXPERTS, dtype=jnp.int32)[None, :])
    run_e = jnp.sum(jnp.where(in_run, blk_e[:, None], 0), axis=0).astype(jnp.int32)
    n_runs = jnp.sum(first).astype(jnp.int32).reshape(1)
    sched = (blk_e, n_used, run_ix, run_e, n_runs)
    return row_tok, sched, dest.astype(jnp.int32).reshape(n_tok, -1)


def _moe_ffn_residual(h, gain, w_router, b_router, w_gu, w_down):
    u, route = _router(h, gain, w_router, b_router)
    top_e = route[:, :2].astype(jnp.int32)
    row_tok, sched, dest = _moe_layout(top_e)
    x_rows = _gather_rows(u, row_tok, sched[1], BF16)
    act = _moe_up(x_rows, sched, w_gu)
    out_rows = _moe_down(act, sched, w_down)
    return _moe_combine(h, route, out_rows, dest)


def kernel(x, positions, norm_mix, norm_ffn, s5_a_re, s5_a_im, s5_log_step, s5_b_re, s5_b_im, s5_c_re, s5_c_im, s5_d, s5_w_glu, nsa_w_in, nsa_q_gain, nsa_k_gain, nsa_pe_k, nsa_pe_v, nsa_ck_w1, nsa_ck_w2, nsa_cv_w1, nsa_cv_w2, nsa_w_out, ffn_w_gu, ffn_w_down, moe_w_router, moe_b_router, moe_w_gu, moe_w_down):
    bsz, seq, d = x.shape
    assert bsz == 1, "the scan and attention kernels take one sequence"
    h = x.reshape(seq, d)
    h = _layer_s5(h, norm_mix[0], norm_ffn[0], s5_a_re[0], s5_a_im[0], s5_log_step[0], s5_b_re[0],
                  s5_b_im[0], s5_c_re[0], s5_c_im[0], s5_d[0], s5_w_glu[0], ffn_w_gu[0], ffn_w_down[0])
    h = _layer_nsa(h, positions[0], norm_mix[1], norm_ffn[1], nsa_w_in[0], nsa_q_gain[0], nsa_k_gain[0],
                   nsa_pe_k[0], nsa_pe_v[0], nsa_ck_w1[0], nsa_ck_w2[0], nsa_cv_w1[0], nsa_cv_w2[0],
                   nsa_w_out[0], moe_w_router[0], moe_b_router[0], moe_w_gu[0], moe_w_down[0])
    return h.reshape(bsz, seq, d)


def _layer_nsa(h, positions, g_mix, g_ffn, w_in, q_gain, k_gain, pe_k, pe_v, ck_w1, ck_w2, cv_w1, cv_w2,
               w_out, w_router, b_router, w_gu, w_down):
    u = _rms_norm(h, g_mix, BF16)
    o = _nsa_mixer(u, positions, w_in, q_gain, k_gain, pe_k, pe_v, ck_w1, ck_w2, cv_w1, cv_w2)
    h = _matmul_residual(o, w_out, h, tm=1024)
    return _moe_ffn_residual(h, g_ffn, w_router, b_router, w_gu, w_down)


def _layer_s5(h, g_mix, g_ffn, a_re, a_im, log_step, b_re, b_im, c_re, c_im, d_skip, w_glu, w_gu, w_down):
    u = _rms_norm(h, g_mix, F32)
    g = _s5_mixer(u, a_re, a_im, log_step, b_re, b_im, c_re, c_im, d_skip)
    h = _glu_residual(g, w_glu, h)
    u = _rms_norm(h, g_ffn, BF16)
    act = _swiglu_up(u, w_gu)
    return _matmul_residual(act, w_down, h)
```

```python
import functools
import math

import jax
import jax.numpy as jnp
from jax import lax
from jax.experimental import pallas as pl
from jax.experimental.pallas import tpu as pltpu

F32 = jnp.float32
BF16 = jnp.bfloat16

RMS_EPS = 1e-6
S5_GROUP = 16
S5_STATE = 64
HEAD_DIM = 128
N_KV_HEADS = 4
GQA_GROUP = 4
ROPE_DIM = 32
ROPE_THETA = 500000.0
CMP_BLOCK = 32
CMP_STRIDE = 16
SEL_BLOCK = 64
SEL_TOPK = 16
SEL_LOCAL = 2
WINDOW = 512
Q_BLOCK = 128
N_EXPERTS = 8
NEG = -1e30

LANES = 128
SUBLANES = 8
VMEM_LIMIT = 56 * 1024 * 1024

S5_SLAB = 256
S5_SLAB_STATES = S5_SLAB // S5_GROUP * S5_STATE
S5_SUB = 64
S5_PAIR = 2
MOE_ROWS = 512
SUB_ROWS = 256
DMA_UNROLL = 8


def _params(*sem):
    return pltpu.CompilerParams(dimension_semantics=sem, vmem_limit_bytes=VMEM_LIMIT)


def _rms_kernel(x_ref, g_ref, o_ref):
    x = x_ref[...]
    ms = jnp.mean(x * x, axis=-1, keepdims=True)
    o_ref[...] = (x * lax.rsqrt(ms + RMS_EPS) * g_ref[...]).astype(o_ref.dtype)


def _rms_norm(x, gain, out_dtype, tm=512):
    m, d = x.shape
    return pl.pallas_call(
        _rms_kernel,
        out_shape=jax.ShapeDtypeStruct((m, d), out_dtype),
        grid=(m // tm,),
        in_specs=[pl.BlockSpec((tm, d), lambda i: (i, 0)),
                  pl.BlockSpec((1, d), lambda i: (0, 0))],
        out_specs=pl.BlockSpec((tm, d), lambda i: (i, 0)),
        compiler_params=_params("arbitrary"),
        name="rms_norm",
    )(x, gain.reshape(1, d))


def _s5_kernel(u_ref, b_ref, c_ref, lam_ref, ptab_ref, apow_ref, d_ref, o_ref,
               xs_ref, carry_ref, up_ref, us_ref, *, sub):
    n = S5_SLAB_STATES
    halves = S5_SLAB // LANES
    row = lax.broadcasted_iota(jnp.int32, (SUBLANES, n), 0)

    @pl.when(pl.program_id(1) == 0)
    def _():
        carry_ref[...] = jnp.zeros_like(carry_ref)

    def in_proj(s):
        for c in range(halves):
            us_ref[s * halves + c] = u_ref[:, S5_SLAB * s + LANES * c:S5_SLAB * s + LANES * (c + 1)]
        for i in range(sub):
            for c in range(halves):
                up_ref[s, SUBLANES * i:SUBLANES * (i + 1), LANES * c:LANES * (c + 1)] = (
                    us_ref[s * halves + c, pl.ds(i, SUBLANES, stride=sub), :])
        xs_ref[s] = jnp.dot(up_ref[s].astype(BF16), b_ref[s], preferred_element_type=F32)

    def scan(s):
        lam = lam_ref[s]
        lr, li = lam[:, :n], lam[:, n:]
        hr = hi = jnp.zeros((SUBLANES, n), F32)
        for i in range(sub):
            rows = slice(SUBLANES * i, SUBLANES * (i + 1))
            x = xs_ref[s, rows, :]
            hr, hi = lr * hr - li * hi + x[:, :n], lr * hi + li * hr + x[:, n:]
            xs_ref[s, rows, :] = jnp.concatenate([hr, hi], axis=1)
        er, ei = hr, hi

        cin = carry_ref[s]
        zr = jnp.where(row == 0, cin[:, :n], pltpu.roll(er, 1, 0))
        zi = jnp.where(row == 0, cin[:, n:], pltpu.roll(ei, 1, 0))
        apow = apow_ref[s]
        for k, d in enumerate((1, 2, 4)):
            ar = apow[SUBLANES * k:SUBLANES * (k + 1), :n]
            ai = apow[SUBLANES * k:SUBLANES * (k + 1), n:]
            sr = pltpu.roll(zr, d, 0)
            si = pltpu.roll(zi, d, 0)
            keep = row >= d
            zr, zi = (zr + jnp.where(keep, ar * sr - ai * si, 0.0),
                      zi + jnp.where(keep, ar * si + ai * sr, 0.0))
        a1r, a1i = apow[:SUBLANES, :n], apow[:SUBLANES, n:]
        nxt_r = a1r * zr - a1i * zi + er
        nxt_i = a1r * zi + a1i * zr + ei
        carry_ref[s] = jnp.concatenate(
            [jnp.broadcast_to(nxt_r[SUBLANES - 1:, :], (SUBLANES, n)),
             jnp.broadcast_to(nxt_i[SUBLANES - 1:, :], (SUBLANES, n))], axis=1)

        for i in range(sub):
            rows = slice(SUBLANES * i, SUBLANES * (i + 1))
            x = xs_ref[s, rows, :]
            p = ptab_ref[s, rows, :]
            pr, pi = p[:, :n], p[:, n:]
            xs_ref[s, rows, :] = jnp.concatenate([x[:, :n] + pr * zr - pi * zi,
                                                  x[:, n:] + pr * zi + pi * zr], axis=1)

    def out_proj(s):
        y = jnp.dot(xs_ref[s].astype(BF16), c_ref[s], preferred_element_type=F32)
        g = jax.nn.gelu(y + d_ref[:, S5_SLAB * s:S5_SLAB * (s + 1)] * up_ref[s])
        for c in range(halves):
            us_ref[s * halves + c] = g[:, LANES * c:LANES * (c + 1)]
        for j in range(SUBLANES):
            for c in range(halves):
                o_ref[sub * j:sub * (j + 1), S5_SLAB * s + LANES * c:S5_SLAB * s + LANES * (c + 1)] = (
                    us_ref[s * halves + c, pl.ds(j, sub, stride=SUBLANES), :].astype(o_ref.dtype))

    for s in range(S5_PAIR):
        in_proj(s)
    for s in range(S5_PAIR):
        scan(s)
        out_proj(s)


def _s5_tables(a_re, a_im, log_step, b_re, b_im, c_re, c_im, sub):
    g = a_re.shape[0]
    n_slab = g * S5_GROUP // S5_SLAB
    gl = S5_SLAB // S5_GROUP
    dt = jnp.exp(log_step.astype(F32))[:, None]
    ar = a_re.astype(F32)
    ai = a_im.astype(F32)
    mag = jnp.exp(ar * dt)
    lb_re = mag * jnp.cos(ai * dt)
    lb_im = mag * jnp.sin(ai * dt)
    den = ar * ar + ai * ai
    nr = lb_re - 1.0
    coef_re = (nr * ar + lb_im * ai) / den
    coef_im = (lb_im * ar - nr * ai) / den
    bb_re = coef_re[..., None] * b_re - coef_im[..., None] * b_im
    bb_im = coef_re[..., None] * b_im + coef_im[..., None] * b_re
    eye = jnp.eye(gl, dtype=F32)

    def b_slab(t):
        t = t.reshape(n_slab, gl, S5_STATE, S5_GROUP)
        return jnp.einsum("kgpc,gh->kgchp", t, eye).reshape(n_slab, S5_SLAB, gl * S5_STATE)

    def c_slab(t):
        t = t.reshape(n_slab, gl, S5_GROUP, S5_STATE)
        return jnp.einsum("kgcp,gh->kgphc", t, eye).reshape(n_slab, gl * S5_STATE, S5_SLAB)

    b_dense = jnp.concatenate([b_slab(bb_re), b_slab(bb_im)], axis=2).astype(BF16)
    c_dense = jnp.concatenate([c_slab(c_re.astype(F32)), -c_slab(c_im.astype(F32))], axis=1).astype(BF16)

    def flat(t):
        return t.reshape(n_slab, gl * S5_STATE)

    def power(k):
        kk = k.astype(F32)[None, :, None]
        m = jnp.exp(flat(ar * dt)[:, None, :] * kk)
        ph = flat(ai * dt)[:, None, :] * kk
        return jnp.concatenate([m * jnp.cos(ph), m * jnp.sin(ph)], axis=2)

    lam = jnp.repeat(power(jnp.array([1])), SUBLANES, axis=1)
    ptab = jnp.repeat(power(jnp.arange(1, sub + 1)), SUBLANES, axis=1)
    apow = jnp.repeat(power(jnp.array([sub, 2 * sub, 4 * sub])), SUBLANES, axis=1)
    return b_dense, c_dense, lam, ptab, apow


def _s5_mixer(u, a_re, a_im, log_step, b_re, b_im, c_re, c_im, d_skip, sub=S5_SUB):
    seq, d = u.shape
    rows = SUBLANES * sub
    n_slab = d // S5_SLAB
    n2 = 2 * S5_SLAB_STATES
    b_dense, c_dense, lam, ptab, apow = _s5_tables(a_re, a_im, log_step, b_re, b_im, c_re, c_im, sub)
    return pl.pallas_call(
        functools.partial(_s5_kernel, sub=sub),
        out_shape=jax.ShapeDtypeStruct((seq, d), BF16),
        grid=(n_slab // S5_PAIR, seq // rows),
        in_specs=[
            pl.BlockSpec((rows, S5_PAIR * S5_SLAB), lambda k, c: (c, k)),
            pl.BlockSpec((S5_PAIR, S5_SLAB, n2), lambda k, c: (k, 0, 0)),
            pl.BlockSpec((S5_PAIR, n2, S5_SLAB), lambda k, c: (k, 0, 0)),
            pl.BlockSpec((S5_PAIR, SUBLANES, n2), lambda k, c: (k, 0, 0)),
            pl.BlockSpec((S5_PAIR, rows, n2), lambda k, c: (k, 0, 0)),
            pl.BlockSpec((S5_PAIR, 3 * SUBLANES, n2), lambda k, c: (k, 0, 0)),
            pl.BlockSpec((1, S5_PAIR * S5_SLAB), lambda k, c: (0, k)),
        ],
        out_specs=pl.BlockSpec((rows, S5_PAIR * S5_SLAB), lambda k, c: (c, k)),
        scratch_shapes=[pltpu.VMEM((S5_PAIR, rows, n2), F32),
                        pltpu.VMEM((S5_PAIR, SUBLANES, n2), F32),
                        pltpu.VMEM((S5_PAIR, rows, S5_SLAB), F32),
                        pltpu.VMEM((S5_PAIR * S5_SLAB // LANES, rows, LANES), F32)],
        compiler_params=_params("arbitrary", "arbitrary"),
        name="s5_scan",
    )(u, b_dense, c_dense, lam, ptab, apow, d_skip.reshape(1, d).astype(F32))


def _cache_weights(first, pairs):
    @pl.when(first)
    def _():
        for src, dst in pairs:
            dst[...] = src[...].astype(BF16)


def _row_tiles(ref):
    return [slice(r0, r0 + SUB_ROWS) for r0 in range(0, ref.shape[0], SUB_ROWS)]


def _glu_kernel(a_ref, wa_ref, wb_ref, r_ref, o_ref, wa_s, wb_s):
    _cache_weights(pl.program_id(1) == 0, ((wa_ref, wa_s), (wb_ref, wb_s)))
    for rows in _row_tiles(a_ref):
        a = a_ref[rows, :]
        va = jnp.dot(a, wa_s[...], preferred_element_type=F32)
        vb = jnp.dot(a, wb_s[...], preferred_element_type=F32)
        o_ref[rows, :] = r_ref[rows, :] + va * jax.nn.sigmoid(vb)


def _glu_residual(a, w, res, tm=1024, tn=512):
    m, k = a.shape
    n = w.shape[1] // 2
    nb = n // tn
    return pl.pallas_call(
        _glu_kernel,
        out_shape=jax.ShapeDtypeStruct((m, n), F32),
        grid=(nb, m // tm),
        in_specs=[pl.BlockSpec((tm, k), lambda j, i: (i, 0)),
                  pl.BlockSpec((k, tn), lambda j, i: (0, j)),
                  pl.BlockSpec((k, tn), lambda j, i: (0, j + nb)),
                  pl.BlockSpec((tm, tn), lambda j, i: (i, j))],
        out_specs=pl.BlockSpec((tm, tn), lambda j, i: (i, j)),
        scratch_shapes=[pltpu.VMEM((k, tn), BF16), pltpu.VMEM((k, tn), BF16)],
        compiler_params=_params("arbitrary", "arbitrary"),
        name="glu_residual",
    )(a, w, w, res)


def _swiglu_up_kernel(a_ref, wg_ref, wu_ref, o_ref, wg_s, wu_s):
    _cache_weights(pl.program_id(1) == 0, ((wg_ref, wg_s), (wu_ref, wu_s)))
    for rows in _row_tiles(a_ref):
        a = a_ref[rows, :]
        vg = jnp.dot(a, wg_s[...], preferred_element_type=F32)
        vu = jnp.dot(a, wu_s[...], preferred_element_type=F32)
        o_ref[rows, :] = (jax.nn.silu(vg) * vu).astype(o_ref.dtype)


def _swiglu_up(a, w_gu, tm=1024, tn=512):
    m, k = a.shape
    f = w_gu.shape[1] // 2
    nb = f // tn
    return pl.pallas_call(
        _swiglu_up_kernel,
        out_shape=jax.ShapeDtypeStruct((m, f), BF16),
        grid=(nb, m // tm),
        in_specs=[pl.BlockSpec((tm, k), lambda j, i: (i, 0)),
                  pl.BlockSpec((k, tn), lambda j, i: (0, j)),
                  pl.BlockSpec((k, tn), lambda j, i: (0, j + nb))],
        out_specs=pl.BlockSpec((tm, tn), lambda j, i: (i, j)),
        scratch_shapes=[pltpu.VMEM((k, tn), BF16), pltpu.VMEM((k, tn), BF16)],
        compiler_params=_params("arbitrary", "arbitrary"),
        name="swiglu_up",
    )(a, w_gu, w_gu)


def _mm_res_kernel(a_ref, w_ref, r_ref, o_ref, w_s):
    _cache_weights(pl.program_id(1) == 0, ((w_ref, w_s),))
    for rows in _row_tiles(a_ref):
        o_ref[rows, :] = r_ref[rows, :] + jnp.dot(a_ref[rows, :], w_s[...], preferred_element_type=F32)


def _matmul_residual(a, w, res, tm=512, tn=512):
    m, k = a.shape
    n = w.shape[1]
    return pl.pallas_call(
        _mm_res_kernel,
        out_shape=jax.ShapeDtypeStruct((m, n), F32),
        grid=(n // tn, m // tm),
        in_specs=[pl.BlockSpec((tm, k), lambda j, i: (i, 0)),
                  pl.BlockSpec((k, tn), lambda j, i: (0, j)),
                  pl.BlockSpec((tm, tn), lambda j, i: (i, j))],
        out_specs=pl.BlockSpec((tm, tn), lambda j, i: (i, j)),
        scratch_shapes=[pltpu.VMEM((k, tn), BF16)],
        compiler_params=_params("arbitrary", "arbitrary"),
        name="matmul_residual",
    )(a, w, res)


def _rope_kernel(pos_ref, inv_ref, cos_ref, sa_ref, sb_ref):
    ang = pos_ref[...].astype(F32) * inv_ref[...]
    c = jnp.cos(ang)
    s = jnp.sin(ang)
    lane = lax.broadcasted_iota(jnp.int32, ang.shape, 1)
    first_half = lane < ROPE_DIM // 2
    cos_ref[...] = c
    sa_ref[...] = jnp.where(first_half, -s, 0.0)
    sb_ref[...] = jnp.where(first_half, 0.0, s)


def _rope_tables(pos, tm=512):
    n = pos.shape[0]
    tm = min(tm, n)
    half = ROPE_DIM // 2
    inv = jnp.power(ROPE_THETA, -jnp.arange(half, dtype=F32) / half)
    inv = jnp.concatenate([inv, inv, jnp.zeros((LANES - ROPE_DIM,), F32)]).reshape(1, LANES)
    spec = pl.BlockSpec((tm, LANES), lambda i: (i, 0))
    return pl.pallas_call(
        _rope_kernel,
        out_shape=[jax.ShapeDtypeStruct((n, LANES), F32)] * 3,
        grid=(n // tm,),
        in_specs=[pl.BlockSpec((tm, 1), lambda i: (i, 0)),
                  pl.BlockSpec((1, LANES), lambda i: (0, 0))],
        out_specs=[spec, spec, spec],
        compiler_params=_params("arbitrary"),
        name="rope_tables",
    )(pos.reshape(n, 1), inv)


def _rope(y, c, sa, sb):
    half = ROPE_DIM // 2
    return y * c + pltpu.roll(y, LANES - half, 1) * sa + pltpu.roll(y, half, 1) * sb


def _head_norm(x, gain):
    return x * lax.rsqrt(jnp.mean(x * x, axis=-1, keepdims=True) + RMS_EPS) * gain


def _nsa_proj_kernel(a_ref, w_ref, gain_ref, cos_ref, sa_ref, sb_ref, o_ref, w_s, *, norm_tiles, sub_rows):
    j = pl.program_id(0)
    _cache_weights(pl.program_id(1) == 0, ((w_ref, w_s),))
    is_norm = functools.reduce(jnp.logical_or, [j == t for t in norm_tiles])

    @pl.when(is_norm)
    def _():
        gain = gain_ref[0]
        for r0 in range(0, a_ref.shape[0], sub_rows):
            rows = slice(r0, r0 + sub_rows)
            acc = jnp.dot(a_ref[rows, :], w_s[...], preferred_element_type=F32)
            c, sa, sb = cos_ref[rows, :], sa_ref[rows, :], sb_ref[rows, :]
            for hh in range(acc.shape[1] // HEAD_DIM):
                sl = slice(HEAD_DIM * hh, HEAD_DIM * (hh + 1))
                o_ref[rows, sl] = _rope(_head_norm(acc[:, sl], gain), c, sa, sb).astype(o_ref.dtype)

    @pl.when(jnp.logical_not(is_norm))
    def _():
        o_ref[...] = jnp.dot(a_ref[...], w_s[...], preferred_element_type=F32).astype(o_ref.dtype)


def _nsa_proj(u, w_in, k_gain, rope, tm=1024, tn=512, sub_rows=256):
    m, k = u.shape
    q_dim = GQA_GROUP * N_KV_HEADS * HEAD_DIM
    kv_dim = N_KV_HEADS * HEAD_DIM
    assert kv_dim == tn
    n_q = q_dim // tn
    n_tiles = 6
    ones = jnp.ones((HEAD_DIM,), F32)
    gains = jnp.stack([ones, ones, k_gain[1], ones, k_gain[2], ones]).reshape(n_tiles, 1, HEAD_DIM)
    tab = pl.BlockSpec((tm, LANES), lambda j, i: (i, 0))
    return pl.pallas_call(
        functools.partial(_nsa_proj_kernel, norm_tiles=(2, 4), sub_rows=min(sub_rows, tm)),
        out_shape=jax.ShapeDtypeStruct((m, n_tiles * tn), BF16),
        grid=(n_tiles, m // tm),
        in_specs=[pl.BlockSpec((tm, k), lambda j, i: (i, 0)),
                  pl.BlockSpec((k, tn), lambda j, i: (0, j + n_q)),
                  pl.BlockSpec((1, 1, HEAD_DIM), lambda j, i: (j, 0, 0)),
                  tab, tab, tab],
        out_specs=pl.BlockSpec((tm, tn), lambda j, i: (i, j)),
        scratch_shapes=[pltpu.VMEM((k, tn), BF16)],
        compiler_params=_params("arbitrary", "arbitrary"),
        name="nsa_proj",
    )(u, w_in, gains, *rope)


def _nsa_q_kernel(a_ref, w_ref, gain_ref, cos_ref, sa_ref, sb_ref, o_ref, w_s, *, sub_rows):
    _cache_weights(pl.program_id(1) == 0, ((w_ref, w_s),))
    gain = gain_ref[...]
    for r0 in range(0, a_ref.shape[0], sub_rows):
        rows = slice(r0, r0 + sub_rows)
        acc = jnp.dot(a_ref[rows, :], w_s[...], preferred_element_type=F32)
        c, sa, sb = cos_ref[rows, :], sa_ref[rows, :], sb_ref[rows, :]
        for g in range(GQA_GROUP):
            y = _rope(_head_norm(acc[:, HEAD_DIM * g:HEAD_DIM * (g + 1)], gain), c, sa, sb)
            for b in range(sub_rows // Q_BLOCK):
                o_ref[0, r0 // Q_BLOCK + b, :, Q_BLOCK * g:Q_BLOCK * (g + 1)] = (
                    y[Q_BLOCK * b:Q_BLOCK * (b + 1), :].T.astype(o_ref.dtype))


def _nsa_q_proj(u, w_in, q_gain, rope, tm=1024, sub_rows=256):
    m, k = u.shape
    tn = GQA_GROUP * HEAD_DIM
    cols = GQA_GROUP * Q_BLOCK
    q_scaled = (q_gain.astype(F32) * (HEAD_DIM ** -0.5 * math.log2(math.e))).reshape(1, HEAD_DIM)
    tab = pl.BlockSpec((tm, LANES), lambda j, i: (i, 0))
    return pl.pallas_call(
        functools.partial(_nsa_q_kernel, sub_rows=min(sub_rows, tm)),
        out_shape=jax.ShapeDtypeStruct((N_KV_HEADS, m // Q_BLOCK, HEAD_DIM, cols), BF16),
        grid=(N_KV_HEADS, m // tm),
        in_specs=[pl.BlockSpec((tm, k), lambda j, i: (i, 0)),
                  pl.BlockSpec((k, tn), lambda j, i: (0, j)),
                  pl.BlockSpec((1, HEAD_DIM), lambda j, i: (0, 0)),
                  tab, tab, tab],
        out_specs=pl.BlockSpec((1, tm // Q_BLOCK, HEAD_DIM, cols), lambda j, i: (j, i, 0, 0)),
        scratch_shapes=[pltpu.VMEM((k, tn), BF16)],
        compiler_params=_params("arbitrary", "arbitrary"),
        name="nsa_q_proj",
    )(u, w_in, q_scaled, *rope)


def _gate_kernel(a_ref, w_ref, o_ref):
    o_ref[...] = jax.nn.sigmoid(jnp.dot(a_ref[...], w_ref[...].astype(BF16), preferred_element_type=F32))


def _nsa_gates(u, w_gate, tm=512):
    m, k = u.shape
    n = w_gate.shape[1]
    w_pad = jnp.pad(w_gate, ((0, 0), (0, LANES - n)))
    return pl.pallas_call(
        _gate_kernel,
        out_shape=jax.ShapeDtypeStruct((m, LANES), F32),
        grid=(m // tm,),
        in_specs=[pl.BlockSpec((tm, k), lambda i: (i, 0)),
                  pl.BlockSpec((k, LANES), lambda i: (0, 0))],
        out_specs=pl.BlockSpec((tm, LANES), lambda i: (i, 0)),
        compiler_params=_params("arbitrary"),
        name="nsa_gates",
    )(u, w_pad)


def _compress_kernel(*refs, is_key):
    if is_key:
        ca_ref, cb_ref, pe_ref, w1_ref, w2_ref, gain_ref, cos_ref, sa_ref, sb_ref, o_ref = refs
    else:
        ca_ref, cb_ref, pe_ref, w1_ref, w2_ref, o_ref = refs
    half = w1_ref.shape[0] // 2
    pe = pe_ref[...]
    xa = (ca_ref[0].astype(F32) + pe[:, :half]).astype(BF16)
    xb = (cb_ref[0].astype(F32) + pe[:, half:]).astype(BF16)
    hid = (jnp.dot(xa, w1_ref[:half, :].astype(BF16), preferred_element_type=F32)
           + jnp.dot(xb, w1_ref[half:, :].astype(BF16), preferred_element_type=F32))
    out = jnp.dot(jax.nn.gelu(hid).astype(BF16), w2_ref[...].astype(BF16), preferred_element_type=F32)
    if is_key:
        out = _rope(_head_norm(out, gain_ref[...]), cos_ref[...], sa_ref[...], sb_ref[...])
    o_ref[0] = out.astype(o_ref.dtype)


def _compress(t, pe, w1, w2, key_extras=None):
    seq = t.shape[0]
    nc = seq // CMP_STRIDE
    width = CMP_STRIDE * HEAD_DIM
    ca = t.reshape(nc, CMP_STRIDE, N_KV_HEADS, HEAD_DIM).transpose(2, 0, 1, 3).reshape(N_KV_HEADS, nc, width)
    cb = jnp.concatenate([ca[:, 1:], jnp.zeros((N_KV_HEADS, 1, width), ca.dtype)], axis=1)
    blk = pl.BlockSpec((1, nc, width), lambda h: (h, 0, 0))
    full = lambda a: pl.BlockSpec(a.shape, lambda h: (0,) * a.ndim)
    args = [ca, cb, pe.reshape(1, CMP_BLOCK * HEAD_DIM), w1, w2]
    if key_extras is not None:
        args += list(key_extras)
    return pl.pallas_call(
        functools.partial(_compress_kernel, is_key=key_extras is not None),
        out_shape=jax.ShapeDtypeStruct((N_KV_HEADS, nc, HEAD_DIM), BF16),
        grid=(N_KV_HEADS,),
        in_specs=[blk, blk] + [full(a) for a in args[2:]],
        out_specs=pl.BlockSpec((1, nc, HEAD_DIM), lambda h: (h, 0, 0)),
        compiler_params=_params("arbitrary"),
        name="nsa_compress_k" if key_extras is not None else "nsa_compress_v",
    )(*args)


def _dot_nt(a, b):
    return lax.dot_general(a, b, (((1,), (1,)), ((), ())), preferred_element_type=F32)


def _split3(x):
    hi = x.astype(BF16)
    r1 = x - hi.astype(F32)
    mid = r1.astype(BF16)
    lo = (r1 - mid.astype(F32)).astype(BF16)
    return hi, mid, lo


def _nsa_attn_kernel(q_ref, kc_ref, vct_ref, ks_ref, vst_ref, kw_ref, vwt_ref, gate_ref, blk_ref, o_ref,
                     acc_ref, mix_ref, m_ref, l_ref, qa_ref, s0_ref, s1_ref, pc_ref, sw_ref, pw_ref,
                     *, seq, tk):
    t0 = pl.program_id(1) * Q_BLOCK
    nc = kc_ref.shape[1]
    ns = seq // SEL_BLOCK
    grp = GQA_GROUP
    cols = grp * Q_BLOCK
    sel_shift = int(math.log2(SEL_BLOCK))
    qt = q_ref[0, 0]
    t_row = t0 + lax.broadcasted_iota(jnp.int32, (1, Q_BLOCK), 1)

    def heads(x):
        return jnp.concatenate([x] * grp, axis=1)

    span = WINDOW + Q_BLOCK
    w0 = pl.multiple_of(jnp.maximum(t0 - WINDOW, 0), Q_BLOCK)
    gate = gate_ref[0, 0]
    n_idx = lax.broadcasted_iota(jnp.int32, (nc, Q_BLOCK), 0)
    ok_c = (n_idx * CMP_STRIDE + (CMP_BLOCK - 1) <= t_row) & (n_idx < nc - 1)
    sb_c = jnp.dot(kc_ref[0], qt, preferred_element_type=F32) + heads(jnp.where(ok_c, 0.0, NEG))
    m_c = jnp.max(sb_c, axis=0, keepdims=True)
    e_c = jnp.exp2(sb_c - m_c)
    den_c = jnp.maximum(jnp.sum(e_c, axis=0, keepdims=True), 1e-30)
    p_c = e_c * jnp.where(m_c > 0.5 * NEG, 1.0 / den_c, 0.0)
    pc_ref[...] = p_c.astype(BF16)
    imp = p_c[:, :Q_BLOCK]
    for g in range(1, grp):
        imp = imp + p_c[:, Q_BLOCK * g:Q_BLOCK * (g + 1)]

    ratio = SEL_BLOCK // CMP_STRIDE
    d = (lax.broadcasted_iota(jnp.int32, (ns, nc), 1)
         - ratio * lax.broadcasted_iota(jnp.int32, (ns, nc), 0))
    overlap = jnp.zeros((ns, nc), F32)
    for n in range(CMP_BLOCK // CMP_STRIDE):
        overlap = overlap + jnp.where((d - n >= 0) & (d - n < ratio), 1.0, 0.0)
    overlap = overlap.astype(BF16)
    p_slc = sum(jnp.dot(overlap, part, preferred_element_type=F32) for part in _split3(imp))

    mix_ref[...] = gate[0:1] * jnp.dot(vct_ref[0], pc_ref[...], preferred_element_type=F32)
    rel = t_row - (w0 + lax.broadcasted_iota(jnp.int32, (span, Q_BLOCK), 0))
    sw_ref[...] = (jnp.dot(kw_ref[pl.ds(w0, span), :], qt, preferred_element_type=F32)
                   + heads(jnp.where((rel >= 0) & (rel < WINDOW), 0.0, NEG)))
    m_w = jnp.max(sw_ref[...], axis=0, keepdims=True)

    j_idx = lax.broadcasted_iota(jnp.int32, (ns, Q_BLOCK), 0)
    j_f = j_idx.astype(F32)
    dist = jnp.right_shift(t_row, sel_shift) - j_idx
    forced = (dist >= 0) & (dist < SEL_LOCAL)
    first_blk = j_idx == 0
    score = jnp.where(first_blk, -jnp.inf, jnp.where(forced, jnp.inf, jnp.where(dist >= 0, p_slc, -jnp.inf)))
    sel = jnp.where(first_blk, 1.0, 0.0)
    k_top = min(SEL_TOPK, ns) - 1
    pack = 2 * SUBLANES
    cuts = [span // pack * r // k_top * pack for r in range(k_top + 1)]
    den_w = jnp.zeros((1, cols), F32)
    for r in range(k_top):
        top = jnp.max(score, axis=0, keepdims=True)
        idx = jnp.min(jnp.where(score == top, j_f, float(ns)), axis=0, keepdims=True)
        pick = j_f == idx
        sel = jnp.where(pick, 1.0, sel)
        score = jnp.where(pick, -jnp.inf, score)
        if cuts[r + 1] > cuts[r]:
            e_w = jnp.exp2(sw_ref[cuts[r]:cuts[r + 1], :] - m_w)
            den_w = den_w + jnp.sum(e_w, axis=0, keepdims=True)
            pw_ref[cuts[r]:cuts[r + 1], :] = e_w.astype(BF16)
    o_w = jnp.dot(vwt_ref[0, :, pl.ds(w0, span)], pw_ref[...], preferred_element_type=F32)
    mix_ref[...] += (gate[2:3] * (1.0 / den_w)) * o_w

    qa_ref[...] = jnp.concatenate([qt, heads(jnp.where(sel > 0.0, 0.0, NEG).astype(BF16))], axis=0)
    acc_ref[...] = jnp.zeros_like(acc_ref)
    m_ref[...] = jnp.full_like(m_ref, NEG)
    l_ref[...] = jnp.zeros_like(l_ref)

    def score_tile(kt, dst):
        k0 = pl.multiple_of(kt * tk, tk)
        k_aug = jnp.concatenate([ks_ref[pl.ds(k0, tk), :], blk_ref[pl.ds(k0, tk), :]], axis=1)
        dst[...] = jnp.dot(k_aug, qa_ref[...], preferred_element_type=F32)

    def consume_tile(kt, src, diagonal):
        k0 = pl.multiple_of(kt * tk, tk)
        sb = src[...]
        if diagonal:
            kpos = k0 + lax.broadcasted_iota(jnp.int32, (tk, Q_BLOCK), 0)
            sb = sb + heads(jnp.where(kpos <= t_row, 0.0, NEG))
        m_i = m_ref[...]
        m_new = jnp.maximum(m_i, jnp.max(sb, axis=0, keepdims=True))
        e = jnp.exp2(sb - m_new)
        alpha = jnp.exp2(m_i - m_new)
        m_ref[...] = m_new
        l_ref[...] = alpha * l_ref[...] + jnp.sum(e, axis=0, keepdims=True)
        acc_ref[...] = alpha * acc_ref[...] + jnp.dot(vst_ref[0, :, pl.ds(k0, tk)], e.astype(BF16),
                                                      preferred_element_type=F32)

    last = t0 // tk
    score_tile(0, s0_ref)

    def tile_pair(i, _):
        score_tile(2 * i + 1, s1_ref)
        consume_tile(2 * i, s0_ref, False)
        score_tile(2 * i + 2, s0_ref)
        consume_tile(2 * i + 1, s1_ref, False)
        return 0

    lax.fori_loop(0, last // 2, tile_pair, 0)

    @pl.when(last % 2 == 1)
    def _():
        score_tile(last, s1_ref)
        consume_tile(last - 1, s0_ref, False)
        consume_tile(last, s1_ref, True)

    @pl.when(last % 2 == 0)
    def _():
        consume_tile(last, s0_ref, True)

    o_s = acc_ref[...] * (1.0 / jnp.maximum(l_ref[...], 1e-30))

    mixed = mix_ref[...] + gate_ref[0, 0, 1:2] * o_s
    for g in range(grp):
        o_ref[:, HEAD_DIM * g:HEAD_DIM * (g + 1)] = mixed[:, Q_BLOCK * g:Q_BLOCK * (g + 1)].T.astype(o_ref.dtype)


def _nsa_attention(q_t, proj, kcmp, vcmp_t, vsl_t, vw_t, gates, tk=1024):
    seq = proj.shape[0]
    tk = min(tk, seq)
    q_dim = GQA_GROUP * N_KV_HEADS * HEAD_DIM
    kv_blocks = N_KV_HEADS
    first = 2 * kv_blocks
    nc = kcmp.shape[1]
    cols = GQA_GROUP * Q_BLOCK

    def k_spec(which):
        return pl.BlockSpec((seq, HEAD_DIM), lambda h, qb: (0, first + which * kv_blocks + h))

    def vt_spec(n):
        return pl.BlockSpec((1, HEAD_DIM, n), lambda h, qb: (h, 0, 0))

    o_spec = pl.BlockSpec((Q_BLOCK, GQA_GROUP * HEAD_DIM), lambda h, qb: (qb, h))
    q_spec = pl.BlockSpec((1, 1, HEAD_DIM, cols), lambda h, qb: (h, qb, 0, 0))
    ns = seq // SEL_BLOCK
    key_block = (jnp.arange(seq, dtype=jnp.int32)[:, None] // SEL_BLOCK
                 == jnp.arange(ns, dtype=jnp.int32)[None, :]).astype(BF16)
    return pl.pallas_call(
        functools.partial(_nsa_attn_kernel, seq=seq, tk=tk),
        out_shape=jax.ShapeDtypeStruct((seq, q_dim), BF16),
        grid=(N_KV_HEADS, seq // Q_BLOCK),
        in_specs=[q_spec, pl.BlockSpec((1, nc, HEAD_DIM), lambda h, qb: (h, 0, 0)), vt_spec(nc),
                  k_spec(0), vt_spec(seq), k_spec(2), vt_spec(seq),
                  pl.BlockSpec((1, 1, 3, cols), lambda h, qb: (h, qb, 0, 0)),
                  pl.BlockSpec((seq, ns), lambda h, qb: (0, 0))],
        out_specs=o_spec,
        scratch_shapes=[pltpu.VMEM((HEAD_DIM, cols), F32), pltpu.VMEM((HEAD_DIM, cols), F32),
                        pltpu.VMEM((1, cols), F32), pltpu.VMEM((1, cols), F32),
                        pltpu.VMEM((HEAD_DIM + ns, cols), BF16),
                        pltpu.VMEM((tk, cols), F32), pltpu.VMEM((tk, cols), F32),
                        pltpu.VMEM((nc, cols), BF16), pltpu.VMEM((WINDOW + Q_BLOCK, cols), F32),
                        pltpu.VMEM((WINDOW + Q_BLOCK, cols), BF16)],
        compiler_params=_params("arbitrary", "arbitrary"),
        name="nsa_attention",
    )(q_t, kcmp, vcmp_t, proj, vsl_t, proj, vw_t, gates, key_block)


def _nsa_mixer(u, positions, w_in, q_gain, k_gain, pe_k, pe_v, ck_w1, ck_w2, cv_w1, cv_w2):
    seq = u.shape[0]
    q_dim = GQA_GROUP * N_KV_HEADS * HEAD_DIM
    kv_dim = N_KV_HEADS * HEAD_DIM
    n_main = q_dim + 6 * kv_dim
    nc = seq // CMP_STRIDE
    rope = _rope_tables(positions)
    q_t = _nsa_q_proj(u, w_in, q_gain, rope)
    proj = _nsa_proj(u, w_in, k_gain, rope)
    gate = _nsa_gates(u, w_in[:, n_main:])
    gates = (gate[:, :3 * N_KV_HEADS * GQA_GROUP].reshape(seq // Q_BLOCK, Q_BLOCK, 3, N_KV_HEADS, GQA_GROUP)
             .transpose(3, 0, 2, 4, 1).reshape(N_KV_HEADS, seq // Q_BLOCK, 3, GQA_GROUP * Q_BLOCK))

    def keys_last(cols):
        return cols.reshape(seq, N_KV_HEADS, HEAD_DIM).transpose(1, 2, 0)

    pos_cmp = jnp.concatenate([positions[CMP_BLOCK - 1::CMP_STRIDE][:nc - 1], jnp.zeros((1,), positions.dtype)])
    rope_cmp = _rope_tables(pos_cmp)
    kcmp = _compress(proj[:, :kv_dim], pe_k, ck_w1, ck_w2,
                     key_extras=(k_gain[0].reshape(1, HEAD_DIM),) + tuple(rope_cmp))
    vcmp = _compress(proj[:, kv_dim:2 * kv_dim], pe_v, cv_w1, cv_w2)
    vsl_t = keys_last(proj[:, 3 * kv_dim:4 * kv_dim])
    vw_t = keys_last(proj[:, 5 * kv_dim:6 * kv_dim])
    return _nsa_attention(q_t, proj, kcmp, vcmp.transpose(0, 2, 1), vsl_t, vw_t, gates)


def _router_kernel(x_ref, g_ref, w_ref, b_ref, u_ref, r_ref):
    x = x_ref[...]
    u = x * lax.rsqrt(jnp.mean(x * x, axis=-1, keepdims=True) + RMS_EPS) * g_ref[...]
    u_ref[...] = u
    uh, um, _ = _split3(u)
    wh, wm, _ = _split3(w_ref[...])
    logits = (jnp.dot(uh, wh, preferred_element_type=F32) + jnp.dot(uh, wm, preferred_element_type=F32)
              + jnp.dot(um, wh, preferred_element_type=F32)) + b_ref[...]
    lane = lax.broadcasted_iota(jnp.int32, logits.shape, 1).astype(F32)
    lg = jnp.where(lane < N_EXPERTS, logits, -jnp.inf)
    v1 = jnp.max(lg, axis=-1, keepdims=True)
    i1 = jnp.min(jnp.where(lg == v1, lane, float(LANES)), axis=-1, keepdims=True)
    lg = jnp.where(lane == i1, -jnp.inf, lg)
    v2 = jnp.max(lg, axis=-1, keepdims=True)
    i2 = jnp.min(jnp.where(lg == v2, lane, float(LANES)), axis=-1, keepdims=True)
    e2 = jnp.exp(v2 - v1)
    den = 1.0 + e2
    r_ref[...] = jnp.where(lane == 0, i1, jnp.where(lane == 1, i2, jnp.where(
        lane == 2, 1.0 / den, jnp.where(lane == 3, e2 / den, 0.0))))


def _router(h, gain, w_router, b_router, tm=256):
    m, d = h.shape
    w_pad = jnp.pad(w_router.astype(F32), ((0, 0), (0, LANES - N_EXPERTS)))
    b_pad = jnp.pad(b_router.astype(F32), (0, LANES - N_EXPERTS)).reshape(1, LANES)
    return pl.pallas_call(
        _router_kernel,
        out_shape=[jax.ShapeDtypeStruct((m, d), F32), jax.ShapeDtypeStruct((m, LANES), F32)],
        grid=(m // tm,),
        in_specs=[pl.BlockSpec((tm, d), lambda i: (i, 0)),
                  pl.BlockSpec((1, d), lambda i: (0, 0)),
                  pl.BlockSpec((d, LANES), lambda i: (0, 0)),
                  pl.BlockSpec((1, LANES), lambda i: (0, 0))],
        out_specs=[pl.BlockSpec((tm, d), lambda i: (i, 0)), pl.BlockSpec((tm, LANES), lambda i: (i, 0))],
        compiler_params=_params("arbitrary"),
        name="moe_router",
    )(h, gain.reshape(1, d), w_pad, b_pad)


def _row_copy(src_hbm, row, dst, r, sem):
    return pltpu.make_async_copy(src_hbm.at[pl.ds(row, 1), :], dst.at[pl.ds(r, 1), :], sem)


def _gather_kernel(idx_ref, used_ref, src_hbm, o_ref, buf, sem):
    rows = o_ref.shape[0]
    i = pl.program_id(0)
    n_used = used_ref[0]

    def issue(blk):
        slot = blk % 2

        def start(r, _):
            _row_copy(src_hbm, idx_ref[blk * rows + r], buf.at[slot], r, sem.at[slot]).start()
            return 0

        lax.fori_loop(0, rows, start, 0, unroll=DMA_UNROLL)

    @pl.when(i == 0)
    def _():
        issue(i)

    @pl.when(i + 1 < n_used)
    def _():
        issue(i + 1)

    @pl.when(i < n_used)
    def _():
        slot = i % 2

        def wait(r, _):
            _row_copy(src_hbm, 0, buf.at[slot], r, sem.at[slot]).wait()
            return 0

        lax.fori_loop(0, rows, wait, 0, unroll=DMA_UNROLL)
        o_ref[...] = buf[slot].astype(o_ref.dtype)

    @pl.when(i >= n_used)
    def _():
        o_ref[...] = jnp.zeros_like(o_ref)


def _gather_rows(src, idx, n_used, out_dtype, rows=MOE_ROWS):
    n = idx.shape[0]
    d = src.shape[1]
    return pl.pallas_call(
        _gather_kernel,
        out_shape=jax.ShapeDtypeStruct((n, d), out_dtype),
        grid_spec=pltpu.PrefetchScalarGridSpec(
            num_scalar_prefetch=2,
            grid=(n // rows,),
            in_specs=[pl.BlockSpec(memory_space=pl.ANY)],
            out_specs=pl.BlockSpec((rows, d), lambda i, idx, used: (i, 0)),
            scratch_shapes=[pltpu.VMEM((2, rows, d), src.dtype), pltpu.SemaphoreType.DMA((2,))]),
        compiler_params=_params("arbitrary"),
        name="moe_gather",
    )(idx, n_used, src)


def _block_state(be_ref, used_ref, i):
    changed = (i == 0) | (be_ref[i] != be_ref[jnp.maximum(i - 1, 0)])
    used = i < used_ref[0]
    return used, used & changed


def _last_used(i, used):
    return jnp.minimum(i, used[0] - 1)


def _stream_expert_weights(first, be_ref, rix_ref, rune_ref, nrun_ref, n_tiles, copies, cast):
    j = pl.program_id(0)
    i = pl.program_id(1)

    @pl.when(first)
    def _():
        rix = rix_ref[i]
        n_runs = nrun_ref[0]
        g = j * n_runs + rix
        slot = g % 2

        @pl.when(g == 0)
        def _():
            for c in copies(be_ref[i], j, slot):
                c.start()

        more = rix + 1 < n_runs
        e_next = jnp.where(more, rune_ref[jnp.minimum(rix + 1, N_EXPERTS - 1)], rune_ref[0])
        j_next = jnp.where(more, j, j + 1)

        @pl.when(j_next < n_tiles)
        def _():
            for c in copies(e_next, j_next, 1 - slot):
                c.start()

        for c in copies(be_ref[i], j, slot):
            c.wait()
        cast(slot)


def _moe_up_kernel(be_ref, used_ref, rix_ref, rune_ref, nrun_ref, x_ref, w_hbm, o_ref, wbuf, wg_s, wu_s, sem,
                   *, nb, tn):
    used, first = _block_state(be_ref, used_ref, pl.program_id(1))

    def copies(e, jj, slot):
        return [pltpu.make_async_copy(w_hbm.at[e, :, pl.ds(pl.multiple_of((jj + m * nb) * tn, tn), tn)],
                                      wbuf.at[slot, m], sem.at[slot, m]) for m in range(2)]

    def cast(slot):
        wg_s[...] = wbuf[slot, 0].astype(BF16)
        wu_s[...] = wbuf[slot, 1].astype(BF16)

    _stream_expert_weights(first, be_ref, rix_ref, rune_ref, nrun_ref, nb, copies, cast)

    @pl.when(used)
    def _():
        for rows in _row_tiles(x_ref):
            a = x_ref[rows, :]
            vg = jnp.dot(a, wg_s[...], preferred_element_type=F32)
            vu = jnp.dot(a, wu_s[...], preferred_element_type=F32)
            o_ref[rows, :] = (jax.nn.silu(vg) * vu).astype(o_ref.dtype)

    @pl.when(jnp.logical_not(used))
    def _():
        o_ref[...] = jnp.zeros_like(o_ref)


def _moe_up(x_rows, sched, w_gu, tn=512, rows=MOE_ROWS):
    n, k = x_rows.shape
    f = w_gu.shape[2] // 2
    nb = f // tn
    return pl.pallas_call(
        functools.partial(_moe_up_kernel, nb=nb, tn=tn),
        out_shape=jax.ShapeDtypeStruct((n, f), BF16),
        grid_spec=pltpu.PrefetchScalarGridSpec(
            num_scalar_prefetch=len(sched),
            grid=(nb, n // rows),
            in_specs=[pl.BlockSpec((rows, k), lambda j, i, be, nu, *_: (_last_used(i, nu), 0)),
                      pl.BlockSpec(memory_space=pl.ANY)],
            out_specs=pl.BlockSpec((rows, tn), lambda j, i, *_: (i, j)),
            scratch_shapes=[pltpu.VMEM((2, 2, k, tn), F32), pltpu.VMEM((k, tn), BF16),
                            pltpu.VMEM((k, tn), BF16), pltpu.SemaphoreType.DMA((2, 2))]),
        compiler_params=_params("arbitrary", "arbitrary"),
        name="moe_up",
    )(*sched, x_rows, w_gu)


def _moe_down_kernel(be_ref, used_ref, rix_ref, rune_ref, nrun_ref, a_ref, w_hbm, o_ref, wbuf, w_s, sem,
                     *, nb, tn):
    used, first = _block_state(be_ref, used_ref, pl.program_id(1))

    def copies(e, jj, slot):
        return [pltpu.make_async_copy(w_hbm.at[e, :, pl.ds(pl.multiple_of(jj * tn, tn), tn)],
                                      wbuf.at[slot], sem.at[slot])]

    def cast(slot):
        w_s[...] = wbuf[slot].astype(BF16)

    _stream_expert_weights(first, be_ref, rix_ref, rune_ref, nrun_ref, nb, copies, cast)

    @pl.when(used)
    def _():
        o_ref[...] = jnp.dot(a_ref[...], w_s[...], preferred_element_type=F32)

    @pl.when(jnp.logical_not(used))
    def _():
        o_ref[...] = jnp.zeros_like(o_ref)


def _moe_down(act, sched, w_down, tn=512, rows=MOE_ROWS):
    n, k = act.shape
    d = w_down.shape[2]
    nb = d // tn
    return pl.pallas_call(
        functools.partial(_moe_down_kernel, nb=nb, tn=tn),
        out_shape=jax.ShapeDtypeStruct((n, d), F32),
        grid_spec=pltpu.PrefetchScalarGridSpec(
            num_scalar_prefetch=len(sched),
            grid=(nb, n // rows),
            in_specs=[pl.BlockSpec((rows, k), lambda j, i, be, nu, *_: (_last_used(i, nu), 0)),
                      pl.BlockSpec(memory_space=pl.ANY)],
            out_specs=pl.BlockSpec((rows, tn), lambda j, i, *_: (i, j)),
            scratch_shapes=[pltpu.VMEM((2, k, tn), F32), pltpu.VMEM((k, tn), BF16),
                            pltpu.SemaphoreType.DMA((2,))]),
        compiler_params=_params("arbitrary", "arbitrary"),
        name="moe_down",
    )(*sched, act, w_down)


def _combine_kernel(dest_ref, h_ref, r_ref, rows_hbm, o_ref, buf, sem):
    tm = h_ref.shape[0]
    base = pl.program_id(0) * tm

    def start(r, _):
        for k in range(2):
            _row_copy(rows_hbm, dest_ref[2 * (base + r) + k], buf.at[k], r, sem.at[k]).start()
        return 0

    def wait(r, _):
        for k in range(2):
            _row_copy(rows_hbm, 0, buf.at[k], r, sem.at[k]).wait()
        return 0

    lax.fori_loop(0, tm, start, 0, unroll=DMA_UNROLL)
    lax.fori_loop(0, tm, wait, 0, unroll=DMA_UNROLL)
    w = r_ref[...]
    o_ref[...] = h_ref[...] + (w[:, 2:3] * buf[0] + w[:, 3:4] * buf[1])


def _moe_combine(h, route, out_rows, dest, tm=512):
    m, d = h.shape
    return pl.pallas_call(
        _combine_kernel,
        out_shape=jax.ShapeDtypeStruct((m, d), F32),
        grid_spec=pltpu.PrefetchScalarGridSpec(
            num_scalar_prefetch=1,
            grid=(m // tm,),
            in_specs=[pl.BlockSpec((tm, d), lambda i, dest: (i, 0)),
                      pl.BlockSpec((tm, LANES), lambda i, dest: (i, 0)),
                      pl.BlockSpec(memory_space=pl.ANY)],
            out_specs=pl.BlockSpec((tm, d), lambda i, dest: (i, 0)),
            scratch_shapes=[pltpu.VMEM((2, tm, d), F32), pltpu.SemaphoreType.DMA((2,))]),
        compiler_params=_params("arbitrary"),
        name="moe_combine",
    )(dest.reshape(-1), h, route, out_rows)


def _moe_layout(top_e, rows=MOE_ROWS):
    n_tok = top_e.shape[0]
    e_flat = top_e.reshape(-1)
    onehot = (e_flat[:, None] == jnp.arange(N_EXPERTS, dtype=jnp.int32)[None, :]).astype(jnp.int32)
    csum = jnp.cumsum(onehot, axis=0)
    rank = jnp.take_along_axis(csum, e_flat[:, None], axis=1)[:, 0] - 1
    counts = csum[-1]
    padded = (counts + rows - 1) // rows * rows
    pad_end = jnp.cumsum(padded)
    dest = (pad_end - padded)[e_flat] + rank
    n_rows = e_flat.shape[0] + N_EXPERTS * rows
    t_flat = jnp.repeat(jnp.arange(n_tok, dtype=jnp.int32), top_e.shape[1])
    row_tok = jnp.zeros((n_rows,), jnp.int32).at[dest].set(t_flat)
    n_blk = n_rows // rows
    blk_start = jnp.arange(n_blk, dtype=jnp.int32) * rows
    blk_e = jnp.minimum(jnp.sum(blk_start[:, None] >= pad_end[None, :], axis=1), N_EXPERTS - 1).astype(jnp.int32)
    n_used = (pad_end[-1:] // rows).astype(jnp.int32)
    first = (jnp.arange(n_blk) < n_used[0]) & (blk_e != jnp.concatenate([blk_e[:1] - 1, blk_e[:-1]]))
    run_ix = (jnp.cumsum(first) - 1).astype(jnp.int32)
    in_run = first[:, None] & (run_ix[:, None] == jnp.arange(N_EXPERTS, dtype=jnp.int32)[None, :])
    run_e = jnp.sum(jnp.where(in_run, blk_e[:, None], 0), axis=0).astype(jnp.int32)
    n_runs = jnp.sum(first).astype(jnp.int32).reshape(1)
    sched = (blk_e, n_used, run_ix, run_e, n_runs)
    return row_tok, sched, dest.astype(jnp.int32).reshape(n_tok, -1)


def _moe_ffn_residual(h, gain, w_router, b_router, w_gu, w_down):
    u, route = _router(h, gain, w_router, b_router)
    top_e = route[:, :2].astype(jnp.int32)
    row_tok, sched, dest = _moe_layout(top_e)
    x_rows = _gather_rows(u, row_tok, sched[1], BF16)
    act = _moe_up(x_rows, sched, w_gu)
    out_rows = _moe_down(act, sched, w_down)
    return _moe_combine(h, route, out_rows, dest)


def kernel(x, positions, norm_mix, norm_ffn, s5_a_re, s5_a_im, s5_log_step, s5_b_re, s5_b_im, s5_c_re, s5_c_im, s5_d, s5_w_glu, nsa_w_in, nsa_q_gain, nsa_k_gain, nsa_pe_k, nsa_pe_v, nsa_ck_w1, nsa_ck_w2, nsa_cv_w1, nsa_cv_w2, nsa_w_out, ffn_w_gu, ffn_w_down, moe_w_router, moe_b_router, moe_w_gu, moe_w_down):
    bsz, seq, d = x.shape
    assert bsz == 1, "the scan and attention kernels take one sequence"
    h = x.reshape(seq, d)
    h = _layer_s5(h, norm_mix[0], norm_ffn[0], s5_a_re[0], s5_a_im[0], s5_log_step[0], s5_b_re[0],
                  s5_b_im[0], s5_c_re[0], s5_c_im[0], s5_d[0], s5_w_glu[0], ffn_w_gu[0], ffn_w_down[0])
    h = _layer_nsa(h, positions[0], norm_mix[1], norm_ffn[1], nsa_w_in[0], nsa_q_gain[0], nsa_k_gain[0],
                   nsa_pe_k[0], nsa_pe_v[0], nsa_ck_w1[0], nsa_ck_w2[0], nsa_cv_w1[0], nsa_cv_w2[0],
                   nsa_w_out[0], moe_w_router[0], moe_b_router[0], moe_w_gu[0], moe_w_down[0])
    return h.reshape(bsz, seq, d)


def _layer_nsa(h, positions, g_mix, g_ffn, w_in, q_gain, k_gain, pe_k, pe_v, ck_w1, ck_w2, cv_w1, cv_w2,
               w_out, w_router, b_router, w_gu, w_down):
    u = _rms_norm(h, g_mix, BF16)
    o = _nsa_mixer(u, positions, w_in, q_gain, k_gain, pe_k, pe_v, ck_w1, ck_w2, cv_w1, cv_w2)
    h = _matmul_residual(o, w_out, h, tm=1024)
    return _moe_ffn_residual(h, g_ffn, w_router, b_router, w_gu, w_down)


def _layer_s5(h, g_mix, g_ffn, a_re, a_im, log_step, b_re, b_im, c_re, c_im, d_skip, w_glu, w_gu, w_down):
    u = _rms_norm(h, g_mix, F32)
    g = _s5_mixer(u, a_re, a_im, log_step, b_re, b_im, c_re, c_im, d_skip)
    h = _glu_residual(g, w_glu, h)
    u = _rms_norm(h, g_ffn, BF16)
    act = _swiglu_up(u, w_gu)
    return _matmul_residual(act, w_down, h)
```
